```python
import math
import jax, jax.numpy as jnp
from jax import lax
import numpy as np

D_MODEL = 2048
BATCH = 4
SEQ = 2048
DEPTH = 1

D_MIX = D_MODEL
DA_QK = 64
DA_V = 2 * DA_QK
DA_HEADS = D_MODEL // 256
DA_WIDTH = DA_HEADS * DA_V
ML_QK = 128
ML_V = 256
ML_HEADS = D_MODEL // 512
ML_WIDTH = ML_HEADS * ML_V
CONV_W = 4
ML_CHUNK = 64
GATE_CAP = 15.0
Q_BLOCK = 128
ROPE_THETA = 10000.0
PEER_HEADS = 8
PEER_NKEYS = 128
PEER_EXPERTS = PEER_NKEYS * PEER_NKEYS
PEER_QDIM = 256
PEER_TOPK = 16
PEER_TOK_BLOCK = 128
EPS = 1e-6

PROJ_SIZES = [2 * DA_HEADS * DA_QK, 2 * DA_HEADS * DA_QK, DA_WIDTH,
              ML_HEADS * ML_QK, ML_HEADS * ML_QK, ML_WIDTH, ML_WIDTH,
              ML_HEADS, ML_HEADS]
PROJ_DIM = sum(PROJ_SIZES)
PROJ_SPLITS = [int(s) for s in np.cumsum(PROJ_SIZES)[:-1]]

kernel_name = 'hymba_diffattn_mlstm_peer_adaln'


def rms_norm(x, g):
    xf = x.astype(jnp.float32)
    y = xf * lax.rsqrt(jnp.mean(xf * xf, axis=-1, keepdims=True) + EPS)
    return (y * g.astype(jnp.float32)).astype(x.dtype)


def modulate(h, shift, scale):
    return h * (1 + scale[:, None, :]) + shift[:, None, :]


def rope(x, pos):
    d = x.shape[-1]
    half = d // 2
    inv = ROPE_THETA ** (-jnp.arange(half, dtype=jnp.float32) / half)
    ang = pos.astype(jnp.float32)[:, None] * inv[None, :]
    cos = jnp.cos(ang)[None, :, None, :]
    sin = jnp.sin(ang)[None, :, None, :]
    xf = x.astype(jnp.float32)
    x1, x2 = xf[..., :half], xf[..., half:]
    return jnp.concatenate([x1 * cos - x2 * sin, x2 * cos + x1 * sin], axis=-1).astype(x.dtype)


def causal_dwconv(x, w, b):
    C = x.shape[-1]
    y = lax.conv_general_dilated(x, w[:, None, :].astype(x.dtype), window_strides=(1,),
                                 padding=[(CONV_W - 1, 0)],
                                 dimension_numbers=('NWC', 'WIO', 'NWC'),
                                 feature_group_count=C)
    return y + b


def differential_attention(q, k, v, lam):
    B, S = q.shape[0], q.shape[1]
    nqb = S // Q_BLOCK
    qb = q.reshape(B, nqb, Q_BLOCK, 2 * DA_HEADS, DA_QK).transpose(1, 0, 2, 3, 4)
    starts = jnp.arange(nqb) * Q_BLOCK
    kpos = jnp.arange(S)
    scale = DA_QK ** -0.5

    def one_block(args):
        q_blk, start = args
        s = jnp.einsum('bqhd,bkhd->bhqk', q_blk, k).astype(jnp.float32) * scale
        qpos = start + jnp.arange(Q_BLOCK)
        s = jnp.where(kpos[None, :] <= qpos[:, None], s, -jnp.inf)
        p = jax.nn.softmax(s, axis=-1).reshape(B, DA_HEADS, 2, Q_BLOCK, S)
        a = p[:, :, 0] - lam * p[:, :, 1]
        return jnp.einsum('bhqk,bkhe->bqhe', a.astype(v.dtype), v)

    o = lax.map(one_block, (qb, starts))
    return o.transpose(1, 0, 2, 3, 4).reshape(B, S, DA_HEADS, DA_V)


def mlstm_chunkwise(q, k, v, li, lf):
    B, S, H, dk = q.shape
    dv = v.shape[-1]
    nc = S // ML_CHUNK
    L = ML_CHUNK

    def to_chunks4(t):
        return t.reshape(B, nc, L, H, t.shape[-1]).transpose(1, 0, 3, 2, 4)

    def to_chunks3(t):
        return t.reshape(B, nc, L, H).transpose(1, 0, 3, 2)

    tri = jnp.tril(jnp.ones((L, L), dtype=bool))

    def body(carry, xs):
        C, n, m = carry
        qc, kc, vc, lic, lfc = xs
        b = jnp.cumsum(lfc, axis=-1)
        D = jnp.where(tri, b[..., :, None] - b[..., None, :] + lic[..., None, :], -jnp.inf)
        inter = b + m[..., None]
        m_j = jnp.maximum(inter, jnp.max(D, axis=-1))
        w_intra = jnp.exp(D - m_j[..., None])
        w_inter = jnp.exp(inter - m_j)
        qk = jnp.einsum('bhjd,bhsd->bhjs', qc, kc) * w_intra
        num = w_inter[..., None] * jnp.einsum('bhjd,bhde->bhje', qc, C) + \
            jnp.einsum('bhjs,bhse->bhje', qk, vc)
        den = w_inter * jnp.einsum('bhjd,bhd->bhj', qc, n) + jnp.sum(qk, axis=-1)
        h = num / jnp.maximum(jnp.abs(den), jnp.exp(-m_j))[..., None]
        bL = b[..., -1]
        logw = bL[..., None] - b + lic
        m_new = jnp.maximum(bL + m, jnp.max(logw, axis=-1))
        decay = jnp.exp(bL + m - m_new)
        ws = jnp.exp(logw - m_new[..., None])
        C_new = decay[..., None, None] * C + jnp.einsum('bhs,bhsd,bhse->bhde', ws, kc, vc)
        n_new = decay[..., None] * n + jnp.einsum('bhs,bhsd->bhd', ws, kc)
        return (C_new, n_new, m_new), h

    init = (jnp.zeros((B, H, dk, dv), jnp.float32),
            jnp.zeros((B, H, dk), jnp.float32),
            jnp.zeros((B, H), jnp.float32))
    xs = (to_chunks4(q), to_chunks4(k), to_chunks4(v), to_chunks3(li), to_chunks3(lf))
    _, hs = lax.scan(body, init, xs)
    return hs.transpose(1, 0, 3, 2, 4).reshape(B, S, H, dv)


def softcap(x):
    return GATE_CAP * jnp.tanh(x / GATE_CAP)


def peer(h, w_pq, sub_keys, peer_u, peer_v):
    B, S, D = h.shape
    T = B * S
    ht = h.reshape(T, D)
    q = (ht @ w_pq).reshape(T, PEER_HEADS, 2, PEER_QDIM // 2)
    s = jnp.einsum('thcd,hcnd->thcn', q, sub_keys).astype(jnp.float32)
    s_top, i_top = lax.top_k(s, PEER_TOPK)
    cand = (s_top[:, :, 0, :, None] + s_top[:, :, 1, None, :]).reshape(T, PEER_HEADS, PEER_TOPK * PEER_TOPK)
    cand_idx = (i_top[:, :, 0, :, None] * PEER_NKEYS + i_top[:, :, 1, None, :]).reshape(T, PEER_HEADS, PEER_TOPK * PEER_TOPK)
    sc, pos = lax.top_k(cand, PEER_TOPK)
    idx = jnp.take_along_axis(cand_idx, pos, axis=-1)
    g = jax.nn.softmax(sc, axis=-1)
    nb = T // PEER_TOK_BLOCK

    def one_block(args):
        h_b, idx_b, g_b = args
        u_sel = jnp.take(peer_u, idx_b, axis=0)
        a = jax.nn.gelu(jnp.einsum('td,thkd->thk', h_b, u_sel).astype(jnp.float32), approximate=False)
        w = (g_b * a).astype(h.dtype)
        v_sel = jnp.take(peer_v, idx_b, axis=0)
        return jnp.einsum('thk,thkd->td', w, v_sel)

    y = lax.map(one_block, (ht.reshape(nb, PEER_TOK_BLOCK, D),
                            idx.reshape(nb, PEER_TOK_BLOCK, PEER_HEADS, PEER_TOPK),
                            g.reshape(nb, PEER_TOK_BLOCK, PEER_HEADS, PEER_TOPK)))
    return y.reshape(B, S, D)


def setup_inputs(seed: int = 0) -> dict:
    key = jax.random.key(seed)
    ks = jax.random.split(key, 26)
    f32 = jnp.float32
    D = D_MODEL
    nrm = lambda k, shp, s: jax.random.normal(k, shp, f32) * s
    return {
        'x': nrm(ks[0], (BATCH, SEQ, D), 1.0),
        'c': nrm(ks[1], (BATCH, D), 1.0),
        'w_ada': nrm(ks[2], (DEPTH, D, 6 * D), 0.5 * D ** -0.5),
        'b_ada': nrm(ks[3], (DEPTH, 6 * D), 0.02),
        'g_mix': 1.0 + nrm(ks[4], (DEPTH, D), 0.02),
        'w_in': nrm(ks[5], (DEPTH, D, PROJ_DIM), D ** -0.5),
        'conv_w': nrm(ks[6], (DEPTH, CONV_W, 2 * ML_HEADS * ML_QK), CONV_W ** -0.5),
        'conv_b': nrm(ks[7], (DEPTH, 2 * ML_HEADS * ML_QK), 0.02),
        'b_igate': -3.0 + nrm(ks[8], (DEPTH, ML_HEADS), 0.1),
        'b_fgate': jnp.broadcast_to(jnp.linspace(3.0, 6.0, ML_HEADS, dtype=f32), (DEPTH, ML_HEADS)) + nrm(ks[9], (DEPTH, ML_HEADS), 0.1),
        'lambda_q1': nrm(ks[10], (DEPTH, DA_QK), 0.1),
        'lambda_k1': nrm(ks[11], (DEPTH, DA_QK), 0.1),
        'lambda_q2': nrm(ks[12], (DEPTH, DA_QK), 0.1),
        'lambda_k2': nrm(ks[13], (DEPTH, DA_QK), 0.1),
        'da_norm': 1.0 + nrm(ks[14], (DEPTH, DA_V), 0.02),
        'ml_norm': 1.0 + nrm(ks[15], (DEPTH, ML_V), 0.02),
        'w_out': nrm(ks[16], (DEPTH, D_MIX, D), D_MIX ** -0.5),
        'g_ffn': 1.0 + nrm(ks[17], (DEPTH, D), 0.02),
        'w_pq': nrm(ks[18], (DEPTH, D, PEER_HEADS * PEER_QDIM), D ** -0.5),
        'sub_keys': nrm(ks[19], (DEPTH, PEER_HEADS, 2, PEER_NKEYS, PEER_QDIM // 2), (PEER_QDIM // 2) ** -0.5),
        'peer_u': nrm(ks[20], (DEPTH, PEER_EXPERTS, D), D ** -0.5),
        'peer_v': nrm(ks[21], (DEPTH, PEER_EXPERTS, D), PEER_HEADS ** -0.5),
        'w_ada_final': nrm(ks[22], (D, 2 * D), 0.5 * D ** -0.5),
        'b_ada_final': nrm(ks[23], (2 * D,), 0.02),
        'g_final': 1.0 + nrm(ks[24], (D,), 0.02),
    }


def reference(x, c, w_ada, b_ada, g_mix, w_in, conv_w, conv_b, b_igate, b_fgate,
              lambda_q1, lambda_k1, lambda_q2, lambda_k2, da_norm, ml_norm, w_out,
              g_ffn, w_pq, sub_keys, peer_u, peer_v, w_ada_final, b_ada_final, g_final):
    B, S, D = x.shape
    pos = jnp.arange(S)
    cs = jax.nn.silu(c.astype(jnp.float32)).astype(x.dtype)
    for l in range(DEPTH):
        mod = (cs @ w_ada[l] + b_ada[l]).reshape(B, 6, D)
        sh_m, sc_m, gt_m, sh_f, sc_f, gt_f = [mod[:, i] for i in range(6)]

        h = modulate(rms_norm(x, g_mix[l]), sh_m, sc_m)
        p = jnp.einsum('bsd,dp->bsp', h, w_in[l])
        da_q, da_k, da_v, ml_q, ml_k, ml_v, ml_o, ml_i, ml_f = jnp.split(p, PROJ_SPLITS, axis=-1)

        lam_init = 0.8 - 0.6 * math.exp(-0.3 * l)
        lam = (jnp.exp(jnp.sum(lambda_q1[l].astype(jnp.float32) * lambda_k1[l].astype(jnp.float32)))
               - jnp.exp(jnp.sum(lambda_q2[l].astype(jnp.float32) * lambda_k2[l].astype(jnp.float32)))
               + lam_init)
        qa = rope(da_q.reshape(B, S, 2 * DA_HEADS, DA_QK), pos)
        ka = rope(da_k.reshape(B, S, 2 * DA_HEADS, DA_QK), pos)
        va = da_v.reshape(B, S, DA_HEADS, DA_V)
        oa = differential_attention(qa, ka, va, lam)
        oa = (rms_norm(oa, da_norm[l]) * (1.0 - lam_init)).reshape(B, S, DA_WIDTH)

        qk_m = jax.nn.silu(causal_dwconv(jnp.concatenate([ml_q, ml_k], axis=-1), conv_w[l], conv_b[l]))
        qm, km = jnp.split(qk_m, 2, axis=-1)
        qm = qm.reshape(B, S, ML_HEADS, ML_QK).astype(jnp.float32)
        km = km.reshape(B, S, ML_HEADS, ML_QK).astype(jnp.float32) * (ML_QK ** -0.5)
        vm = ml_v.reshape(B, S, ML_HEADS, ML_V).astype(jnp.float32)
        li = softcap((ml_i + b_igate[l]).astype(jnp.float32))
        lf = jax.nn.log_sigmoid(softcap((ml_f + b_fgate[l]).astype(jnp.float32)))
        hm = mlstm_chunkwise(qm, km, vm, li, lf).astype(x.dtype)
        om = rms_norm(hm, ml_norm[l]).reshape(B, S, ML_WIDTH) * jax.nn.sigmoid(ml_o)

        mixed = jnp.einsum('bsm,md->bsd', jnp.concatenate([oa, om], axis=-1), w_out[l])
        x = x + gt_m[:, None, :] * mixed

        h2 = modulate(rms_norm(x, g_ffn[l]), sh_f, sc_f)
        x = x + gt_f[:, None, :] * peer(h2, w_pq[l], sub_keys[l], peer_u[l], peer_v[l])

    mod_o = (cs @ w_ada_final + b_ada_final).reshape(B, 2, D)
    return modulate(rms_norm(x, g_final), mod_o[:, 0], mod_o[:, 1])
```

```python
import functools
import math

import numpy as np
import jax
import jax.numpy as jnp
from jax import lax
from jax.experimental import pallas as pl
from jax.experimental.pallas import tpu as pltpu

F32 = jnp.float32
BF16 = jnp.bfloat16

DA_QK = 64
DA_V = 128
ML_QK = 128
ML_V = 256
CONV_W = 4
GATE_CAP = 15.0
ROPE_THETA = 10000.0
PEER_HEADS = 8
PEER_NKEYS = 128
PEER_TOPK = 16
EPS = 1e-6

LANES = 128
SUBLANES = 8
VMEM_LIMIT = 56 * 1024 * 1024

ADA_TN = 1024
PROJ_TM = 512
PROJ_TN = 512
ATT_TQ = 256
ML_CHUNK = 64
OUT_TM = 256
SEL_TM = 256
FFN_TB = 256
FFN_EC = 512

NEG_INF = float("-inf")


def _cparams(sem):
    return pltpu.CompilerParams(dimension_semantics=sem, vmem_limit_bytes=VMEM_LIMIT)


def _rms(x, g):
    return x * lax.rsqrt(jnp.mean(x * x, axis=-1, keepdims=True) + EPS) * g


def _ada_kernel(c_ref, w_ref, b_ref, o_ref):
    c = c_ref[...]
    cs = c * jax.nn.sigmoid(c)
    o_ref[...] = jnp.dot(cs, w_ref[...], preferred_element_type=F32,
                         precision=lax.Precision.HIGHEST) + b_ref[...]


def _ada(c8, w, b):
    d, n = w.shape
    return pl.pallas_call(
        _ada_kernel,
        grid=(n // ADA_TN,),
        in_specs=[pl.BlockSpec((SUBLANES, d), lambda j: (0, 0)),
                  pl.BlockSpec((d, ADA_TN), lambda j: (0, j)),
                  pl.BlockSpec((1, ADA_TN), lambda j: (0, j))],
        out_specs=pl.BlockSpec((SUBLANES, ADA_TN), lambda j: (0, j)),
        out_shape=jax.ShapeDtypeStruct((SUBLANES, n), F32),
        compiler_params=_cparams(("arbitrary",)),
        name="ada",
    )(c8, w, b.reshape(1, n))


def _proj_kernel(n_rope, n_att, x_ref, mod_ref, g_ref, w_ref, wg_ref, cos_ref, sa_ref, sb_ref,
                 pa_ref, pm_ref, gate_ref, h_scr):
    j = pl.program_id(1)

    @pl.when(j == 0)
    def _():
        y = _rms(x_ref[...], g_ref[...])
        h = y * (1.0 + mod_ref[0, 1:2, :]) + mod_ref[0, 0:1, :]
        hb = h.astype(BF16)
        h_scr[...] = hb
        gate_ref[...] = jnp.dot(hb, wg_ref[...], preferred_element_type=F32)

    acc = jnp.dot(h_scr[...], w_ref[...], preferred_element_type=F32)

    @pl.when(j < n_rope)
    def _():
        cos, sa, sb = cos_ref[...], sa_ref[...], sb_ref[...]
        for g in range(PROJ_TN // LANES):
            xg = acc[:, g * LANES:(g + 1) * LANES]
            r = (xg * cos + pltpu.roll(xg, LANES - DA_QK // 2, 1) * sa
                 + pltpu.roll(xg, DA_QK // 2, 1) * sb)
            pa_ref[:, g * LANES:(g + 1) * LANES] = r.astype(BF16)

    @pl.when(jnp.logical_and(j >= n_rope, j < n_att))
    def _():
        pa_ref[...] = acc.astype(BF16)

    @pl.when(j >= n_att)
    def _():
        pm_ref[...] = acc


def _rope_tables(seq):
    half = DA_QK // 2
    inv = ROPE_THETA ** (-jnp.arange(half, dtype=F32) / half)
    ang = jnp.arange(seq, dtype=F32)[:, None] * inv[None, :]
    lane = np.arange(LANES)
    first = (lane % DA_QK) < half
    cos = jnp.cos(ang)[:, lane % half]
    sin = jnp.sin(ang)[:, lane % half]
    sa = jnp.where(first[None, :], -sin, 0.0)
    sb = jnp.where(first[None, :], 0.0, sin)
    return cos, sa, sb


def _proj(x2, mod3, g, w_main, w_gate, seq, n_att_cols, n_rope_cols):
    t, d = x2.shape
    n = w_main.shape[1]
    n_att = n_att_cols // PROJ_TN
    n_rope = n_rope_cols // PROJ_TN
    n_tiles = n // PROJ_TN
    tiles_per_seq = seq // PROJ_TM
    cos, sa, sb = _rope_tables(seq)
    tab_spec = pl.BlockSpec((PROJ_TM, LANES), lambda i, j: (i % tiles_per_seq, 0))
    return pl.pallas_call(
        functools.partial(_proj_kernel, n_rope, n_att),
        grid=(t // PROJ_TM, n_tiles),
        in_specs=[pl.BlockSpec((PROJ_TM, d), lambda i, j: (i, 0)),
                  pl.BlockSpec((1, 6, d), lambda i, j: (i // tiles_per_seq, 0, 0)),
                  pl.BlockSpec((1, d), lambda i, j: (0, 0)),
                  pl.BlockSpec((d, PROJ_TN), lambda i, j: (0, j)),
                  pl.BlockSpec((d, LANES), lambda i, j: (0, 0)),
                  tab_spec, tab_spec, tab_spec],
        out_specs=[pl.BlockSpec((PROJ_TM, PROJ_TN), lambda i, j: (i, jnp.minimum(j, n_att - 1))),
                   pl.BlockSpec((PROJ_TM, PROJ_TN), lambda i, j: (i, jnp.maximum(j - n_att, 0))),
                   pl.BlockSpec((PROJ_TM, LANES), lambda i, j: (i, 0))],
        out_shape=[jax.ShapeDtypeStruct((t, n_att_cols), BF16),
                   jax.ShapeDtypeStruct((t, n - n_att_cols), F32),
                   jax.ShapeDtypeStruct((t, LANES), F32)],
        scratch_shapes=[pltpu.VMEM((PROJ_TM, d), BF16)],
        compiler_params=_cparams(("arbitrary", "arbitrary")),
        name="proj",
    )(x2, mod3, g.reshape(1, d), w_main, w_gate, cos, sa, sb)


def _attn_kernel(lam_init, lam_ref, q_ref, k_ref, v_ref, nrm_ref, o_ref, m_scr, l_scr, acc_scr):
    i = pl.program_id(2)
    tq = ATT_TQ
    lv = lam_ref[...]
    lam = (jnp.exp(jnp.sum(lv[0:1] * lv[1:2], axis=-1, keepdims=True))
           - jnp.exp(jnp.sum(lv[2:3] * lv[3:4], axis=-1, keepdims=True)) + lam_init)

    q = q_ref[...] * (DA_QK ** -0.5)
    lane = lax.broadcasted_iota(jnp.int32, (tq, LANES), 1)
    zero = jnp.zeros_like(q)
    q2 = jnp.concatenate([jnp.where(lane < DA_QK, q, zero), jnp.where(lane >= DA_QK, q, zero)], axis=0)

    m_scr[...] = jnp.full(m_scr.shape, NEG_INF, F32)
    l_scr[...] = jnp.zeros(l_scr.shape, F32)
    acc_scr[...] = jnp.zeros(acc_scr.shape, F32)

    row = lax.broadcasted_iota(jnp.int32, (2 * tq, tq), 0)
    qpos = jnp.where(row >= tq, row - tq, row)
    kcol = lax.broadcasted_iota(jnp.int32, (2 * tq, tq), 1)

    def body(jb, carry):
        start = pl.multiple_of(jb * tq, tq)
        k = k_ref[pl.ds(start, tq), :]
        v = v_ref[pl.ds(start, tq), :]
        s = lax.dot_general(q2, k, (((1,), (1,)), ((), ())), preferred_element_type=F32)
        s = jnp.where(kcol + (jb - i) * tq <= qpos, s, NEG_INF)
        m_old = m_scr[...]
        m_new = jnp.maximum(m_old, jnp.max(s, axis=-1, keepdims=True))
        alpha = jnp.exp(m_old - m_new)
        p = jnp.exp(s - m_new)
        l_scr[...] = alpha * l_scr[...] + jnp.sum(p, axis=-1, keepdims=True)
        acc_scr[...] = alpha * acc_scr[...] + jnp.dot(p.astype(BF16), v, preferred_element_type=F32)
        m_scr[...] = m_new
        return carry

    lax.fori_loop(0, i + 1, body, 0)

    o2 = acc_scr[...] / l_scr[...]
    o = o2[:tq] - lam * o2[tq:]
    o_ref[...] = (_rms(o, nrm_ref[...]) * (1.0 - lam_init)).astype(o_ref.dtype)


def _attn(pa, lam4, da_norm, batch, seq, heads, lam_init):
    t = pa.shape[0]
    nq = seq // ATT_TQ
    return pl.pallas_call(
        functools.partial(_attn_kernel, lam_init),
        grid=(batch, heads, nq),
        in_specs=[pl.BlockSpec((4, DA_QK), lambda b, h, i: (0, 0)),
                  pl.BlockSpec((ATT_TQ, LANES), lambda b, h, i: (b * nq + i, h)),
                  pl.BlockSpec((seq, LANES), lambda b, h, i: (b, heads + h)),
                  pl.BlockSpec((seq, LANES), lambda b, h, i: (b, 2 * heads + h)),
                  pl.BlockSpec((1, DA_V), lambda b, h, i: (0, 0))],
        out_specs=pl.BlockSpec((ATT_TQ, DA_V), lambda b, h, i: (b * nq + i, h)),
        out_shape=jax.ShapeDtypeStruct((t, heads * DA_V), BF16),
        scratch_shapes=[pltpu.VMEM((2 * ATT_TQ, 1), F32), pltpu.VMEM((2 * ATT_TQ, 1), F32),
                        pltpu.VMEM((2 * ATT_TQ, DA_V), F32)],
        compiler_params=_cparams(("arbitrary", "arbitrary", "arbitrary")),
        name="attn",
    )(lam4, pa, pa, pa, da_norm.reshape(1, DA_V))


def _softcap(x):
    return GATE_CAP * jnp.tanh(x / GATE_CAP)


def _dwconv_silu(x, w, b):
    seq = x.shape[0]
    row = lax.broadcasted_iota(jnp.int32, x.shape, 0)
    y = x * w[CONV_W - 1:CONV_W]
    for s in range(1, CONV_W):
        xs = jnp.where(row >= s, pltpu.roll(x, s, 0), 0.0)
        y = y + xs * w[CONV_W - 1 - s:CONV_W - s]
    y = y + b
    return y * jax.nn.sigmoid(y)


def _mlstm_kernel(bi_ref, bf_ref, q_ref, k_ref, v_ref, og_ref, cwq_ref, cwk_ref, cbq_ref, cbk_ref,
                  gc_ref, gr_ref, nrm_ref, o_ref,
                  q_scr, k_scr, lic_scr, lfc_scr, lir_scr, lfr_scr, c_scr, m_scr):
    hd = pl.program_id(1)
    L = ML_CHUNK
    seq = q_ref.shape[0]
    nc = seq // L
    bi = bi_ref[hd]
    bf = bf_ref[hd]

    q_scr[...] = _dwconv_silu(q_ref[...], cwq_ref[...], cbq_ref[...]).astype(BF16)
    k_scr[...] = _dwconv_silu(k_ref[...], cwk_ref[...], cbk_ref[...]) * (ML_QK ** -0.5)

    gc = gc_ref[0, 0]
    lic_scr[...] = _softcap(gc[:, 0:1] + bi)
    lfc_scr[...] = jax.nn.log_sigmoid(_softcap(gc[:, 1:2] + bf))
    lir_scr[...] = _softcap(gr_ref[0, 0, 0] + bi)
    lfr_scr[...] = jax.nn.log_sigmoid(_softcap(gr_ref[0, 0, 1] + bf))

    c_scr[...] = jnp.zeros(c_scr.shape, F32)
    m_scr[...] = jnp.zeros(m_scr.shape, F32)

    r_i = lax.broadcasted_iota(jnp.int32, (L, L), 0)
    c_i = lax.broadcasted_iota(jnp.int32, (L, L), 1)
    tril = (c_i <= r_i)
    tril_f = tril.astype(F32)
    triu_f = (r_i <= c_i).astype(F32)
    ones_col = (lax.broadcasted_iota(jnp.int32, (L, LANES), 1) == 0).astype(BF16)
    nrm = nrm_ref[...]
    hp = lax.Precision.HIGHEST

    def body(c, carry):
        start = pl.multiple_of(c * L, L)
        sl = pl.ds(start, L)
        qc = q_scr[sl, :]
        kc = k_scr[sl, :]
        vc = v_ref[sl, :].astype(BF16)
        lfc = lfc_scr[sl, :]
        lic = lic_scr[sl, :]
        lfr = lfr_scr[pl.ds(c, 1), :]
        lir = lir_scr[pl.ds(c, 1), :]
        m = m_scr[...]

        b_c = jnp.dot(tril_f, jnp.broadcast_to(lfc, (L, L)), preferred_element_type=F32, precision=hp)
        b_r = jnp.dot(jnp.broadcast_to(lfr, (L, L)), triu_f, preferred_element_type=F32, precision=hp)
        dmat = jnp.where(tril, b_c - b_r + lir, NEG_INF)
        bcol = b_c[:, 0:1]
        inter = bcol + m
        m_j = jnp.maximum(inter, jnp.max(dmat, axis=-1, keepdims=True))
        w_intra = jnp.exp(dmat - m_j)
        w_inter = jnp.exp(inter - m_j)

        sqk = lax.dot_general(qc, kc.astype(BF16), (((1,), (1,)), ((), ())), preferred_element_type=F32)
        qk = sqk * w_intra
        q_c = jnp.dot(qc, c_scr[...].astype(BF16), preferred_element_type=F32)
        num = w_inter * q_c[:, :ML_V] + jnp.dot(qk.astype(BF16), vc, preferred_element_type=F32)
        den = w_inter * q_c[:, ML_V:ML_V + 1] + jnp.sum(qk, axis=-1, keepdims=True)
        hh = num / jnp.maximum(jnp.abs(den), jnp.exp(-m_j))
        og = og_ref[sl, :]
        o_ref[sl, :] = (_rms(hh, nrm) * jax.nn.sigmoid(og)).astype(o_ref.dtype)

        b_l = b_c[L - 1:L, 0:1]
        logw = b_l - bcol + lic
        m_new = jnp.maximum(b_l + m, jnp.max(logw, axis=0, keepdims=True))
        decay = jnp.exp(b_l + m - m_new)
        ws = jnp.exp(logw - m_new)
        kw = (kc * ws).astype(BF16)
        vext = jnp.concatenate([vc, ones_col], axis=1)
        upd = lax.dot_general(kw, vext, (((0,), (0,)), ((), ())), preferred_element_type=F32)
        c_scr[...] = decay * c_scr[...] + upd
        m_scr[...] = m_new
        return carry

    lax.fori_loop(0, nc, body, 0)


def _mlstm(pm, gates_col, gates_row, conv_w, conv_b, b_ig, b_fg, ml_norm, batch, seq, heads):
    t = pm.shape[0]
    nc = seq // ML_CHUNK
    qk_w = heads * ML_QK
    vblk = 2 * qk_w // ML_V
    smem = pl.BlockSpec(memory_space=pltpu.SMEM)
    return pl.pallas_call(
        _mlstm_kernel,
        grid=(batch, heads),
        in_specs=[smem, smem,
                  pl.BlockSpec((seq, ML_QK), lambda b, h: (b, h)),
                  pl.BlockSpec((seq, ML_QK), lambda b, h: (b, heads + h)),
                  pl.BlockSpec((seq, ML_V), lambda b, h: (b, vblk + h)),
                  pl.BlockSpec((seq, ML_V), lambda b, h: (b, vblk + heads + h)),
                  pl.BlockSpec((CONV_W, ML_QK), lambda b, h: (0, h)),
                  pl.BlockSpec((CONV_W, ML_QK), lambda b, h: (0, heads + h)),
                  pl.BlockSpec((1, ML_QK), lambda b, h: (0, h)),
                  pl.BlockSpec((1, ML_QK), lambda b, h: (0, heads + h)),
                  pl.BlockSpec((1, 1, seq, 2), lambda b, h: (b, h, 0, 0)),
                  pl.BlockSpec((1, 1, 2, nc, ML_CHUNK), lambda b, h: (b, h, 0, 0, 0)),
                  pl.BlockSpec((1, ML_V), lambda b, h: (0, 0))],
        out_specs=pl.BlockSpec((seq, ML_V), lambda b, h: (b, h)),
        out_shape=jax.ShapeDtypeStruct((t, heads * ML_V), BF16),
        scratch_shapes=[pltpu.VMEM((seq, ML_QK), BF16), pltpu.VMEM((seq, ML_QK), F32),
                        pltpu.VMEM((seq, 1), F32), pltpu.VMEM((seq, 1), F32),
                        pltpu.VMEM((nc, ML_CHUNK), F32), pltpu.VMEM((nc, ML_CHUNK), F32),
                        pltpu.VMEM((ML_QK, ML_V + LANES), F32), pltpu.VMEM((1, 1), F32)],
        compiler_params=_cparams(("arbitrary", "arbitrary")),
        name="mlstm",
    )(b_ig, b_fg, pm, pm, pm, pm, conv_w, conv_w, conv_b.reshape(1, -1), conv_b.reshape(1, -1),
      gates_col, gates_row, ml_norm.reshape(1, ML_V))


def _outproj_kernel(oa_ref, om_ref, w_ref, x_ref, mod_ref, g_ref, x1_ref, h2_ref):
    ka = oa_ref.shape[1]
    mixed = (jnp.dot(oa_ref[...], w_ref[:ka, :], preferred_element_type=F32)
             + jnp.dot(om_ref[...], w_ref[ka:, :], preferred_element_type=F32))
    x1 = x_ref[...] + mod_ref[0, 2:3, :] * mixed
    x1_ref[...] = x1
    h2 = _rms(x1, g_ref[...]) * (1.0 + mod_ref[0, 4:5, :]) + mod_ref[0, 3:4, :]
    h2_ref[...] = h2.astype(BF16)


def _outproj(oa, om, w_out, x2, mod3, g_ffn, seq):
    t, d = x2.shape
    tiles_per_seq = seq // OUT_TM
    return pl.pallas_call(
        _outproj_kernel,
        grid=(t // OUT_TM,),
        in_specs=[pl.BlockSpec((OUT_TM, oa.shape[1]), lambda i: (i, 0)),
                  pl.BlockSpec((OUT_TM, om.shape[1]), lambda i: (i, 0)),
                  pl.BlockSpec(w_out.shape, lambda i: (0, 0)),
                  pl.BlockSpec((OUT_TM, d), lambda i: (i, 0)),
                  pl.BlockSpec((1, 6, d), lambda i: (i // tiles_per_seq, 0, 0)),
                  pl.BlockSpec((1, d), lambda i: (0, 0))],
        out_specs=[pl.BlockSpec((OUT_TM, d), lambda i: (i, 0)),
                   pl.BlockSpec((OUT_TM, d), lambda i: (i, 0))],
        out_shape=[jax.ShapeDtypeStruct((t, d), F32), jax.ShapeDtypeStruct((t, d), BF16)],
        compiler_params=_cparams(("arbitrary",)),
        name="outproj",
    )(oa, om, w_out, x2, mod3, g_ffn.reshape(1, d))


_CAND_ROWS = 16 + 8 + 6 * 8 + 8


def _rank16(s, top_scr):
    n, tm = s.shape
    iota = lax.broadcasted_iota(jnp.int32, (n, tm), 0).astype(F32)
    rank = jnp.full((n, tm), float(PEER_TOPK), F32)
    for r in range(PEER_TOPK):
        mx = jnp.max(s, axis=0, keepdims=True)
        first = jnp.min(jnp.where(s == mx, iota, float(n)), axis=0, keepdims=True)
        sel = iota == first
        rank = jnp.where(sel, float(r), rank)
        s = jnp.where(sel, NEG_INF, s)
        top_scr[r:r + 1, :] = mx
    return rank


def _peer_sel_kernel(h2_ref, w_ref, keys_ref, r2_ref, e2_ref, n1_ref, e1_ref, t1_scr, t2_scr):
    k = PEER_TOPK
    half = PEER_NKEYS
    q = jnp.dot(h2_ref[...], w_ref[...], preferred_element_type=F32).astype(BF16)
    nt = (((1,), (1,)), ((), ()))
    s1 = lax.dot_general(keys_ref[0, 0], q[:, :half], nt, preferred_element_type=F32)
    s2 = lax.dot_general(keys_ref[0, 1], q[:, half:], nt, preferred_element_type=F32)
    r1 = _rank16(s1, t1_scr)
    r2 = _rank16(s2, t2_scr)
    t1 = t1_scr[...]
    t2 = t2_scr[...]
    tm = s1.shape[1]

    brow = lax.broadcasted_iota(jnp.int32, (SUBLANES, tm), 0)
    pieces = [t1[0:1] + t2, t1[1:2] + t2[0:SUBLANES]]
    for a in range(2, SUBLANES):
        pieces.append(jnp.where(brow < k // (a + 1), t1[a:a + 1] + t2[0:SUBLANES], NEG_INF))
    pieces.append(t1[SUBLANES:] + t2[0:1])
    cand = jnp.concatenate(pieces, axis=0)
    cmax = cand[0:1]

    work = cand
    iota = lax.broadcasted_iota(jnp.int32, cand.shape, 0).astype(F32)
    taken = jnp.zeros(cand.shape, F32)
    for _ in range(k):
        mx = jnp.max(work, axis=0, keepdims=True)
        first = jnp.min(jnp.where(work == mx, iota, float(_CAND_ROWS)), axis=0, keepdims=True)
        sel = iota == first
        taken = jnp.where(sel, 1.0, taken)
        work = jnp.where(sel, NEG_INF, work)
    z = jnp.sum(jnp.where(taken > 0.0, jnp.exp(cand - cmax), 0.0), axis=0, keepdims=True)

    n_rows = [jnp.sum(taken[0:16], axis=0, keepdims=True)]
    for a in range(1, SUBLANES):
        lo = 16 + (a - 1) * SUBLANES
        n_rows.append(jnp.sum(taken[lo:lo + SUBLANES], axis=0, keepdims=True))
    base = 16 + 7 * SUBLANES
    for a in range(SUBLANES, k):
        n_rows.append(taken[base + a - SUBLANES:base + a - SUBLANES + 1])
    n1 = jnp.zeros(r1.shape, F32)
    for a in range(k):
        n1 = jnp.where(r1 == float(a), n_rows[a], n1)

    r2_ref[0] = r2
    e2_ref[0] = jnp.exp(s2 - t2[0:1])
    n1_ref[0] = n1
    e1_ref[0] = jnp.exp(s1 - t1[0:1]) / z


def _peer_sel(h2, w_pq, keys):
    t, d = h2.shape
    nh = PEER_HEADS
    qd = 2 * PEER_NKEYS
    tab = jax.ShapeDtypeStruct((nh, PEER_NKEYS, t), F32)
    tab_spec = pl.BlockSpec((1, PEER_NKEYS, SEL_TM), lambda i, h: (h, 0, i))
    return pl.pallas_call(
        _peer_sel_kernel,
        grid=(t // SEL_TM, nh),
        in_specs=[pl.BlockSpec((SEL_TM, d), lambda i, h: (i, 0)),
                  pl.BlockSpec((d, qd), lambda i, h: (0, h)),
                  pl.BlockSpec((1, 2, PEER_NKEYS, PEER_NKEYS), lambda i, h: (h, 0, 0, 0))],
        out_specs=[tab_spec, tab_spec, tab_spec, tab_spec],
        out_shape=[tab, tab, tab, tab],
        scratch_shapes=[pltpu.VMEM((PEER_TOPK, SEL_TM), F32), pltpu.VMEM((PEER_TOPK, SEL_TM), F32)],
        compiler_params=_cparams(("arbitrary", "arbitrary")),
        name="peer_sel",
    )(h2, w_pq, keys)


def _peer_ffn_kernel(h2_ref, u_ref, vt_ref, r2_ref, e2_ref, n1_ref, e1_ref, x1_ref, mod_ref, modo_ref,
                     g_ref, o_ref, acc_scr, w_scr):
    c = pl.program_id(1)
    nk = PEER_NKEYS
    per = FFN_EC // nk

    @pl.when(c == 0)
    def _():
        acc_scr[...] = jnp.zeros(acc_scr.shape, F32)

    h2 = h2_ref[...]
    for ii in range(per):
        i = c * per + ii
        a = lax.dot_general(u_ref[ii * nk:(ii + 1) * nk, :], h2, (((1,), (1,)), ((), ())),
                            preferred_element_type=F32)
        gsum = jnp.zeros(a.shape, F32)
        for hd in range(PEER_HEADS):
            n1 = n1_ref[hd, pl.ds(i, 1), :]
            e1 = e1_ref[hd, pl.ds(i, 1), :]
            gsum = gsum + jnp.where(r2_ref[hd] < n1, e2_ref[hd], 0.0) * e1
        act = 0.5 * a * (1.0 + lax.erf(a * (2.0 ** -0.5)))
        w_scr[ii * nk:(ii + 1) * nk, :] = (gsum * act).astype(BF16)

    acc_scr[...] += jnp.dot(vt_ref[...], w_scr[...], preferred_element_type=F32)

    @pl.when(c == pl.num_programs(1) - 1)
    def _():
        y = acc_scr[...].T
        x2 = x1_ref[...] + mod_ref[0, 5:6, :] * y
        o_ref[...] = _rms(x2, g_ref[...]) * (1.0 + modo_ref[0, 1:2, :]) + modo_ref[0, 0:1, :]


def _peer_ffn(h2, u, vt, tabs, x1, mod3, modo3, g_final, seq):
    t, d = h2.shape
    ne = u.shape[0]
    tiles_per_seq = seq // FFN_TB
    tab_spec = pl.BlockSpec((PEER_HEADS, PEER_NKEYS, FFN_TB), lambda i, c: (0, 0, i))
    return pl.pallas_call(
        _peer_ffn_kernel,
        grid=(t // FFN_TB, ne // FFN_EC),
        in_specs=[pl.BlockSpec((FFN_TB, d), lambda i, c: (i, 0)),
                  pl.BlockSpec((FFN_EC, d), lambda i, c: (c, 0)),
                  pl.BlockSpec((d, FFN_EC), lambda i, c: (0, c)),
                  tab_spec, tab_spec, tab_spec, tab_spec,
                  pl.BlockSpec((FFN_TB, d), lambda i, c: (i, 0)),
                  pl.BlockSpec((1, 6, d), lambda i, c: (i // tiles_per_seq, 0, 0)),
                  pl.BlockSpec((1, 2, d), lambda i, c: (i // tiles_per_seq, 0, 0)),
                  pl.BlockSpec((1, d), lambda i, c: (0, 0))],
        out_specs=pl.BlockSpec((FFN_TB, d), lambda i, c: (i, 0)),
        out_shape=jax.ShapeDtypeStruct((t, d), F32),
        scratch_shapes=[pltpu.VMEM((d, FFN_TB), F32), pltpu.VMEM((FFN_EC, FFN_TB), BF16)],
        compiler_params=_cparams(("arbitrary", "arbitrary")),
        name="peer_ffn",
    )(h2, u, vt, *tabs, x1, mod3, modo3, g_final.reshape(1, d))


def kernel(x, c, w_ada, b_ada, g_mix, w_in, conv_w, conv_b, b_igate, b_fgate, lambda_q1, lambda_k1,
           lambda_q2, lambda_k2, da_norm, ml_norm, w_out, g_ffn, w_pq, sub_keys, peer_u, peer_v,
           w_ada_final, b_ada_final, g_final):
    batch, seq, d = x.shape
    depth = w_ada.shape[0]
    t = batch * seq
    da_heads = d // 256
    ml_heads = d // 512
    att_cols = 3 * da_heads * DA_V
    rope_cols = 2 * da_heads * DA_V
    main_cols = att_cols + 2 * ml_heads * ML_QK + 2 * ml_heads * ML_V
    assert batch <= SUBLANES and seq % PROJ_TM == 0 and seq % ATT_TQ == 0 and seq % ML_CHUNK == 0
    assert w_in.shape[2] == main_cols + 2 * ml_heads

    c8 = jnp.zeros((SUBLANES, d), F32).at[:batch].set(c.astype(F32))
    modo3 = _ada(c8, w_ada_final, b_ada_final)[:batch].reshape(batch, 2, d)
    xt = x.reshape(t, d)

    for l in range(depth):
        mod3 = _ada(c8, w_ada[l], b_ada[l])[:batch].reshape(batch, 6, d)
        lam_init = 0.8 - 0.6 * math.exp(-0.3 * l)

        w_main = w_in[l, :, :main_cols].astype(BF16)
        w_gate = jnp.zeros((d, LANES), BF16).at[:, :2 * ml_heads].set(w_in[l, :, main_cols:].astype(BF16))
        pa, pm, gates = _proj(xt, mod3, g_mix[l], w_main, w_gate, seq, att_cols, rope_cols)

        lam4 = jnp.stack([lambda_q1[l], lambda_k1[l], lambda_q2[l], lambda_k2[l]]).astype(F32)
        oa = _attn(pa, lam4, da_norm[l], batch, seq, da_heads, lam_init)

        nc = seq // ML_CHUNK
        g8 = gates[:, :2 * ml_heads].reshape(batch, seq, 2, ml_heads)
        gates_col = g8.transpose(0, 3, 1, 2)
        gates_row = g8.transpose(0, 3, 2, 1).reshape(batch, ml_heads, 2, nc, ML_CHUNK)
        om = _mlstm(pm, gates_col, gates_row, conv_w[l], conv_b[l], b_igate[l], b_fgate[l], ml_norm[l],
                    batch, seq, ml_heads)

        x1, h2 = _outproj(oa, om, w_out[l].astype(BF16), xt, mod3, g_ffn[l], seq)

        tabs = _peer_sel(h2, w_pq[l].astype(BF16), sub_keys[l].astype(BF16))
        assert depth == 1
        xt = _peer_ffn(h2, peer_u[l].astype(BF16), peer_v[l].T.astype(BF16), tabs, x1, mod3, modo3,
                       g_final, seq)

    return xt.reshape(batch, seq, d)
```

```python
import functools
import math

import numpy as np
import jax
import jax.numpy as jnp
from jax import lax
from jax.experimental import pallas as pl
from jax.experimental.pallas import tpu as pltpu

F32 = jnp.float32
BF16 = jnp.bfloat16

DA_QK = 64
DA_V = 128
ML_QK = 128
ML_V = 256
CONV_W = 4
GATE_CAP = 15.0
ROPE_THETA = 10000.0
PEER_HEADS = 8
PEER_NKEYS = 128
PEER_TOPK = 16
EPS = 1e-6

LANES = 128
SUBLANES = 8
VMEM_LIMIT = 56 * 1024 * 1024
FFN_VMEM_LIMIT = 60 * 1024 * 1024

ADA_TN = 1024
PROJ_TM = 512
PROJ_TN = 512
ATT_TQ = 256
ML_CHUNK = 64
OUT_TM = 256
SEL_TM = 256
FFN_TB = 512
FFN_EC = 512

NEG_INF = float("-inf")


def _cparams(sem):
    return pltpu.CompilerParams(dimension_semantics=sem, vmem_limit_bytes=VMEM_LIMIT)


def _rms(x, g):
    return x * lax.rsqrt(jnp.mean(x * x, axis=-1, keepdims=True) + EPS) * g


def _ada_kernel(c_ref, w_ref, b_ref, o_ref):
    c = c_ref[...]
    cs = c * jax.nn.sigmoid(c)
    o_ref[...] = jnp.dot(cs, w_ref[...], preferred_element_type=F32,
                         precision=lax.Precision.HIGHEST) + b_ref[...]


def _ada(c8, w, b):
    d, n = w.shape
    return pl.pallas_call(
        _ada_kernel,
        grid=(n // ADA_TN,),
        in_specs=[pl.BlockSpec((SUBLANES, d), lambda j: (0, 0)),
                  pl.BlockSpec((d, ADA_TN), lambda j: (0, j)),
                  pl.BlockSpec((1, ADA_TN), lambda j: (0, j))],
        out_specs=pl.BlockSpec((SUBLANES, ADA_TN), lambda j: (0, j)),
        out_shape=jax.ShapeDtypeStruct((SUBLANES, n), F32),
        compiler_params=_cparams(("arbitrary",)),
        name="ada",
    )(c8, w, b.reshape(1, n))


def _proj_kernel(n_rope, n_att, x_ref, mod_ref, g_ref, w_ref, wg_ref, cos_ref, sa_ref, sb_ref,
                 pa_ref, pm_ref, gate_ref, h_scr):
    j = pl.program_id(1)

    @pl.when(j == 0)
    def _():
        y = _rms(x_ref[...], g_ref[...])
        h = y * (1.0 + mod_ref[0, 1:2, :]) + mod_ref[0, 0:1, :]
        hb = h.astype(BF16)
        h_scr[...] = hb
        gate_ref[...] = jnp.dot(hb, wg_ref[...], preferred_element_type=F32)

    acc = jnp.dot(h_scr[...], w_ref[...], preferred_element_type=F32)

    @pl.when(j < n_rope)
    def _():
        cos, sa, sb = cos_ref[...], sa_ref[...], sb_ref[...]
        for g in range(PROJ_TN // LANES):
            xg = acc[:, g * LANES:(g + 1) * LANES]
            r = (xg * cos + pltpu.roll(xg, LANES - DA_QK // 2, 1) * sa
                 + pltpu.roll(xg, DA_QK // 2, 1) * sb)
            pa_ref[:, g * LANES:(g + 1) * LANES] = r.astype(BF16)

    @pl.when(jnp.logical_and(j >= n_rope, j < n_att))
    def _():
        pa_ref[...] = acc.astype(BF16)

    @pl.when(j >= n_att)
    def _():
        pm_ref[...] = acc


def _rope_tables(seq):
    half = DA_QK // 2
    inv = ROPE_THETA ** (-jnp.arange(half, dtype=F32) / half)
    ang = jnp.arange(seq, dtype=F32)[:, None] * inv[None, :]
    lane = np.arange(LANES)
    first = (lane % DA_QK) < half
    cos = jnp.cos(ang)[:, lane % half]
    sin = jnp.sin(ang)[:, lane % half]
    sa = jnp.where(first[None, :], -sin, 0.0)
    sb = jnp.where(first[None, :], 0.0, sin)
    return cos, sa, sb


def _proj(x2, mod3, g, w_main, w_gate, seq, n_att_cols, n_rope_cols):
    t, d = x2.shape
    n = w_main.shape[1]
    n_att = n_att_cols // PROJ_TN
    n_rope = n_rope_cols // PROJ_TN
    n_tiles = n // PROJ_TN
    tiles_per_seq = seq // PROJ_TM
    cos, sa, sb = _rope_tables(seq)
    tab_spec = pl.BlockSpec((PROJ_TM, LANES), lambda i, j: (i % tiles_per_seq, 0))
    return pl.pallas_call(
        functools.partial(_proj_kernel, n_rope, n_att),
        grid=(t // PROJ_TM, n_tiles),
        in_specs=[pl.BlockSpec((PROJ_TM, d), lambda i, j: (i, 0)),
                  pl.BlockSpec((1, 6, d), lambda i, j: (i // tiles_per_seq, 0, 0)),
                  pl.BlockSpec((1, d), lambda i, j: (0, 0)),
                  pl.BlockSpec((d, PROJ_TN), lambda i, j: (0, j)),
                  pl.BlockSpec((d, LANES), lambda i, j: (0, 0)),
                  tab_spec, tab_spec, tab_spec],
        out_specs=[pl.BlockSpec((PROJ_TM, PROJ_TN), lambda i, j: (i, jnp.minimum(j, n_att - 1))),
                   pl.BlockSpec((PROJ_TM, PROJ_TN), lambda i, j: (i, jnp.maximum(j - n_att, 0))),
                   pl.BlockSpec((PROJ_TM, LANES), lambda i, j: (i, 0))],
        out_shape=[jax.ShapeDtypeStruct((t, n_att_cols), BF16),
                   jax.ShapeDtypeStruct((t, n - n_att_cols), F32),
                   jax.ShapeDtypeStruct((t, LANES), F32)],
        scratch_shapes=[pltpu.VMEM((PROJ_TM, d), BF16)],
        compiler_params=_cparams(("arbitrary", "arbitrary")),
        name="proj",
    )(x2, mod3, g.reshape(1, d), w_main, w_gate, cos, sa, sb)


def _attn_kernel(lam_init, lam_ref, q_ref, k_ref, v_ref, nrm_ref, o_ref, vt_scr, m_scr, l_scr, acc_scr):
    i = pl.program_id(2)
    tq = ATT_TQ
    nkv = vt_scr.shape[0]

    @pl.when(i == 0)
    def _():
        for jb in range(nkv):
            vt_scr[jb] = v_ref[jb * tq:(jb + 1) * tq, :].astype(F32).T.astype(BF16)

    lv = lam_ref[...]
    lam = (jnp.exp(jnp.sum(lv[0:1] * lv[1:2], axis=-1, keepdims=True))
           - jnp.exp(jnp.sum(lv[2:3] * lv[3:4], axis=-1, keepdims=True)) + lam_init)

    q = q_ref[...] * (DA_QK ** -0.5)
    lane = lax.broadcasted_iota(jnp.int32, (tq, LANES), 1)
    zero = jnp.zeros_like(q)
    q2 = jnp.concatenate([jnp.where(lane < DA_QK, q, zero), jnp.where(lane >= DA_QK, q, zero)], axis=0)

    m_scr[...] = jnp.full(m_scr.shape, NEG_INF, F32)
    l_scr[...] = jnp.zeros(l_scr.shape, F32)
    acc_scr[...] = jnp.zeros(acc_scr.shape, F32)

    def step(jb, diagonal):
        start = pl.multiple_of(jb * tq, tq)
        k = k_ref[pl.ds(start, tq), :]
        s = lax.dot_general(k, q2, (((1,), (1,)), ((), ())), preferred_element_type=F32)
        if diagonal:
            kpos = lax.broadcasted_iota(jnp.int32, s.shape, 0)
            col = lax.broadcasted_iota(jnp.int32, s.shape, 1)
            qpos = jnp.where(col >= tq, col - tq, col)
            s = jnp.where(kpos <= qpos, s, NEG_INF)
        m_old = m_scr[...]
        m_new = jnp.maximum(m_old, jnp.max(s, axis=0, keepdims=True))
        alpha = jnp.exp(m_old - m_new)
        p = jnp.exp(s - m_new)
        l_scr[...] = alpha * l_scr[...] + jnp.sum(p, axis=0, keepdims=True)
        acc_scr[...] = alpha * acc_scr[...] + jnp.dot(vt_scr[jb], p.astype(BF16),
                                                      preferred_element_type=F32)
        m_scr[...] = m_new

    def body(jb, carry):
        step(jb, False)
        return carry

    lax.fori_loop(0, i, body, 0)
    step(i, True)

    o2 = acc_scr[...] / l_scr[...]
    o = (o2[:, :tq] - lam * o2[:, tq:]).T
    o_ref[...] = (_rms(o, nrm_ref[...]) * (1.0 - lam_init)).astype(o_ref.dtype)


def _attn(pa, lam4, da_norm, batch, seq, heads, lam_init):
    t = pa.shape[0]
    nq = seq // ATT_TQ
    return pl.pallas_call(
        functools.partial(_attn_kernel, lam_init),
        grid=(batch, heads, nq),
        in_specs=[pl.BlockSpec((4, DA_QK), lambda b, h, i: (0, 0)),
                  pl.BlockSpec((ATT_TQ, LANES), lambda b, h, i: (b * nq + i, h)),
                  pl.BlockSpec((seq, LANES), lambda b, h, i: (b, heads + h)),
                  pl.BlockSpec((seq, LANES), lambda b, h, i: (b, 2 * heads + h)),
                  pl.BlockSpec((1, DA_V), lambda b, h, i: (0, 0))],
        out_specs=pl.BlockSpec((ATT_TQ, DA_V), lambda b, h, i: (b * nq + i, h)),
        out_shape=jax.ShapeDtypeStruct((t, heads * DA_V), BF16),
        scratch_shapes=[pltpu.VMEM((nq, DA_V, ATT_TQ), BF16),
                        pltpu.VMEM((1, 2 * ATT_TQ), F32), pltpu.VMEM((1, 2 * ATT_TQ), F32),
                        pltpu.VMEM((DA_V, 2 * ATT_TQ), F32)],
        compiler_params=_cparams(("arbitrary", "arbitrary", "arbitrary")),
        name="attn",
    )(lam4, pa, pa, pa, da_norm.reshape(1, DA_V))


def _softcap(x):
    return GATE_CAP * jnp.tanh(x / GATE_CAP)


def _dwconv_silu(x, w, b):
    seq = x.shape[0]
    row = lax.broadcasted_iota(jnp.int32, x.shape, 0)
    y = x * w[CONV_W - 1:CONV_W]
    for s in range(1, CONV_W):
        xs = jnp.where(row >= s, pltpu.roll(x, s, 0), 0.0)
        y = y + xs * w[CONV_W - 1 - s:CONV_W - s]
    y = y + b
    return y * jax.nn.sigmoid(y)


def _mlstm_kernel(bi_ref, bf_ref, q_ref, k_ref, v_ref, og_ref, cwq_ref, cwk_ref, cbq_ref, cbk_ref,
                  gc_ref, gr_ref, nrm_ref, o_ref,
                  q_scr, k_scr, lic_scr, lfc_scr, lir_scr, lfr_scr, c_scr, m_scr):
    hd = pl.program_id(1)
    L = ML_CHUNK
    seq = q_ref.shape[0]
    nc = seq // L
    bi = bi_ref[hd]
    bf = bf_ref[hd]

    q_scr[...] = _dwconv_silu(q_ref[...], cwq_ref[...], cbq_ref[...]).astype(BF16)
    k_scr[...] = _dwconv_silu(k_ref[...], cwk_ref[...], cbk_ref[...]) * (ML_QK ** -0.5)

    gc = gc_ref[0, 0]
    lic_scr[...] = _softcap(gc[:, 0:1] + bi)
    lfc_scr[...] = jax.nn.log_sigmoid(_softcap(gc[:, 1:2] + bf))
    lir_scr[...] = _softcap(gr_ref[0, 0, 0] + bi)
    lfr_scr[...] = jax.nn.log_sigmoid(_softcap(gr_ref[0, 0, 1] + bf))

    c_scr[...] = jnp.zeros(c_scr.shape, F32)
    m_scr[...] = jnp.zeros(m_scr.shape, F32)

    r_i = lax.broadcasted_iota(jnp.int32, (L, L), 0)
    c_i = lax.broadcasted_iota(jnp.int32, (L, L), 1)
    tril = (c_i <= r_i)
    tril_f = tril.astype(F32)
    triu_f = (r_i <= c_i).astype(F32)
    ones_col = (lax.broadcasted_iota(jnp.int32, (L, LANES), 1) == 0).astype(BF16)
    nrm = nrm_ref[...]
    hp = lax.Precision.HIGHEST

    def body(c, carry):
        start = pl.multiple_of(c * L, L)
        sl = pl.ds(start, L)
        qc = q_scr[sl, :]
        kc = k_scr[sl, :]
        vc = v_ref[sl, :].astype(BF16)
        lfc = lfc_scr[sl, :]
        lic = lic_scr[sl, :]
        lfr = lfr_scr[pl.ds(c, 1), :]
        lir = lir_scr[pl.ds(c, 1), :]
        m = m_scr[...]

        b_c = jnp.dot(tril_f, jnp.broadcast_to(lfc, (L, L)), preferred_element_type=F32, precision=hp)
        b_r = jnp.dot(jnp.broadcast_to(lfr, (L, L)), triu_f, preferred_element_type=F32, precision=hp)
        dmat = jnp.where(tril, b_c - b_r + lir, NEG_INF)
        bcol = b_c[:, 0:1]
        inter = bcol + m
        m_j = jnp.maximum(inter, jnp.max(dmat, axis=-1, keepdims=True))
        w_intra = jnp.exp(dmat - m_j)
        w_inter = jnp.exp(inter - m_j)

        sqk = lax.dot_general(qc, kc.astype(BF16), (((1,), (1,)), ((), ())), preferred_element_type=F32)
        qk = sqk * w_intra
        q_c = jnp.dot(qc, c_scr[...].astype(BF16), preferred_element_type=F32)
        num = w_inter * q_c[:, :ML_V] + jnp.dot(qk.astype(BF16), vc, preferred_element_type=F32)
        den = w_inter * q_c[:, ML_V:ML_V + 1] + jnp.sum(qk, axis=-1, keepdims=True)
        hh = num / jnp.maximum(jnp.abs(den), jnp.exp(-m_j))
        og = og_ref[sl, :]
        o_ref[sl, :] = (_rms(hh, nrm) * jax.nn.sigmoid(og)).astype(o_ref.dtype)

        b_l = b_c[L - 1:L, 0:1]
        logw = b_l - bcol + lic
        m_new = jnp.maximum(b_l + m, jnp.max(logw, axis=0, keepdims=True))
        decay = jnp.exp(b_l + m - m_new)
        ws = jnp.exp(logw - m_new)
        kw = (kc * ws).astype(BF16)
        vext = jnp.concatenate([vc, ones_col], axis=1)
        upd = lax.dot_general(kw, vext, (((0,), (0,)), ((), ())), preferred_element_type=F32)
        c_scr[...] = decay * c_scr[...] + upd
        m_scr[...] = m_new
        return carry

    lax.fori_loop(0, nc, body, 0)


def _mlstm(pm, gates_col, gates_row, conv_w, conv_b, b_ig, b_fg, ml_norm, batch, seq, heads):
    t = pm.shape[0]
    nc = seq // ML_CHUNK
    qk_w = heads * ML_QK
    vblk = 2 * qk_w // ML_V
    smem = pl.BlockSpec(memory_space=pltpu.SMEM)
    return pl.pallas_call(
        _mlstm_kernel,
        grid=(batch, heads),
        in_specs=[smem, smem,
                  pl.BlockSpec((seq, ML_QK), lambda b, h: (b, h)),
                  pl.BlockSpec((seq, ML_QK), lambda b, h: (b, heads + h)),
                  pl.BlockSpec((seq, ML_V), lambda b, h: (b, vblk + h)),
                  pl.BlockSpec((seq, ML_V), lambda b, h: (b, vblk + heads + h)),
                  pl.BlockSpec((CONV_W, ML_QK), lambda b, h: (0, h)),
                  pl.BlockSpec((CONV_W, ML_QK), lambda b, h: (0, heads + h)),
                  pl.BlockSpec((1, ML_QK), lambda b, h: (0, h)),
                  pl.BlockSpec((1, ML_QK), lambda b, h: (0, heads + h)),
                  pl.BlockSpec((1, 1, seq, 2), lambda b, h: (b, h, 0, 0)),
                  pl.BlockSpec((1, 1, 2, nc, ML_CHUNK), lambda b, h: (b, h, 0, 0, 0)),
                  pl.BlockSpec((1, ML_V), lambda b, h: (0, 0))],
        out_specs=pl.BlockSpec((seq, ML_V), lambda b, h: (b, h)),
        out_shape=jax.ShapeDtypeStruct((t, heads * ML_V), BF16),
        scratch_shapes=[pltpu.VMEM((seq, ML_QK), BF16), pltpu.VMEM((seq, ML_QK), F32),
                        pltpu.VMEM((seq, 1), F32), pltpu.VMEM((seq, 1), F32),
                        pltpu.VMEM((nc, ML_CHUNK), F32), pltpu.VMEM((nc, ML_CHUNK), F32),
                        pltpu.VMEM((ML_QK, ML_V + LANES), F32), pltpu.VMEM((1, 1), F32)],
        compiler_params=_cparams(("arbitrary", "arbitrary")),
        name="mlstm",
    )(b_ig, b_fg, pm, pm, pm, pm, conv_w, conv_w, conv_b.reshape(1, -1), conv_b.reshape(1, -1),
      gates_col, gates_row, ml_norm.reshape(1, ML_V))


def _outproj_kernel(oa_ref, om_ref, w_ref, x_ref, mod_ref, g_ref, x1_ref, h2_ref):
    ka = oa_ref.shape[1]
    mixed = (jnp.dot(oa_ref[...], w_ref[:ka, :], preferred_element_type=F32)
             + jnp.dot(om_ref[...], w_ref[ka:, :], preferred_element_type=F32))
    x1 = x_ref[...] + mod_ref[0, 2:3, :] * mixed
    x1_ref[...] = x1
    h2 = _rms(x1, g_ref[...]) * (1.0 + mod_ref[0, 4:5, :]) + mod_ref[0, 3:4, :]
    h2_ref[...] = h2.astype(BF16)


def _outproj(oa, om, w_out, x2, mod3, g_ffn, seq):
    t, d = x2.shape
    tiles_per_seq = seq // OUT_TM
    return pl.pallas_call(
        _outproj_kernel,
        grid=(t // OUT_TM,),
        in_specs=[pl.BlockSpec((OUT_TM, oa.shape[1]), lambda i: (i, 0)),
                  pl.BlockSpec((OUT_TM, om.shape[1]), lambda i: (i, 0)),
                  pl.BlockSpec(w_out.shape, lambda i: (0, 0)),
                  pl.BlockSpec((OUT_TM, d), lambda i: (i, 0)),
                  pl.BlockSpec((1, 6, d), lambda i: (i // tiles_per_seq, 0, 0)),
                  pl.BlockSpec((1, d), lambda i: (0, 0))],
        out_specs=[pl.BlockSpec((OUT_TM, d), lambda i: (i, 0)),
                   pl.BlockSpec((OUT_TM, d), lambda i: (i, 0))],
        out_shape=[jax.ShapeDtypeStruct((t, d), F32), jax.ShapeDtypeStruct((t, d), BF16)],
        compiler_params=_cparams(("arbitrary",)),
        name="outproj",
    )(oa, om, w_out, x2, mod3, g_ffn.reshape(1, d))


_CAND_ROWS = 16 + 8 + 6 * 8 + 8


def _rank16(s, top_scr):
    n, tm = s.shape
    iota = lax.broadcasted_iota(jnp.int32, (n, tm), 0).astype(F32)
    rank = jnp.full((n, tm), float(PEER_TOPK), F32)
    for r in range(PEER_TOPK):
        mx = jnp.max(s, axis=0, keepdims=True)
        first = jnp.min(jnp.where(s == mx, iota, float(n)), axis=0, keepdims=True)
        sel = iota == first
        rank = jnp.where(sel, float(r), rank)
        s = jnp.where(sel, NEG_INF, s)
        top_scr[r:r + 1, :] = mx
    return rank


def _peer_sel_kernel(h2_ref, w_ref, keys_ref, r2_ref, e2_ref, n1_ref, e1_ref, t1_scr, t2_scr):
    k = PEER_TOPK
    half = PEER_NKEYS
    q = jnp.dot(h2_ref[...], w_ref[...], preferred_element_type=F32).astype(BF16)
    nt = (((1,), (1,)), ((), ()))
    s1 = lax.dot_general(keys_ref[0, 0], q[:, :half], nt, preferred_element_type=F32)
    s2 = lax.dot_general(keys_ref[0, 1], q[:, half:], nt, preferred_element_type=F32)
    r1 = _rank16(s1, t1_scr)
    r2 = _rank16(s2, t2_scr)
    t1 = t1_scr[...]
    t2 = t2_scr[...]
    tm = s1.shape[1]

    brow = lax.broadcasted_iota(jnp.int32, (SUBLANES, tm), 0)
    pieces = [t1[0:1] + t2, t1[1:2] + t2[0:SUBLANES]]
    for a in range(2, SUBLANES):
        pieces.append(jnp.where(brow < k // (a + 1), t1[a:a + 1] + t2[0:SUBLANES], NEG_INF))
    pieces.append(t1[SUBLANES:] + t2[0:1])
    cand = jnp.concatenate(pieces, axis=0)
    cmax = cand[0:1]

    work = cand
    iota = lax.broadcasted_iota(jnp.int32, cand.shape, 0).astype(F32)
    taken = jnp.zeros(cand.shape, F32)
    for _ in range(k):
        mx = jnp.max(work, axis=0, keepdims=True)
        first = jnp.min(jnp.where(work == mx, iota, float(_CAND_ROWS)), axis=0, keepdims=True)
        sel = iota == first
        taken = jnp.where(sel, 1.0, taken)
        work = jnp.where(sel, NEG_INF, work)
    z = jnp.sum(jnp.where(taken > 0.0, jnp.exp(cand - cmax), 0.0), axis=0, keepdims=True)

    n_rows = [jnp.sum(taken[0:16], axis=0, keepdims=True)]
    for a in range(1, SUBLANES):
        lo = 16 + (a - 1) * SUBLANES
        n_rows.append(jnp.sum(taken[lo:lo + SUBLANES], axis=0, keepdims=True))
    base = 16 + 7 * SUBLANES
    for a in range(SUBLANES, k):
        n_rows.append(taken[base + a - SUBLANES:base + a - SUBLANES + 1])
    n1 = jnp.zeros(r1.shape, F32)
    for a in range(k):
        n1 = jnp.where(r1 == float(a), n_rows[a], n1)

    r2_ref[0] = r2.astype(r2_ref.dtype)
    e2_ref[0] = jnp.exp(s2 - t2[0:1]).astype(e2_ref.dtype)
    n1_ref[0] = n1
    e1_ref[0] = jnp.exp(s1 - t1[0:1]) / z


def _peer_sel(h2, w_pq, keys):
    t, d = h2.shape
    nh = PEER_HEADS
    qd = 2 * PEER_NKEYS
    tab = jax.ShapeDtypeStruct((nh, PEER_NKEYS, t), F32)
    tab16 = jax.ShapeDtypeStruct((nh, PEER_NKEYS, t), BF16)
    tab_spec = pl.BlockSpec((1, PEER_NKEYS, SEL_TM), lambda i, h: (h, 0, i))
    return pl.pallas_call(
        _peer_sel_kernel,
        grid=(t // SEL_TM, nh),
        in_specs=[pl.BlockSpec((SEL_TM, d), lambda i, h: (i, 0)),
                  pl.BlockSpec((d, qd), lambda i, h: (0, h)),
                  pl.BlockSpec((1, 2, PEER_NKEYS, PEER_NKEYS), lambda i, h: (h, 0, 0, 0))],
        out_specs=[tab_spec, tab_spec, tab_spec, tab_spec],
        out_shape=[tab16, tab16, tab, tab],
        scratch_shapes=[pltpu.VMEM((PEER_TOPK, SEL_TM), F32), pltpu.VMEM((PEER_TOPK, SEL_TM), F32)],
        compiler_params=_cparams(("arbitrary", "arbitrary")),
        name="peer_sel",
    )(h2, w_pq, keys)


def _peer_ffn_kernel(h2_ref, u0_ref, uodd_ref, unext_ref, vt_ref, r2_ref, e2_ref, n1_ref, e1_ref, x1_ref,
                     mod_ref, modo_ref, g_ref, o_ref, acc_scr, a0_scr, a1_scr):
    c = pl.program_id(1)
    nk = PEER_NKEYS
    per = FFN_EC // nk
    nt = (((1,), (1,)), ((), ()))
    zero = jnp.zeros((), BF16)

    def preact(u_ref):
        return lax.dot_general(u_ref[...], h2_ref[...], nt, preferred_element_type=F32)

    def apply(a_scr, chunk, col0):
        halves = []
        for ii in range(per):
            i = chunk * per + ii
            a = a_scr[ii * nk:(ii + 1) * nk, :]
            gsum = None
            for hd in range(PEER_HEADS):
                n1 = n1_ref[hd, pl.ds(i, 1), :].astype(BF16)
                e1 = e1_ref[hd, pl.ds(i, 1), :].astype(BF16)
                g = jnp.where(r2_ref[hd] < n1, e2_ref[hd], zero) * e1
                gsum = g if gsum is None else gsum + g
            act = 0.5 * a * (1.0 + lax.erf(a * (2.0 ** -0.5)))
            halves.append(gsum * act.astype(BF16))
            if ii % 2 == 1:
                w = jnp.concatenate(halves, axis=0)
                halves = []
                lo = col0 + (ii - 1) * nk
                acc_scr[...] += jnp.dot(vt_ref[:, lo:lo + 2 * nk], w, preferred_element_type=F32)

    @pl.when(c == 0)
    def _():
        acc_scr[...] = jnp.zeros(acc_scr.shape, F32)
        a0_scr[...] = preact(u0_ref)

    a1_scr[...] = preact(uodd_ref)
    apply(a0_scr, 2 * c, 0)
    a0_scr[...] = preact(unext_ref)
    apply(a1_scr, 2 * c + 1, FFN_EC)

    @pl.when(c == pl.num_programs(1) - 1)
    def _():
        y = acc_scr[...].T
        x2 = x1_ref[...] + mod_ref[0, 5:6, :] * y
        o_ref[...] = _rms(x2, g_ref[...]) * (1.0 + modo_ref[0, 1:2, :]) + modo_ref[0, 0:1, :]


def _peer_ffn(h2, u, vt, tabs, x1, mod3, modo3, g_final, seq):
    t, d = h2.shape
    ne = u.shape[0]
    tiles_per_seq = seq // FFN_TB
    tab_spec = pl.BlockSpec((PEER_HEADS, PEER_NKEYS, FFN_TB), lambda i, c: (0, 0, i))
    nchunk = ne // FFN_EC
    once = pl.Buffered(1)
    tab_once = pl.BlockSpec((PEER_HEADS, PEER_NKEYS, FFN_TB), lambda i, c: (0, 0, i), pipeline_mode=once)
    return pl.pallas_call(
        _peer_ffn_kernel,
        grid=(t // FFN_TB, nchunk // 2),
        in_specs=[pl.BlockSpec((FFN_TB, d), lambda i, c: (i, 0)),
                  pl.BlockSpec((FFN_EC, d), lambda i, c: (0, 0), pipeline_mode=once),
                  pl.BlockSpec((FFN_EC, d), lambda i, c: (2 * c + 1, 0)),
                  pl.BlockSpec((FFN_EC, d), lambda i, c: (jnp.minimum(2 * c + 2, nchunk - 1), 0)),
                  pl.BlockSpec((d, 2 * FFN_EC), lambda i, c: (0, c)),
                  tab_spec, tab_spec, tab_once, tab_once,
                  pl.BlockSpec((FFN_TB, d), lambda i, c: (i, 0), pipeline_mode=once),
                  pl.BlockSpec((1, 6, d), lambda i, c: (i // tiles_per_seq, 0, 0)),
                  pl.BlockSpec((1, 2, d), lambda i, c: (i // tiles_per_seq, 0, 0)),
                  pl.BlockSpec((1, d), lambda i, c: (0, 0))],
        out_specs=pl.BlockSpec((FFN_TB, d), lambda i, c: (i, 0)),
        out_shape=jax.ShapeDtypeStruct((t, d), F32),
        scratch_shapes=[pltpu.VMEM((d, FFN_TB), F32), pltpu.VMEM((FFN_EC, FFN_TB), F32),
                        pltpu.VMEM((FFN_EC, FFN_TB), F32)],
        compiler_params=pltpu.CompilerParams(dimension_semantics=("arbitrary", "arbitrary"),
                                             vmem_limit_bytes=FFN_VMEM_LIMIT),
        name="peer_ffn",
    )(h2, u, u, u, vt, *tabs, x1, mod3, modo3, g_final.reshape(1, d))


def kernel(x, c, w_ada, b_ada, g_mix, w_in, conv_w, conv_b, b_igate, b_fgate, lambda_q1, lambda_k1,
           lambda_q2, lambda_k2, da_norm, ml_norm, w_out, g_ffn, w_pq, sub_keys, peer_u, peer_v,
           w_ada_final, b_ada_final, g_final):
    batch, seq, d = x.shape
    depth = w_ada.shape[0]
    t = batch * seq
    da_heads = d // 256
    ml_heads = d // 512
    att_cols = 3 * da_heads * DA_V
    rope_cols = 2 * da_heads * DA_V
    main_cols = att_cols + 2 * ml_heads * ML_QK + 2 * ml_heads * ML_V
    assert batch <= SUBLANES and seq % PROJ_TM == 0 and seq % ATT_TQ == 0 and seq % ML_CHUNK == 0
    assert w_in.shape[2] == main_cols + 2 * ml_heads

    c8 = jnp.zeros((SUBLANES, d), F32).at[:batch].set(c.astype(F32))
    modo3 = _ada(c8, w_ada_final, b_ada_final)[:batch].reshape(batch, 2, d)
    xt = x.reshape(t, d)

    for l in range(depth):
        mod3 = _ada(c8, w_ada[l], b_ada[l])[:batch].reshape(batch, 6, d)
        lam_init = 0.8 - 0.6 * math.exp(-0.3 * l)

        w_main = w_in[l, :, :main_cols].astype(BF16)
        w_gate = jnp.zeros((d, LANES), BF16).at[:, :2 * ml_heads].set(w_in[l, :, main_cols:].astype(BF16))
        pa, pm, gates = _proj(xt, mod3, g_mix[l], w_main, w_gate, seq, att_cols, rope_cols)

        lam4 = jnp.stack([lambda_q1[l], lambda_k1[l], lambda_q2[l], lambda_k2[l]]).astype(F32)
        oa = _attn(pa, lam4, da_norm[l], batch, seq, da_heads, lam_init)

        nc = seq // ML_CHUNK
        g8 = gates[:, :2 * ml_heads].reshape(batch, seq, 2, ml_heads)
        gates_col = g8.transpose(0, 3, 1, 2)
        gates_row = g8.transpose(0, 3, 2, 1).reshape(batch, ml_heads, 2, nc, ML_CHUNK)
        om = _mlstm(pm, gates_col, gates_row, conv_w[l], conv_b[l], b_igate[l], b_fgate[l], ml_norm[l],
                    batch, seq, ml_heads)

        x1, h2 = _outproj(oa, om, w_out[l].astype(BF16), xt, mod3, g_ffn[l], seq)

        tabs = _peer_sel(h2, w_pq[l].astype(BF16), sub_keys[l].astype(BF16))
        assert depth == 1
        xt = _peer_ffn(h2, peer_u[l].astype(BF16), peer_v[l].T.astype(BF16), tabs, x1, mod3, modo3,
                       g_final, seq)

    return xt.reshape(batch, seq, d)
```

```python
import functools
import math

import numpy as np
import jax
import jax.numpy as jnp
from jax import lax
from jax.experimental import pallas as pl
from jax.experimental.pallas import tpu as pltpu

F32 = jnp.float32
BF16 = jnp.bfloat16

DA_QK = 64
DA_V = 128
ML_QK = 128
ML_V = 256
CONV_W = 4
GATE_CAP = 15.0
ROPE_THETA = 10000.0
PEER_HEADS = 8
PEER_NKEYS = 128
PEER_TOPK = 16
EPS = 1e-6

LANES = 128
SUBLANES = 8
VMEM_LIMIT = 56 * 1024 * 1024
FFN_VMEM_LIMIT = 60 * 1024 * 1024

ADA_TN = 1024
PROJ_TM = 1024
PROJ_TN = 512
ATT_TQ = 512
ATT_TK = 512
ATT_ONES_ROWS = 16
ML_CHUNK = 256
OUT_TM = 256
SEL_TM = 256
FFN_TB = 512
FFN_EC = 512

NEG_INF = float("-inf")


def _cparams(sem):
    return pltpu.CompilerParams(dimension_semantics=sem, vmem_limit_bytes=VMEM_LIMIT)


def _rms(x, g):
    return x * lax.rsqrt(jnp.mean(x * x, axis=-1, keepdims=True) + EPS) * g


def _ada_kernel(c_ref, w_ref, b_ref, o_ref):
    c = c_ref[...]
    cs = c * jax.nn.sigmoid(c)
    o_ref[...] = jnp.dot(cs, w_ref[...], preferred_element_type=F32,
                         precision=lax.Precision.HIGHEST) + b_ref[...]


def _ada(c8, w, b):
    d, n = w.shape
    return pl.pallas_call(
        _ada_kernel,
        grid=(n // ADA_TN,),
        in_specs=[pl.BlockSpec((SUBLANES, d), lambda j: (0, 0)),
                  pl.BlockSpec((d, ADA_TN), lambda j: (0, j)),
                  pl.BlockSpec((1, ADA_TN), lambda j: (0, j))],
        out_specs=pl.BlockSpec((SUBLANES, ADA_TN), lambda j: (0, j)),
        out_shape=jax.ShapeDtypeStruct((SUBLANES, n), F32),
        compiler_params=_cparams(("arbitrary",)),
        name="ada",
    )(c8, w, b.reshape(1, n))


def _proj_kernel(n_rope, n_att, x_ref, mod_ref, g_ref, w_ref, wg_ref, cos_ref, sa_ref, sb_ref,
                 pa_ref, pm_ref, gate_ref, h_scr):
    j = pl.program_id(1)

    @pl.when(j == 0)
    def _():
        y = _rms(x_ref[...], g_ref[...])
        h = y * (1.0 + mod_ref[0, 1:2, :]) + mod_ref[0, 0:1, :]
        hb = h.astype(BF16)
        h_scr[...] = hb
        gate_ref[...] = jnp.dot(hb, wg_ref[...], preferred_element_type=F32)

    acc = jnp.dot(h_scr[...], w_ref[...], preferred_element_type=F32)

    @pl.when(j < n_rope)
    def _():
        cos, sa, sb = cos_ref[...], sa_ref[...], sb_ref[...]
        for g in range(PROJ_TN // LANES):
            xg = acc[:, g * LANES:(g + 1) * LANES]
            r = (xg * cos + pltpu.roll(xg, LANES - DA_QK // 2, 1) * sa
                 + pltpu.roll(xg, DA_QK // 2, 1) * sb)
            pa_ref[:, g * LANES:(g + 1) * LANES] = r.astype(BF16)

    @pl.when(jnp.logical_and(j >= n_rope, j < n_att))
    def _():
        pa_ref[...] = acc.astype(BF16)

    @pl.when(j >= n_att)
    def _():
        pm_ref[...] = acc


def _rope_tables(seq):
    half = DA_QK // 2
    inv = ROPE_THETA ** (-jnp.arange(half, dtype=F32) / half)
    ang = jnp.arange(seq, dtype=F32)[:, None] * inv[None, :]
    lane = np.arange(LANES)
    first = (lane % DA_QK) < half
    cos = jnp.cos(ang)[:, lane % half]
    sin = jnp.sin(ang)[:, lane % half]
    sa = jnp.where(first[None, :], -sin, 0.0)
    sb = jnp.where(first[None, :], 0.0, sin)
    return cos, sa, sb


def _proj(x2, mod3, g, w_main, w_gate, seq, n_att_cols, n_rope_cols):
    t, d = x2.shape
    n = w_main.shape[1]
    n_att = n_att_cols // PROJ_TN
    n_rope = n_rope_cols // PROJ_TN
    n_tiles = n // PROJ_TN
    tiles_per_seq = seq // PROJ_TM
    cos, sa, sb = _rope_tables(seq)
    tab_spec = pl.BlockSpec((PROJ_TM, LANES), lambda i, j: (i % tiles_per_seq, 0))
    return pl.pallas_call(
        functools.partial(_proj_kernel, n_rope, n_att),
        grid=(t // PROJ_TM, n_tiles),
        in_specs=[pl.BlockSpec((PROJ_TM, d), lambda i, j: (i, 0)),
                  pl.BlockSpec((1, 6, d), lambda i, j: (i // tiles_per_seq, 0, 0)),
                  pl.BlockSpec((1, d), lambda i, j: (0, 0)),
                  pl.BlockSpec((d, PROJ_TN), lambda i, j: (0, j)),
                  pl.BlockSpec((d, LANES), lambda i, j: (0, 0)),
                  tab_spec, tab_spec, tab_spec],
        out_specs=[pl.BlockSpec((PROJ_TM, PROJ_TN), lambda i, j: (i, jnp.minimum(j, n_att - 1))),
                   pl.BlockSpec((PROJ_TM, PROJ_TN), lambda i, j: (i, jnp.maximum(j - n_att, 0))),
                   pl.BlockSpec((PROJ_TM, LANES), lambda i, j: (i, 0))],
        out_shape=[jax.ShapeDtypeStruct((t, n_att_cols), BF16),
                   jax.ShapeDtypeStruct((t, n - n_att_cols), F32),
                   jax.ShapeDtypeStruct((t, LANES), F32)],
        scratch_shapes=[pltpu.VMEM((PROJ_TM, d), BF16)],
        compiler_params=_cparams(("arbitrary", "arbitrary")),
        name="proj",
    )(x2, mod3, g.reshape(1, d), w_main, w_gate, cos, sa, sb)


def _attn_kernel(lam_init, lam_ref, q_ref, k_ref, v_ref, nrm_ref, o_ref, vt_scr, m_scr, acc_scr):
    i = pl.program_id(2)
    tq = ATT_TQ
    tk = ATT_TK
    nkv = vt_scr.shape[0]

    @pl.when(i == 0)
    def _():
        for jb in range(nkv):
            vt_scr[jb, :DA_V, :] = v_ref[jb * tk:(jb + 1) * tk, :].astype(F32).T.astype(BF16)
            vt_scr[jb, DA_V:, :] = jnp.ones((ATT_ONES_ROWS, tk), BF16)

    lv = lam_ref[...]
    lam = (jnp.exp(jnp.sum(lv[0:1] * lv[1:2], axis=-1, keepdims=True))
           - jnp.exp(jnp.sum(lv[2:3] * lv[3:4], axis=-1, keepdims=True)) + lam_init)

    q = q_ref[...] * (DA_QK ** -0.5)
    lane = lax.broadcasted_iota(jnp.int32, (tq, LANES), 1)
    zero = jnp.zeros_like(q)
    q2 = jnp.concatenate([jnp.where(lane < DA_QK, q, zero), jnp.where(lane >= DA_QK, q, zero)], axis=0)

    m_scr[...] = jnp.full(m_scr.shape, NEG_INF, F32)
    acc_scr[...] = jnp.zeros(acc_scr.shape, F32)

    def step(jb, diagonal):
        start = pl.multiple_of(jb * tk, tk)
        k = k_ref[pl.ds(start, tk), :]
        s = lax.dot_general(k, q2, (((1,), (1,)), ((), ())), preferred_element_type=F32)
        if diagonal:
            kpos = lax.broadcasted_iota(jnp.int32, s.shape, 0) + jb * tk
            col = lax.broadcasted_iota(jnp.int32, s.shape, 1)
            qpos = jnp.where(col >= tq, col - tq, col) + i * tq
            s = jnp.where(kpos <= qpos, s, NEG_INF)
        m_old = m_scr[...]
        m_new = jnp.maximum(m_old, jnp.max(s, axis=0, keepdims=True))
        p = jnp.exp(s - m_new).astype(BF16)
        acc_scr[...] = jnp.exp(m_old - m_new) * acc_scr[...] + jnp.dot(
            vt_scr[jb], p, preferred_element_type=F32)
        m_scr[...] = m_new

    def body(jb, carry):
        step(jb, False)
        return carry

    last = (i * tq) // tk
    lax.fori_loop(0, last, body, 0)
    step(last, True)

    o2 = acc_scr[:DA_V, :] / acc_scr[DA_V:DA_V + 1, :]
    o = (o2[:, :tq] - lam * o2[:, tq:]).T
    o_ref[...] = (_rms(o, nrm_ref[...]) * (1.0 - lam_init)).astype(o_ref.dtype)


def _attn(pa, lam4, da_norm, batch, seq, heads, lam_init):
    t = pa.shape[0]
    nq = seq // ATT_TQ
    return pl.pallas_call(
        functools.partial(_attn_kernel, lam_init),
        grid=(batch, heads, nq),
        in_specs=[pl.BlockSpec((4, DA_QK), lambda b, h, i: (0, 0)),
                  pl.BlockSpec((ATT_TQ, LANES), lambda b, h, i: (b * nq + i, h)),
                  pl.BlockSpec((seq, LANES), lambda b, h, i: (b, heads + h)),
                  pl.BlockSpec((seq, LANES), lambda b, h, i: (b, 2 * heads + h)),
                  pl.BlockSpec((1, DA_V), lambda b, h, i: (0, 0))],
        out_specs=pl.BlockSpec((ATT_TQ, DA_V), lambda b, h, i: (b * nq + i, h)),
        out_shape=jax.ShapeDtypeStruct((t, heads * DA_V), BF16),
        scratch_shapes=[pltpu.VMEM((seq // ATT_TK, DA_V + ATT_ONES_ROWS, ATT_TK), BF16),
                        pltpu.VMEM((1, 2 * ATT_TQ), F32),
                        pltpu.VMEM((DA_V + ATT_ONES_ROWS, 2 * ATT_TQ), F32)],
        compiler_params=_cparams(("arbitrary", "arbitrary", "arbitrary")),
        name="attn",
    )(lam4, pa, pa, pa, da_norm.reshape(1, DA_V))


def _softcap(x):
    return GATE_CAP * jnp.tanh(x / GATE_CAP)


def _dwconv_silu(x, w, b):
    seq = x.shape[0]
    row = lax.broadcasted_iota(jnp.int32, x.shape, 0)
    y = x * w[CONV_W - 1:CONV_W]
    for s in range(1, CONV_W):
        xs = jnp.where(row >= s, pltpu.roll(x, s, 0), 0.0)
        y = y + xs * w[CONV_W - 1 - s:CONV_W - s]
    y = y + b
    return y * jax.nn.sigmoid(y)


def _mlstm_kernel(bi_ref, bf_ref, q_ref, k_ref, v_ref, og_ref, cwq_ref, cwk_ref, cbq_ref, cbk_ref,
                  gc_ref, gr_ref, nrm_ref, o_ref,
                  q_scr, k_scr, lic_scr, lfc_scr, lir_scr, lfr_scr, c_scr, m_scr):
    hd = pl.program_id(1)
    L = ML_CHUNK
    seq = q_ref.shape[0]
    nc = seq // L
    bi = bi_ref[hd]
    bf = bf_ref[hd]

    q_scr[...] = _dwconv_silu(q_ref[...], cwq_ref[...], cbq_ref[...]).astype(BF16)
    k_scr[...] = _dwconv_silu(k_ref[...], cwk_ref[...], cbk_ref[...]) * (ML_QK ** -0.5)

    gc = gc_ref[0, 0]
    lic_scr[...] = _softcap(gc[:, 0:1] + bi)
    lfc_scr[...] = jax.nn.log_sigmoid(_softcap(gc[:, 1:2] + bf))
    lir_scr[...] = _softcap(gr_ref[0, 0, 0] + bi)
    lfr_scr[...] = jax.nn.log_sigmoid(_softcap(gr_ref[0, 0, 1] + bf))

    c_scr[...] = jnp.zeros(c_scr.shape, F32)
    m_scr[...] = jnp.zeros(m_scr.shape, F32)

    r_i = lax.broadcasted_iota(jnp.int32, (L, L), 0)
    c_i = lax.broadcasted_iota(jnp.int32, (L, L), 1)
    tril = (c_i <= r_i)
    tril_f = tril.astype(F32)
    triu_f = (r_i <= c_i).astype(F32)
    ones_col = (lax.broadcasted_iota(jnp.int32, (L, LANES), 1) == 0).astype(BF16)
    nrm = nrm_ref[...]
    hp = lax.Precision.HIGHEST

    def body(c, carry):
        start = pl.multiple_of(c * L, L)
        sl = pl.ds(start, L)
        qc = q_scr[sl, :]
        kc = k_scr[sl, :]
        vc = v_ref[sl, :].astype(BF16)
        lfc = lfc_scr[sl, :]
        lic = lic_scr[sl, :]
        lfr = lfr_scr[pl.ds(c, 1), :]
        lir = lir_scr[pl.ds(c, 1), :]
        m = m_scr[...]

        b_c = jnp.dot(tril_f, jnp.broadcast_to(lfc, (L, L)), preferred_element_type=F32, precision=hp)
        b_r = jnp.dot(jnp.broadcast_to(lfr, (L, L)), triu_f, preferred_element_type=F32, precision=hp)
        dmat = jnp.where(tril, b_c - b_r + lir, NEG_INF)
        bcol = b_c[:, 0:1]
        inter = bcol + m
        m_j = jnp.maximum(inter, jnp.max(dmat, axis=-1, keepdims=True))
        w_intra = jnp.exp(dmat - m_j)
        w_inter = jnp.exp(inter - m_j)

        sqk = lax.dot_general(qc, kc.astype(BF16), (((1,), (1,)), ((), ())), preferred_element_type=F32)
        qk = sqk * w_intra
        q_c = jnp.dot(qc, c_scr[...].astype(BF16), preferred_element_type=F32)
        num = w_inter * q_c[:, :ML_V] + jnp.dot(qk.astype(BF16), vc, preferred_element_type=F32)
        den = w_inter * q_c[:, ML_V:ML_V + 1] + jnp.sum(qk, axis=-1, keepdims=True)
        hh = num / jnp.maximum(jnp.abs(den), jnp.exp(-m_j))
        og = og_ref[sl, :]
        o_ref[sl, :] = (_rms(hh, nrm) * jax.nn.sigmoid(og)).astype(o_ref.dtype)

        b_l = b_c[L - 1:L, 0:1]
        logw = b_l - bcol + lic
        m_new = jnp.maximum(b_l + m, jnp.max(logw, axis=0, keepdims=True))
        decay = jnp.exp(b_l + m - m_new)
        ws = jnp.exp(logw - m_new)
        kw = (kc * ws).astype(BF16)
        vext = jnp.concatenate([vc, ones_col], axis=1)
        upd = lax.dot_general(kw, vext, (((0,), (0,)), ((), ())), preferred_element_type=F32)
        c_scr[...] = decay * c_scr[...] + upd
        m_scr[...] = m_new
        return carry

    lax.fori_loop(0, nc, body, 0)


def _mlstm(pm, gates_col, gates_row, conv_w, conv_b, b_ig, b_fg, ml_norm, batch, seq, heads):
    t = pm.shape[0]
    nc = seq // ML_CHUNK
    qk_w = heads * ML_QK
    vblk = 2 * qk_w // ML_V
    smem = pl.BlockSpec(memory_space=pltpu.SMEM)
    return pl.pallas_call(
        _mlstm_kernel,
        grid=(batch, heads),
        in_specs=[smem, smem,
                  pl.BlockSpec((seq, ML_QK), lambda b, h: (b, h)),
                  pl.BlockSpec((seq, ML_QK), lambda b, h: (b, heads + h)),
                  pl.BlockSpec((seq, ML_V), lambda b, h: (b, vblk + h)),
                  pl.BlockSpec((seq, ML_V), lambda b, h: (b, vblk + heads + h)),
                  pl.BlockSpec((CONV_W, ML_QK), lambda b, h: (0, h)),
                  pl.BlockSpec((CONV_W, ML_QK), lambda b, h: (0, heads + h)),
                  pl.BlockSpec((1, ML_QK), lambda b, h: (0, h)),
                  pl.BlockSpec((1, ML_QK), lambda b, h: (0, heads + h)),
                  pl.BlockSpec((1, 1, seq, 2), lambda b, h: (b, h, 0, 0)),
                  pl.BlockSpec((1, 1, 2, nc, ML_CHUNK), lambda b, h: (b, h, 0, 0, 0)),
                  pl.BlockSpec((1, ML_V), lambda b, h: (0, 0))],
        out_specs=pl.BlockSpec((seq, ML_V), lambda b, h: (b, h)),
        out_shape=jax.ShapeDtypeStruct((t, heads * ML_V), BF16),
        scratch_shapes=[pltpu.VMEM((seq, ML_QK), BF16), pltpu.VMEM((seq, ML_QK), F32),
                        pltpu.VMEM((seq, 1), F32), pltpu.VMEM((seq, 1), F32),
                        pltpu.VMEM((nc, ML_CHUNK), F32), pltpu.VMEM((nc, ML_CHUNK), F32),
                        pltpu.VMEM((ML_QK, ML_V + LANES), F32), pltpu.VMEM((1, 1), F32)],
        compiler_params=_cparams(("arbitrary", "arbitrary")),
        name="mlstm",
    )(b_ig, b_fg, pm, pm, pm, pm, conv_w, conv_w, conv_b.reshape(1, -1), conv_b.reshape(1, -1),
      gates_col, gates_row, ml_norm.reshape(1, ML_V))


def _outproj_kernel(oa_ref, om_ref, w_ref, x_ref, mod_ref, g_ref, x1_ref, h2_ref):
    ka = oa_ref.shape[1]
    mixed = (jnp.dot(oa_ref[...], w_ref[:ka, :], preferred_element_type=F32)
             + jnp.dot(om_ref[...], w_ref[ka:, :], preferred_element_type=F32))
    x1 = x_ref[...] + mod_ref[0, 2:3, :] * mixed
    x1_ref[...] = x1
    h2 = _rms(x1, g_ref[...]) * (1.0 + mod_ref[0, 4:5, :]) + mod_ref[0, 3:4, :]
    h2_ref[...] = h2.astype(BF16)


def _outproj(oa, om, w_out, x2, mod3, g_ffn, seq):
    t, d = x2.shape
    tiles_per_seq = seq // OUT_TM
    return pl.pallas_call(
        _outproj_kernel,
        grid=(t // OUT_TM,),
        in_specs=[pl.BlockSpec((OUT_TM, oa.shape[1]), lambda i: (i, 0)),
                  pl.BlockSpec((OUT_TM, om.shape[1]), lambda i: (i, 0)),
                  pl.BlockSpec(w_out.shape, lambda i: (0, 0)),
                  pl.BlockSpec((OUT_TM, d), lambda i: (i, 0)),
                  pl.BlockSpec((1, 6, d), lambda i: (i // tiles_per_seq, 0, 0)),
                  pl.BlockSpec((1, d), lambda i: (0, 0))],
        out_specs=[pl.BlockSpec((OUT_TM, d), lambda i: (i, 0)),
                   pl.BlockSpec((OUT_TM, d), lambda i: (i, 0))],
        out_shape=[jax.ShapeDtypeStruct((t, d), F32), jax.ShapeDtypeStruct((t, d), BF16)],
        compiler_params=_cparams(("arbitrary",)),
        name="outproj",
    )(oa, om, w_out, x2, mod3, g_ffn.reshape(1, d))


_CAND_ROWS = 16 + 8 + 6 * 8 + 8


def _pick_max(work, iota, exact):
    mx = jnp.max(work, axis=0, keepdims=True)
    if not exact:
        return mx, work == mx
    first = jnp.min(jnp.where(work == mx, iota, float(work.shape[0])), axis=0, keepdims=True)
    return mx, iota == first


def _rank16(s, top_scr, exact):
    iota = lax.broadcasted_iota(jnp.int32, s.shape, 0).astype(F32) if exact else None
    rank = jnp.full(s.shape, float(PEER_TOPK), F32)
    for r in range(PEER_TOPK):
        mx, sel = _pick_max(s, iota, exact)
        rank = jnp.where(sel, float(r), rank)
        s = jnp.where(sel, NEG_INF, s)
        top_scr[r:r + 1, :] = mx
    return rank


def _selection_tables(s1, s2, t1_scr, t2_scr, exact):
    k = PEER_TOPK
    r1 = _rank16(s1, t1_scr, exact)
    r2 = _rank16(s2, t2_scr, exact)
    t1 = t1_scr[...]
    t2 = t2_scr[...]
    tm = s1.shape[1]

    brow = lax.broadcasted_iota(jnp.int32, (SUBLANES, tm), 0)
    pieces = [t1[0:1] + t2, t1[1:2] + t2[0:SUBLANES]]
    for a in range(2, SUBLANES):
        pieces.append(jnp.where(brow < k // (a + 1), t1[a:a + 1] + t2[0:SUBLANES], NEG_INF))
    pieces.append(t1[SUBLANES:] + t2[0:1])
    cand = jnp.concatenate(pieces, axis=0)
    cmax = cand[0:1]

    work = cand
    iota = lax.broadcasted_iota(jnp.int32, cand.shape, 0).astype(F32) if exact else None
    for _ in range(k):
        _, sel = _pick_max(work, iota, exact)
        work = jnp.where(sel, NEG_INF, work)
    taken = jnp.where(work != cand, 1.0, 0.0)
    z = jnp.sum(taken * jnp.exp(cand - cmax), axis=0, keepdims=True)

    n_rows = [jnp.sum(taken[0:16], axis=0, keepdims=True)]
    for a in range(1, SUBLANES):
        lo = 16 + (a - 1) * SUBLANES
        n_rows.append(jnp.sum(taken[lo:lo + SUBLANES], axis=0, keepdims=True))
    base = 16 + 7 * SUBLANES
    for a in range(SUBLANES, k):
        n_rows.append(taken[base + a - SUBLANES:base + a - SUBLANES + 1])
    n1 = jnp.zeros(r1.shape, F32)
    for a in range(k):
        n1 = jnp.where(r1 == float(a), n_rows[a], n1)

    extra = None
    if not exact:
        def count(x):
            return jnp.sum(x, axis=0, keepdims=True) - float(k)
        extra = (count(jnp.where(r1 < float(k), 1.0, 0.0)) + count(jnp.where(r2 < float(k), 1.0, 0.0))
                 + count(taken))
    return (r2, jnp.exp(s2 - t2[0:1]), n1, jnp.exp(s1 - t1[0:1]) / z), extra


def _peer_sel_kernel(h2_ref, w_ref, keys_ref, r2_ref, e2_ref, n1_ref, e1_ref, t1_scr, t2_scr):
    half = PEER_NKEYS
    q = jnp.dot(h2_ref[...], w_ref[...], preferred_element_type=F32).astype(BF16)
    nt = (((1,), (1,)), ((), ()))
    s1 = lax.dot_general(keys_ref[0, 0], q[:, :half], nt, preferred_element_type=F32)
    s2 = lax.dot_general(keys_ref[0, 1], q[:, half:], nt, preferred_element_type=F32)

    def store(tables):
        for ref, val in zip((r2_ref, e2_ref, n1_ref, e1_ref), tables):
            ref[0] = val.astype(ref.dtype)

    tables, extra = _selection_tables(s1, s2, t1_scr, t2_scr, exact=False)
    store(tables)

    @pl.when(jnp.max(extra) > 0.0)
    def _():
        store(_selection_tables(s1, s2, t1_scr, t2_scr, exact=True)[0])


def _peer_sel(h2, w_pq, keys):
    t, d = h2.shape
    nh = PEER_HEADS
    qd = 2 * PEER_NKEYS
    tab = jax.ShapeDtypeStruct((nh, PEER_NKEYS, t), F32)
    tab16 = jax.ShapeDtypeStruct((nh, PEER_NKEYS, t), BF16)
    tab_spec = pl.BlockSpec((1, PEER_NKEYS, SEL_TM), lambda i, h: (h, 0, i))
    return pl.pallas_call(
        _peer_sel_kernel,
        grid=(t // SEL_TM, nh),
        in_specs=[pl.BlockSpec((SEL_TM, d), lambda i, h: (i, 0)),
                  pl.BlockSpec((d, qd), lambda i, h: (0, h)),
                  pl.BlockSpec((1, 2, PEER_NKEYS, PEER_NKEYS), lambda i, h: (h, 0, 0, 0))],
        out_specs=[tab_spec, tab_spec, tab_spec, tab_spec],
        out_shape=[tab16, tab16, tab, tab],
        scratch_shapes=[pltpu.VMEM((PEER_TOPK, SEL_TM), F32), pltpu.VMEM((PEER_TOPK, SEL_TM), F32)],
        compiler_params=_cparams(("arbitrary", "arbitrary")),
        name="peer_sel",
    )(h2, w_pq, keys)


def _peer_ffn_kernel(h2_ref, u0_ref, uodd_ref, unext_ref, vt_ref, r2_ref, e2_ref, n1_ref, e1_ref, x1_ref,
                     mod_ref, modo_ref, g_ref, o_ref, acc_scr, a0_scr, a1_scr):
    c = pl.program_id(1)
    nk = PEER_NKEYS
    per = FFN_EC // nk
    nt = (((1,), (1,)), ((), ()))
    zero = jnp.zeros((), BF16)

    def preact(u_ref):
        return lax.dot_general(u_ref[...], h2_ref[...], nt, preferred_element_type=F32)

    def apply(a_scr, chunk, col0):
        halves = []
        for ii in range(per):
            i = chunk * per + ii
            a = a_scr[ii * nk:(ii + 1) * nk, :]
            gsum = None
            for hd in range(PEER_HEADS):
                n1 = n1_ref[hd, pl.ds(i, 1), :].astype(BF16)
                e1 = e1_ref[hd, pl.ds(i, 1), :].astype(BF16)
                g = jnp.where(r2_ref[hd] < n1, e2_ref[hd], zero) * e1
                gsum = g if gsum is None else gsum + g
            act = 0.5 * a * (1.0 + lax.erf(a * (2.0 ** -0.5)))
            halves.append(gsum * act.astype(BF16))
            if ii % 2 == 1:
                w = jnp.concatenate(halves, axis=0)
                halves = []
                lo = col0 + (ii - 1) * nk
                acc_scr[...] += jnp.dot(vt_ref[:, lo:lo + 2 * nk], w, preferred_element_type=F32)

    @pl.when(c == 0)
    def _():
        acc_scr[...] = jnp.zeros(acc_scr.shape, F32)
        a0_scr[...] = preact(u0_ref)

    a1_scr[...] = preact(uodd_ref)
    apply(a0_scr, 2 * c, 0)
    a0_scr[...] = preact(unext_ref)
    apply(a1_scr, 2 * c + 1, FFN_EC)

    @pl.when(c == pl.num_programs(1) - 1)
    def _():
        y = acc_scr[...].T
        x2 = x1_ref[...] + mod_ref[0, 5:6, :] * y
        o_ref[...] = _rms(x2, g_ref[...]) * (1.0 + modo_ref[0, 1:2, :]) + modo_ref[0, 0:1, :]


def _peer_ffn(h2, u, vt, tabs, x1, mod3, modo3, g_final, seq):
    t, d = h2.shape
    ne = u.shape[0]
    tiles_per_seq = seq // FFN_TB
    tab_spec = pl.BlockSpec((PEER_HEADS, PEER_NKEYS, FFN_TB), lambda i, c: (0, 0, i))
    nchunk = ne // FFN_EC
    once = pl.Buffered(1)
    tab_once = pl.BlockSpec((PEER_HEADS, PEER_NKEYS, FFN_TB), lambda i, c: (0, 0, i), pipeline_mode=once)
    return pl.pallas_call(
        _peer_ffn_kernel,
        grid=(t // FFN_TB, nchunk // 2),
        in_specs=[pl.BlockSpec((FFN_TB, d), lambda i, c: (i, 0)),
                  pl.BlockSpec((FFN_EC, d), lambda i, c: (0, 0), pipeline_mode=once),
                  pl.BlockSpec((FFN_EC, d), lambda i, c: (2 * c + 1, 0)),
                  pl.BlockSpec((FFN_EC, d), lambda i, c: (jnp.minimum(2 * c + 2, nchunk - 1), 0)),
                  pl.BlockSpec((d, 2 * FFN_EC), lambda i, c: (0, c)),
                  tab_spec, tab_spec, tab_once, tab_once,
                  pl.BlockSpec((FFN_TB, d), lambda i, c: (i, 0), pipeline_mode=once),
                  pl.BlockSpec((1, 6, d), lambda i, c: (i // tiles_per_seq, 0, 0)),
                  pl.BlockSpec((1, 2, d), lambda i, c: (i // tiles_per_seq, 0, 0)),
                  pl.BlockSpec((1, d), lambda i, c: (0, 0))],
        out_specs=pl.BlockSpec((FFN_TB, d), lambda i, c: (i, 0)),
        out_shape=jax.ShapeDtypeStruct((t, d), F32),
        scratch_shapes=[pltpu.VMEM((d, FFN_TB), F32), pltpu.VMEM((FFN_EC, FFN_TB), F32),
                        pltpu.VMEM((FFN_EC, FFN_TB), F32)],
        compiler_params=pltpu.CompilerParams(dimension_semantics=("arbitrary", "arbitrary"),
                                             vmem_limit_bytes=FFN_VMEM_LIMIT),
        name="peer_ffn",
    )(h2, u, u, u, vt, *tabs, x1, mod3, modo3, g_final.reshape(1, d))


def kernel(x, c, w_ada, b_ada, g_mix, w_in, conv_w, conv_b, b_igate, b_fgate, lambda_q1, lambda_k1,
           lambda_q2, lambda_k2, da_norm, ml_norm, w_out, g_ffn, w_pq, sub_keys, peer_u, peer_v,
           w_ada_final, b_ada_final, g_final):
    batch, seq, d = x.shape
    depth = w_ada.shape[0]
    t = batch * seq
    da_heads = d // 256
    ml_heads = d // 512
    att_cols = 3 * da_heads * DA_V
    rope_cols = 2 * da_heads * DA_V
    main_cols = att_cols + 2 * ml_heads * ML_QK + 2 * ml_heads * ML_V
    assert batch <= SUBLANES and seq % PROJ_TM == 0 and seq % ATT_TQ == 0 and seq % ML_CHUNK == 0
    assert w_in.shape[2] == main_cols + 2 * ml_heads

    c8 = jnp.zeros((SUBLANES, d), F32).at[:batch].set(c.astype(F32))
    modo3 = _ada(c8, w_ada_final, b_ada_final)[:batch].reshape(batch, 2, d)
    xt = x.reshape(t, d)

    for l in range(depth):
        mod3 = _ada(c8, w_ada[l], b_ada[l])[:batch].reshape(batch, 6, d)
        lam_init = 0.8 - 0.6 * math.exp(-0.3 * l)

        w_main = w_in[l, :, :main_cols].astype(BF16)
        w_gate = jnp.zeros((d, LANES), BF16).at[:, :2 * ml_heads].set(w_in[l, :, main_cols:].astype(BF16))
        pa, pm, gates = _proj(xt, mod3, g_mix[l], w_main, w_gate, seq, att_cols, rope_cols)

        lam4 = jnp.stack([lambda_q1[l], lambda_k1[l], lambda_q2[l], lambda_k2[l]]).astype(F32)
        oa = _attn(pa, lam4, da_norm[l], batch, seq, da_heads, lam_init)

        nc = seq // ML_CHUNK
        g8 = gates[:, :2 * ml_heads].reshape(batch, seq, 2, ml_heads)
        gates_col = g8.transpose(0, 3, 1, 2)
        gates_row = g8.transpose(0, 3, 2, 1).reshape(batch, ml_heads, 2, nc, ML_CHUNK)
        om = _mlstm(pm, gates_col, gates_row, conv_w[l], conv_b[l], b_igate[l], b_fgate[l], ml_norm[l],
                    batch, seq, ml_heads)

        x1, h2 = _outproj(oa, om, w_out[l].astype(BF16), xt, mod3, g_ffn[l], seq)

        tabs = _peer_sel(h2, w_pq[l].astype(BF16), sub_keys[l].astype(BF16))
        assert depth == 1
        xt = _peer_ffn(h2, peer_u[l].astype(BF16), peer_v[l].T.astype(BF16), tabs, x1, mod3, modo3,
                       g_final, seq)

    return xt.reshape(batch, seq, d)
```

```python
import functools
import math

import numpy as np
import jax
import jax.numpy as jnp
from jax import lax
from jax.experimental import pallas as pl
from jax.experimental.pallas import tpu as pltpu

F32 = jnp.float32
BF16 = jnp.bfloat16

DA_QK = 64
DA_V = 128
ML_QK = 128
ML_V = 256
CONV_W = 4
GATE_CAP = 15.0
ROPE_THETA = 10000.0
PEER_HEADS = 8
PEER_NKEYS = 128
PEER_TOPK = 16
EPS = 1e-6

LANES = 128
SUBLANES = 8
VMEM_LIMIT = 56 * 1024 * 1024
FFN_VMEM_LIMIT = 60 * 1024 * 1024

ADA_TN = 1024
PROJ_TM = 1024
PROJ_TN = 512
ATT_TQ = 512
ATT_TK = 512
ATT_ONES_ROWS = 16
ML_CHUNK = 256
OUT_TM = 256
SEL_TM = 512
FFN_TB = 512
FFN_EC = 512
FFN_PIECE = 512

NEG_INF = float("-inf")


def _cparams(sem):
    return pltpu.CompilerParams(dimension_semantics=sem, vmem_limit_bytes=VMEM_LIMIT)


def _rms(x, g):
    return x * lax.rsqrt(jnp.mean(x * x, axis=-1, keepdims=True) + EPS) * g


def _ada_kernel(c_ref, w_ref, b_ref, o_ref):
    c = c_ref[...]
    cs = c * jax.nn.sigmoid(c)
    o_ref[...] = jnp.dot(cs, w_ref[...], preferred_element_type=F32,
                         precision=lax.Precision.HIGHEST) + b_ref[...]


def _ada(c8, w, b):
    d, n = w.shape
    return pl.pallas_call(
        _ada_kernel,
        grid=(n // ADA_TN,),
        in_specs=[pl.BlockSpec((SUBLANES, d), lambda j: (0, 0)),
                  pl.BlockSpec((d, ADA_TN), lambda j: (0, j)),
                  pl.BlockSpec((1, ADA_TN), lambda j: (0, j))],
        out_specs=pl.BlockSpec((SUBLANES, ADA_TN), lambda j: (0, j)),
        out_shape=jax.ShapeDtypeStruct((SUBLANES, n), F32),
        compiler_params=_cparams(("arbitrary",)),
        name="ada",
    )(c8, w, b.reshape(1, n))


def _proj_kernel(n_rope, n_att, x_ref, mod_ref, g_ref, w_ref, wg_ref, cos_ref, sa_ref, sb_ref,
                 pa_ref, pm_ref, gate_ref, h_scr):
    j = pl.program_id(1)

    @pl.when(j == 0)
    def _():
        y = _rms(x_ref[...], g_ref[...])
        h = y * (1.0 + mod_ref[0, 1:2, :]) + mod_ref[0, 0:1, :]
        hb = h.astype(BF16)
        h_scr[...] = hb
        gate_ref[...] = jnp.dot(hb, wg_ref[...], preferred_element_type=F32)

    acc = jnp.dot(h_scr[...], w_ref[...], preferred_element_type=F32)

    @pl.when(j < n_rope)
    def _():
        cos, sa, sb = cos_ref[...], sa_ref[...], sb_ref[...]
        for g in range(PROJ_TN // LANES):
            xg = acc[:, g * LANES:(g + 1) * LANES]
            r = (xg * cos + pltpu.roll(xg, LANES - DA_QK // 2, 1) * sa
                 + pltpu.roll(xg, DA_QK // 2, 1) * sb)
            pa_ref[:, g * LANES:(g + 1) * LANES] = r.astype(BF16)

    @pl.when(jnp.logical_and(j >= n_rope, j < n_att))
    def _():
        pa_ref[...] = acc.astype(BF16)

    @pl.when(j >= n_att)
    def _():
        pm_ref[...] = acc


def _rope_tables(seq):
    half = DA_QK // 2
    inv = ROPE_THETA ** (-jnp.arange(half, dtype=F32) / half)
    ang = jnp.arange(seq, dtype=F32)[:, None] * inv[None, :]
    lane = np.arange(LANES)
    first = (lane % DA_QK) < half
    cos = jnp.cos(ang)[:, lane % half]
    sin = jnp.sin(ang)[:, lane % half]
    sa = jnp.where(first[None, :], -sin, 0.0)
    sb = jnp.where(first[None, :], 0.0, sin)
    return cos, sa, sb


def _proj(x2, mod3, g, w_main, w_gate, seq, n_att_cols, n_rope_cols):
    t, d = x2.shape
    n = w_main.shape[1]
    n_att = n_att_cols // PROJ_TN
    n_rope = n_rope_cols // PROJ_TN
    n_tiles = n // PROJ_TN
    tiles_per_seq = seq // PROJ_TM
    cos, sa, sb = _rope_tables(seq)
    tab_spec = pl.BlockSpec((PROJ_TM, LANES), lambda i, j: (i % tiles_per_seq, 0))
    return pl.pallas_call(
        functools.partial(_proj_kernel, n_rope, n_att),
        grid=(t // PROJ_TM, n_tiles),
        in_specs=[pl.BlockSpec((PROJ_TM, d), lambda i, j: (i, 0)),
                  pl.BlockSpec((1, 6, d), lambda i, j: (i // tiles_per_seq, 0, 0)),
                  pl.BlockSpec((1, d), lambda i, j: (0, 0)),
                  pl.BlockSpec((d, PROJ_TN), lambda i, j: (0, j)),
                  pl.BlockSpec((d, LANES), lambda i, j: (0, 0)),
                  tab_spec, tab_spec, tab_spec],
        out_specs=[pl.BlockSpec((PROJ_TM, PROJ_TN), lambda i, j: (i, jnp.minimum(j, n_att - 1))),
                   pl.BlockSpec((PROJ_TM, PROJ_TN), lambda i, j: (i, jnp.maximum(j - n_att, 0))),
                   pl.BlockSpec((PROJ_TM, LANES), lambda i, j: (i, 0))],
        out_shape=[jax.ShapeDtypeStruct((t, n_att_cols), BF16),
                   jax.ShapeDtypeStruct((t, n - n_att_cols), F32),
                   jax.ShapeDtypeStruct((t, LANES), F32)],
        scratch_shapes=[pltpu.VMEM((PROJ_TM, d), BF16)],
        compiler_params=_cparams(("arbitrary", "arbitrary")),
        name="proj",
    )(x2, mod3, g.reshape(1, d), w_main, w_gate, cos, sa, sb)


def _attn_kernel(lam_init, lam_ref, q_ref, k_ref, v_ref, nrm_ref, o_ref, vt_scr, m_scr, acc_scr):
    i = pl.program_id(2)
    tq = ATT_TQ
    tk = ATT_TK
    nkv = vt_scr.shape[0]

    @pl.when(i == 0)
    def _():
        for jb in range(nkv):
            vt_scr[jb, :DA_V, :] = v_ref[jb * tk:(jb + 1) * tk, :].astype(F32).T.astype(BF16)
            vt_scr[jb, DA_V:, :] = jnp.ones((ATT_ONES_ROWS, tk), BF16)

    lv = lam_ref[...]
    lam = (jnp.exp(jnp.sum(lv[0:1] * lv[1:2], axis=-1, keepdims=True))
           - jnp.exp(jnp.sum(lv[2:3] * lv[3:4], axis=-1, keepdims=True)) + lam_init)

    q = q_ref[...] * (DA_QK ** -0.5)
    lane = lax.broadcasted_iota(jnp.int32, (tq, LANES), 1)
    zero = jnp.zeros_like(q)
    q2 = jnp.concatenate([jnp.where(lane < DA_QK, q, zero), jnp.where(lane >= DA_QK, q, zero)], axis=0)

    m_scr[...] = jnp.full(m_scr.shape, NEG_INF, F32)
    acc_scr[...] = jnp.zeros(acc_scr.shape, F32)

    def step(jb, diagonal):
        start = pl.multiple_of(jb * tk, tk)
        k = k_ref[pl.ds(start, tk), :]
        s = lax.dot_general(k, q2, (((1,), (1,)), ((), ())), preferred_element_type=F32)
        if diagonal:
            kpos = lax.broadcasted_iota(jnp.int32, s.shape, 0) + jb * tk
            col = lax.broadcasted_iota(jnp.int32, s.shape, 1)
            qpos = jnp.where(col >= tq, col - tq, col) + i * tq
            s = jnp.where(kpos <= qpos, s, NEG_INF)
        m_old = m_scr[...]
        m_new = jnp.maximum(m_old, jnp.max(s, axis=0, keepdims=True))
        p = jnp.exp(s - m_new).astype(BF16)
        acc_scr[...] = jnp.exp(m_old - m_new) * acc_scr[...] + jnp.dot(
            vt_scr[jb], p, preferred_element_type=F32)
        m_scr[...] = m_new

    def body(jb, carry):
        step(jb, False)
        return carry

    last = (i * tq) // tk
    lax.fori_loop(0, last, body, 0)
    step(last, True)

    o2 = acc_scr[:DA_V, :] / acc_scr[DA_V:DA_V + 1, :]
    o = (o2[:, :tq] - lam * o2[:, tq:]).T
    o_ref[...] = (_rms(o, nrm_ref[...]) * (1.0 - lam_init)).astype(o_ref.dtype)


def _attn(pa, lam4, da_norm, batch, seq, heads, lam_init):
    t = pa.shape[0]
    nq = seq // ATT_TQ
    return pl.pallas_call(
        functools.partial(_attn_kernel, lam_init),
        grid=(batch, heads, nq),
        in_specs=[pl.BlockSpec((4, DA_QK), lambda b, h, i: (0, 0)),
                  pl.BlockSpec((ATT_TQ, LANES), lambda b, h, i: (b * nq + i, h)),
                  pl.BlockSpec((seq, LANES), lambda b, h, i: (b, heads + h)),
                  pl.BlockSpec((seq, LANES), lambda b, h, i: (b, 2 * heads + h)),
                  pl.BlockSpec((1, DA_V), lambda b, h, i: (0, 0))],
        out_specs=pl.BlockSpec((ATT_TQ, DA_V), lambda b, h, i: (b * nq + i, h)),
        out_shape=jax.ShapeDtypeStruct((t, heads * DA_V), BF16),
        scratch_shapes=[pltpu.VMEM((seq // ATT_TK, DA_V + ATT_ONES_ROWS, ATT_TK), BF16),
                        pltpu.VMEM((1, 2 * ATT_TQ), F32),
                        pltpu.VMEM((DA_V + ATT_ONES_ROWS, 2 * ATT_TQ), F32)],
        compiler_params=_cparams(("arbitrary", "arbitrary", "arbitrary")),
        name="attn",
    )(lam4, pa, pa, pa, da_norm.reshape(1, DA_V))


def _softcap(x):
    return GATE_CAP * jnp.tanh(x / GATE_CAP)


def _dwconv_silu(x, w, b):
    seq = x.shape[0]
    row = lax.broadcasted_iota(jnp.int32, x.shape, 0)
    y = x * w[CONV_W - 1:CONV_W]
    for s in range(1, CONV_W):
        xs = jnp.where(row >= s, pltpu.roll(x, s, 0), 0.0)
        y = y + xs * w[CONV_W - 1 - s:CONV_W - s]
    y = y + b
    return y * jax.nn.sigmoid(y)


def _mlstm_kernel(bi_ref, bf_ref, q_ref, k_ref, v_ref, og_ref, cwq_ref, cwk_ref, cbq_ref, cbk_ref,
                  gc_ref, gr_ref, nrm_ref, o_ref,
                  q_scr, k_scr, lic_scr, lfc_scr, lir_scr, lfr_scr, c_scr, m_scr):
    hd = pl.program_id(1)
    L = ML_CHUNK
    seq = q_ref.shape[0]
    nc = seq // L
    bi = bi_ref[hd]
    bf = bf_ref[hd]

    q_scr[...] = _dwconv_silu(q_ref[...], cwq_ref[...], cbq_ref[...]).astype(BF16)
    k_scr[...] = _dwconv_silu(k_ref[...], cwk_ref[...], cbk_ref[...]) * (ML_QK ** -0.5)

    gc = gc_ref[...]
    glane = lax.broadcasted_iota(jnp.int32, gc.shape, 1)
    gi = jnp.sum(jnp.where(glane == hd, gc, 0.0), axis=1, keepdims=True)
    gf = jnp.sum(jnp.where(glane == pl.num_programs(1) + hd, gc, 0.0), axis=1, keepdims=True)
    lic_scr[...] = _softcap(gi + bi)
    lfc_scr[...] = jax.nn.log_sigmoid(_softcap(gf + bf))
    lir_scr[...] = _softcap(gr_ref[0, 0, 0] + bi)
    lfr_scr[...] = jax.nn.log_sigmoid(_softcap(gr_ref[0, 0, 1] + bf))

    c_scr[...] = jnp.zeros(c_scr.shape, F32)
    m_scr[...] = jnp.zeros(m_scr.shape, F32)

    r_i = lax.broadcasted_iota(jnp.int32, (L, L), 0)
    c_i = lax.broadcasted_iota(jnp.int32, (L, L), 1)
    tril = (c_i <= r_i)
    tril_f = tril.astype(F32)
    triu_f = (r_i <= c_i).astype(F32)
    ones_col = (lax.broadcasted_iota(jnp.int32, (L, LANES), 1) == 0).astype(BF16)
    nrm = nrm_ref[...]
    hp = lax.Precision.HIGHEST

    def body(c, carry):
        start = pl.multiple_of(c * L, L)
        sl = pl.ds(start, L)
        qc = q_scr[sl, :]
        kc = k_scr[sl, :]
        vc = v_ref[sl, :].astype(BF16)
        lfc = lfc_scr[sl, :]
        lic = lic_scr[sl, :]
        lfr = lfr_scr[pl.ds(c, 1), :]
        lir = lir_scr[pl.ds(c, 1), :]
        m = m_scr[...]

        b_c = jnp.dot(tril_f, jnp.broadcast_to(lfc, (L, L)), preferred_element_type=F32, precision=hp)
        b_r = jnp.dot(jnp.broadcast_to(lfr, (L, L)), triu_f, preferred_element_type=F32, precision=hp)
        dmat = jnp.where(tril, b_c - b_r + lir, NEG_INF)
        bcol = b_c[:, 0:1]
        inter = bcol + m
        m_j = jnp.maximum(inter, jnp.max(dmat, axis=-1, keepdims=True))
        w_intra = jnp.exp(dmat - m_j)
        w_inter = jnp.exp(inter - m_j)

        sqk = lax.dot_general(qc, kc.astype(BF16), (((1,), (1,)), ((), ())), preferred_element_type=F32)
        qk = sqk * w_intra
        q_c = jnp.dot(qc, c_scr[...].astype(BF16), preferred_element_type=F32)
        num = w_inter * q_c[:, :ML_V] + jnp.dot(qk.astype(BF16), vc, preferred_element_type=F32)
        den = w_inter * q_c[:, ML_V:ML_V + 1] + jnp.sum(qk, axis=-1, keepdims=True)
        hh = num / jnp.maximum(jnp.abs(den), jnp.exp(-m_j))
        og = og_ref[sl, :]
        o_ref[sl, :] = (_rms(hh, nrm) * jax.nn.sigmoid(og)).astype(o_ref.dtype)

        b_l = b_c[L - 1:L, 0:1]
        logw = b_l - bcol + lic
        m_new = jnp.maximum(b_l + m, jnp.max(logw, axis=0, keepdims=True))
        decay = jnp.exp(b_l + m - m_new)
        ws = jnp.exp(logw - m_new)
        kw = (kc * ws).astype(BF16)
        vext = jnp.concatenate([vc, ones_col], axis=1)
        upd = lax.dot_general(kw, vext, (((0,), (0,)), ((), ())), preferred_element_type=F32)
        c_scr[...] = decay * c_scr[...] + upd
        m_scr[...] = m_new
        return carry

    lax.fori_loop(0, nc, body, 0)


def _mlstm(pm, gates_col, gates_row, conv_w, conv_b, b_ig, b_fg, ml_norm, batch, seq, heads):
    t = pm.shape[0]
    nc = seq // ML_CHUNK
    qk_w = heads * ML_QK
    vblk = 2 * qk_w // ML_V
    smem = pl.BlockSpec(memory_space=pltpu.SMEM)
    return pl.pallas_call(
        _mlstm_kernel,
        grid=(batch, heads),
        in_specs=[smem, smem,
                  pl.BlockSpec((seq, ML_QK), lambda b, h: (b, h)),
                  pl.BlockSpec((seq, ML_QK), lambda b, h: (b, heads + h)),
                  pl.BlockSpec((seq, ML_V), lambda b, h: (b, vblk + h)),
                  pl.BlockSpec((seq, ML_V), lambda b, h: (b, vblk + heads + h)),
                  pl.BlockSpec((CONV_W, ML_QK), lambda b, h: (0, h)),
                  pl.BlockSpec((CONV_W, ML_QK), lambda b, h: (0, heads + h)),
                  pl.BlockSpec((1, ML_QK), lambda b, h: (0, h)),
                  pl.BlockSpec((1, ML_QK), lambda b, h: (0, heads + h)),
                  pl.BlockSpec((seq, LANES), lambda b, h: (b, 0)),
                  pl.BlockSpec((1, 1, 2, nc, ML_CHUNK), lambda b, h: (b, h, 0, 0, 0)),
                  pl.BlockSpec((1, ML_V), lambda b, h: (0, 0))],
        out_specs=pl.BlockSpec((seq, ML_V), lambda b, h: (b, h)),
        out_shape=jax.ShapeDtypeStruct((t, heads * ML_V), BF16),
        scratch_shapes=[pltpu.VMEM((seq, ML_QK), BF16), pltpu.VMEM((seq, ML_QK), F32),
                        pltpu.VMEM((seq, 1), F32), pltpu.VMEM((seq, 1), F32),
                        pltpu.VMEM((nc, ML_CHUNK), F32), pltpu.VMEM((nc, ML_CHUNK), F32),
                        pltpu.VMEM((ML_QK, ML_V + LANES), F32), pltpu.VMEM((1, 1), F32)],
        compiler_params=_cparams(("arbitrary", "arbitrary")),
        name="mlstm",
    )(b_ig, b_fg, pm, pm, pm, pm, conv_w, conv_w, conv_b.reshape(1, -1), conv_b.reshape(1, -1),
      gates_col, gates_row, ml_norm.reshape(1, ML_V))


def _outproj_kernel(oa_ref, om_ref, w_ref, x_ref, mod_ref, g_ref, x1_ref, h2_ref):
    ka = oa_ref.shape[1]
    mixed = (jnp.dot(oa_ref[...], w_ref[:ka, :], preferred_element_type=F32)
             + jnp.dot(om_ref[...], w_ref[ka:, :], preferred_element_type=F32))
    x1 = x_ref[...] + mod_ref[0, 2:3, :] * mixed
    x1_ref[...] = x1
    h2 = _rms(x1, g_ref[...]) * (1.0 + mod_ref[0, 4:5, :]) + mod_ref[0, 3:4, :]
    h2_ref[...] = h2.astype(BF16)


def _outproj(oa, om, w_out, x2, mod3, g_ffn, seq):
    t, d = x2.shape
    tiles_per_seq = seq // OUT_TM
    return pl.pallas_call(
        _outproj_kernel,
        grid=(t // OUT_TM,),
        in_specs=[pl.BlockSpec((OUT_TM, oa.shape[1]), lambda i: (i, 0)),
                  pl.BlockSpec((OUT_TM, om.shape[1]), lambda i: (i, 0)),
                  pl.BlockSpec(w_out.shape, lambda i: (0, 0)),
                  pl.BlockSpec((OUT_TM, d), lambda i: (i, 0)),
                  pl.BlockSpec((1, 6, d), lambda i: (i // tiles_per_seq, 0, 0)),
                  pl.BlockSpec((1, d), lambda i: (0, 0))],
        out_specs=[pl.BlockSpec((OUT_TM, d), lambda i: (i, 0)),
                   pl.BlockSpec((OUT_TM, d), lambda i: (i, 0))],
        out_shape=[jax.ShapeDtypeStruct((t, d), F32), jax.ShapeDtypeStruct((t, d), BF16)],
        compiler_params=_cparams(("arbitrary",)),
        name="outproj",
    )(oa, om, w_out, x2, mod3, g_ffn.reshape(1, d))


_CAND_ROWS = 16 + 8 + 6 * 8 + 8


def _pick_max(work, iota, exact):
    mx = jnp.max(work, axis=0, keepdims=True)
    if not exact:
        return mx, work == mx
    first = jnp.min(jnp.where(work == mx, iota, float(work.shape[0])), axis=0, keepdims=True)
    return mx, iota == first


def _rank16(s, top_scr, exact):
    iota = lax.broadcasted_iota(jnp.int32, s.shape, 0).astype(F32) if exact else None
    rank = jnp.full(s.shape, float(PEER_TOPK), F32)
    for r in range(PEER_TOPK):
        mx, sel = _pick_max(s, iota, exact)
        rank = jnp.where(sel, float(r), rank)
        s = jnp.where(sel, NEG_INF, s)
        top_scr[r:r + 1, :] = mx
    return rank


def _selection_tables(s1, s2, t1_scr, t2_scr, exact):
    k = PEER_TOPK
    r1 = _rank16(s1, t1_scr, exact)
    r2 = _rank16(s2, t2_scr, exact)
    t1 = t1_scr[...]
    t2 = t2_scr[...]
    tm = s1.shape[1]

    brow = lax.broadcasted_iota(jnp.int32, (SUBLANES, tm), 0)
    pieces = [t1[0:1] + t2, t1[1:2] + t2[0:SUBLANES]]
    for a in range(2, SUBLANES):
        pieces.append(jnp.where(brow < k // (a + 1), t1[a:a + 1] + t2[0:SUBLANES], NEG_INF))
    pieces.append(t1[SUBLANES:] + t2[0:1])
    cand = jnp.concatenate(pieces, axis=0)
    cmax = cand[0:1]

    work = cand
    iota = lax.broadcasted_iota(jnp.int32, cand.shape, 0).astype(F32) if exact else None
    for _ in range(k):
        _, sel = _pick_max(work, iota, exact)
        work = jnp.where(sel, NEG_INF, work)
    taken = jnp.where(work != cand, 1.0, 0.0)
    z = jnp.sum(taken * jnp.exp(cand - cmax), axis=0, keepdims=True)

    n_rows = [jnp.sum(taken[0:16], axis=0, keepdims=True)]
    for a in range(1, SUBLANES):
        lo = 16 + (a - 1) * SUBLANES
        n_rows.append(jnp.sum(taken[lo:lo + SUBLANES], axis=0, keepdims=True))
    base = 16 + 7 * SUBLANES
    for a in range(SUBLANES, k):
        n_rows.append(taken[base + a - SUBLANES:base + a - SUBLANES + 1])
    n1 = jnp.zeros(r1.shape, F32)
    for a in range(k):
        n1 = jnp.where(r1 == float(a), n_rows[a], n1)

    extra = None
    if not exact:
        def count(x):
            return jnp.sum(x, axis=0, keepdims=True) - float(k)
        extra = (count(jnp.where(r1 < float(k), 1.0, 0.0)) + count(jnp.where(r2 < float(k), 1.0, 0.0))
                 + count(taken))
    return (r2, jnp.exp(s2 - t2[0:1]), n1, jnp.exp(s1 - t1[0:1]) / z), extra


def _peer_sel_kernel(h2_ref, w_ref, keys_ref, u_ref, v_ref, r2_ref, e2_ref, n1_ref, e1_ref, ub_ref, vtb_ref,
                     t1_scr, t2_scr):
    ub_ref[...] = u_ref[...].astype(BF16)
    vtb_ref[...] = v_ref[...].T.astype(BF16)

    half = PEER_NKEYS
    q = jnp.dot(h2_ref[...], w_ref[...], preferred_element_type=F32).astype(BF16)
    nt = (((1,), (1,)), ((), ()))
    s1 = lax.dot_general(keys_ref[0, 0], q[:, :half], nt, preferred_element_type=F32)
    s2 = lax.dot_general(keys_ref[0, 1], q[:, half:], nt, preferred_element_type=F32)

    def store(tables):
        for ref, val in zip((r2_ref, e2_ref, n1_ref, e1_ref), tables):
            ref[0] = val.astype(ref.dtype)

    tables, extra = _selection_tables(s1, s2, t1_scr, t2_scr, exact=False)
    store(tables)

    @pl.when(jnp.max(extra) > 0.0)
    def _():
        store(_selection_tables(s1, s2, t1_scr, t2_scr, exact=True)[0])


def _peer_sel(h2, w_pq, keys, peer_u, peer_v):
    t, d = h2.shape
    ne = peer_u.shape[0]
    nh = PEER_HEADS
    qd = 2 * PEER_NKEYS
    steps = (t // SEL_TM) * nh
    rows = ne // steps
    assert rows * steps == ne and rows % LANES == 0
    tab = jax.ShapeDtypeStruct((nh, PEER_NKEYS, t), F32)
    tab16 = jax.ShapeDtypeStruct((nh, PEER_NKEYS, t), BF16)
    tab_spec = pl.BlockSpec((1, PEER_NKEYS, SEL_TM), lambda i, h: (h, 0, i))
    return pl.pallas_call(
        _peer_sel_kernel,
        grid=(t // SEL_TM, nh),
        in_specs=[pl.BlockSpec((SEL_TM, d), lambda i, h: (i, 0)),
                  pl.BlockSpec((d, qd), lambda i, h: (0, h)),
                  pl.BlockSpec((1, 2, PEER_NKEYS, PEER_NKEYS), lambda i, h: (h, 0, 0, 0)),
                  pl.BlockSpec((rows, d), lambda i, h: (i * nh + h, 0)),
                  pl.BlockSpec((rows, d), lambda i, h: (i * nh + h, 0))],
        out_specs=[tab_spec, tab_spec, tab_spec, tab_spec,
                   pl.BlockSpec((rows, d), lambda i, h: (i * nh + h, 0)),
                   pl.BlockSpec((d, rows), lambda i, h: (0, i * nh + h))],
        out_shape=[tab16, tab16, tab, tab,
                   jax.ShapeDtypeStruct((ne, d), BF16), jax.ShapeDtypeStruct((d, ne), BF16)],
        scratch_shapes=[pltpu.VMEM((PEER_TOPK, SEL_TM), F32), pltpu.VMEM((PEER_TOPK, SEL_TM), F32)],
        compiler_params=_cparams(("arbitrary", "arbitrary")),
        name="peer_sel",
    )(h2, w_pq, keys, peer_u, peer_v)


def _peer_ffn_kernel(h2_ref, u0_ref, uodd_ref, unext_ref, vt_ref, r2_ref, e2_ref, n1_ref, e1_ref, x1_ref,
                     mod_ref, modo_ref, g_ref, o_ref, acc_scr, a0_scr, a1_scr):
    c = pl.program_id(1)
    nk = PEER_NKEYS
    per = FFN_EC // nk
    nt = (((1,), (1,)), ((), ()))
    zero = jnp.zeros((), BF16)

    def preact(u_ref, dst_scr, lo, hi):
        dst_scr[lo:hi, :] = lax.dot_general(u_ref[lo:hi, :], h2_ref[...], nt, preferred_element_type=F32)

    def weights(a_scr, chunk, lo, hi):
        parts = []
        for ii in range(lo // nk, hi // nk):
            i = chunk * per + ii
            a = a_scr[ii * nk:(ii + 1) * nk, :]
            gsum = None
            for hd in range(PEER_HEADS):
                n1 = n1_ref[hd, pl.ds(i, 1), :].astype(BF16)
                e1 = e1_ref[hd, pl.ds(i, 1), :].astype(BF16)
                g = jnp.where(r2_ref[hd] < n1, e2_ref[hd], zero) * e1
                gsum = g if gsum is None else gsum + g
            act = 0.5 * a * (1.0 + lax.erf(a * (2.0 ** -0.5)))
            parts.append(gsum * act.astype(BF16))
        return jnp.concatenate(parts, axis=0)

    def accumulate(w, col0, lo, hi):
        acc_scr[...] += jnp.dot(vt_ref[:, col0 + lo:col0 + hi], w, preferred_element_type=F32)

    @pl.when(c == 0)
    def _():
        acc_scr[...] = jnp.zeros(acc_scr.shape, F32)
        preact(u0_ref, a0_scr, 0, FFN_EC)

    bounds = [(p * FFN_PIECE, (p + 1) * FFN_PIECE) for p in range(FFN_EC // FFN_PIECE)]
    pieces = ([(a0_scr, 2 * c, 0, uodd_ref, a1_scr, lo, hi) for lo, hi in bounds]
              + [(a1_scr, 2 * c + 1, FFN_EC, unext_ref, a0_scr, lo, hi) for lo, hi in bounds])
    pending = None
    for src_scr, chunk, col0, u_ref, dst_scr, lo, hi in pieces:
        w = weights(src_scr, chunk, lo, hi)
        if pending is not None:
            accumulate(*pending)
        preact(u_ref, dst_scr, lo, hi)
        pending = (w, col0, lo, hi)
    accumulate(*pending)

    @pl.when(c == pl.num_programs(1) - 1)
    def _():
        y = acc_scr[...].T
        x2 = x1_ref[...] + mod_ref[0, 5:6, :] * y
        o_ref[...] = _rms(x2, g_ref[...]) * (1.0 + modo_ref[0, 1:2, :]) + modo_ref[0, 0:1, :]


def _peer_ffn(h2, u, vt, tabs, x1, mod3, modo3, g_final, seq):
    t, d = h2.shape
    ne = u.shape[0]
    tiles_per_seq = seq // FFN_TB
    tab_spec = pl.BlockSpec((PEER_HEADS, PEER_NKEYS, FFN_TB), lambda i, c: (0, 0, i))
    nchunk = ne // FFN_EC
    once = pl.Buffered(1)
    tab_once = pl.BlockSpec((PEER_HEADS, PEER_NKEYS, FFN_TB), lambda i, c: (0, 0, i), pipeline_mode=once)
    return pl.pallas_call(
        _peer_ffn_kernel,
        grid=(t // FFN_TB, nchunk // 2),
        in_specs=[pl.BlockSpec((FFN_TB, d), lambda i, c: (i, 0)),
                  pl.BlockSpec((FFN_EC, d), lambda i, c: (0, 0), pipeline_mode=once),
                  pl.BlockSpec((FFN_EC, d), lambda i, c: (2 * c + 1, 0)),
                  pl.BlockSpec((FFN_EC, d), lambda i, c: (jnp.minimum(2 * c + 2, nchunk - 1), 0)),
                  pl.BlockSpec((d, 2 * FFN_EC), lambda i, c: (0, c)),
                  tab_spec, tab_spec, tab_once, tab_once,
                  pl.BlockSpec((FFN_TB, d), lambda i, c: (i, 0), pipeline_mode=once),
                  pl.BlockSpec((1, 6, d), lambda i, c: (i // tiles_per_seq, 0, 0)),
                  pl.BlockSpec((1, 2, d), lambda i, c: (i // tiles_per_seq, 0, 0)),
                  pl.BlockSpec((1, d), lambda i, c: (0, 0))],
        out_specs=pl.BlockSpec((FFN_TB, d), lambda i, c: (i, 0)),
        out_shape=jax.ShapeDtypeStruct((t, d), F32),
        scratch_shapes=[pltpu.VMEM((d, FFN_TB), F32), pltpu.VMEM((FFN_EC, FFN_TB), F32),
                        pltpu.VMEM((FFN_EC, FFN_TB), F32)],
        compiler_params=pltpu.CompilerParams(dimension_semantics=("arbitrary", "arbitrary"),
                                             vmem_limit_bytes=FFN_VMEM_LIMIT),
        name="peer_ffn",
    )(h2, u, u, u, vt, *tabs, x1, mod3, modo3, g_final.reshape(1, d))


def kernel(x, c, w_ada, b_ada, g_mix, w_in, conv_w, conv_b, b_igate, b_fgate, lambda_q1, lambda_k1,
           lambda_q2, lambda_k2, da_norm, ml_norm, w_out, g_ffn, w_pq, sub_keys, peer_u, peer_v,
           w_ada_final, b_ada_final, g_final):
    batch, seq, d = x.shape
    depth = w_ada.shape[0]
    t = batch * seq
    da_heads = d // 256
    ml_heads = d // 512
    att_cols = 3 * da_heads * DA_V
    rope_cols = 2 * da_heads * DA_V
    main_cols = att_cols + 2 * ml_heads * ML_QK + 2 * ml_heads * ML_V
    assert batch <= SUBLANES and seq % PROJ_TM == 0 and seq % ATT_TQ == 0 and seq % ML_CHUNK == 0
    assert w_in.shape[2] == main_cols + 2 * ml_heads

    c8 = jnp.zeros((SUBLANES, d), F32).at[:batch].set(c.astype(F32))
    modo3 = _ada(c8, w_ada_final, b_ada_final)[:batch].reshape(batch, 2, d)
    xt = x.reshape(t, d)

    for l in range(depth):
        mod3 = _ada(c8, w_ada[l], b_ada[l])[:batch].reshape(batch, 6, d)
        lam_init = 0.8 - 0.6 * math.exp(-0.3 * l)

        w_main = w_in[l, :, :main_cols].astype(BF16)
        w_gate = jnp.zeros((d, LANES), BF16).at[:, :2 * ml_heads].set(w_in[l, :, main_cols:].astype(BF16))
        pa, pm, gates = _proj(xt, mod3, g_mix[l], w_main, w_gate, seq, att_cols, rope_cols)

        lam4 = jnp.stack([lambda_q1[l], lambda_k1[l], lambda_q2[l], lambda_k2[l]]).astype(F32)
        oa = _attn(pa, lam4, da_norm[l], batch, seq, da_heads, lam_init)

        nc = seq // ML_CHUNK
        g8 = gates[:, :2 * ml_heads].reshape(batch, seq, 2, ml_heads)
        gates_row = g8.transpose(0, 3, 2, 1).reshape(batch, ml_heads, 2, nc, ML_CHUNK)
        om = _mlstm(pm, gates, gates_row, conv_w[l], conv_b[l], b_igate[l], b_fgate[l], ml_norm[l],
                    batch, seq, ml_heads)

        x1, h2 = _outproj(oa, om, w_out[l].astype(BF16), xt, mod3, g_ffn[l], seq)

        *tabs, u_b, vt_b = _peer_sel(h2, w_pq[l].astype(BF16), sub_keys[l].astype(BF16), peer_u[l], peer_v[l])
        assert depth == 1
        xt = _peer_ffn(h2, u_b, vt_b, tabs, x1, mod3, modo3, g_final, seq)

    return xt.reshape(batch, seq, d)
```

```python
import functools
import math

import numpy as np
import jax
import jax.numpy as jnp
from jax import lax
from jax.experimental import pallas as pl
from jax.experimental.pallas import tpu as pltpu

F32 = jnp.float32
BF16 = jnp.bfloat16

DA_QK = 64
DA_V = 128
ML_QK = 128
ML_V = 256
CONV_W = 4
GATE_CAP = 15.0
ROPE_THETA = 10000.0
PEER_HEADS = 8
PEER_NKEYS = 128
PEER_TOPK = 16
EPS = 1e-6

LANES = 128
SUBLANES = 8
VMEM_LIMIT = 56 * 1024 * 1024
FFN_VMEM_LIMIT = 60 * 1024 * 1024

ADA_TN = 1024
PROJ_TM = 1024
PROJ_TN = 512
ATT_TQ = 512
ATT_TK = 512
ATT_ONES_ROWS = 16
ML_CHUNK = 256
OUT_TM = 256
SEL_TM = 512
FFN_TB = 512
FFN_EC = 512
FFN_PIECE = 512

NEG_INF = float("-inf")


def _cparams(sem):
    return pltpu.CompilerParams(dimension_semantics=sem, vmem_limit_bytes=VMEM_LIMIT)


def _rms(x, g):
    return x * lax.rsqrt(jnp.mean(x * x, axis=-1, keepdims=True) + EPS) * g


def _ada_kernel(c_ref, w_ref, b_ref, o_ref):
    c = c_ref[...]
    cs = c * jax.nn.sigmoid(c)
    o_ref[...] = jnp.dot(cs, w_ref[...], preferred_element_type=F32,
                         precision=lax.Precision.HIGHEST) + b_ref[...]


def _ada(c8, w, b):
    d, n = w.shape
    return pl.pallas_call(
        _ada_kernel,
        grid=(n // ADA_TN,),
        in_specs=[pl.BlockSpec((SUBLANES, d), lambda j: (0, 0)),
                  pl.BlockSpec((d, ADA_TN), lambda j: (0, j)),
                  pl.BlockSpec((1, ADA_TN), lambda j: (0, j))],
        out_specs=pl.BlockSpec((SUBLANES, ADA_TN), lambda j: (0, j)),
        out_shape=jax.ShapeDtypeStruct((SUBLANES, n), F32),
        compiler_params=_cparams(("arbitrary",)),
        name="ada",
    )(c8, w, b.reshape(1, n))


def _proj_kernel(n_rope, n_att, x_ref, mod_ref, g_ref, w_ref, wg_ref, cos_ref, sa_ref, sb_ref,
                 pa_ref, pm_ref, gate_ref, h_scr):
    j = pl.program_id(1)

    @pl.when(j == 0)
    def _():
        y = _rms(x_ref[...], g_ref[...])
        h = y * (1.0 + mod_ref[0, 1:2, :]) + mod_ref[0, 0:1, :]
        hb = h.astype(BF16)
        h_scr[...] = hb
        gate_ref[...] = jnp.dot(hb, wg_ref[...], preferred_element_type=F32)

    acc = jnp.dot(h_scr[...], w_ref[...], preferred_element_type=F32)

    @pl.when(j < n_rope)
    def _():
        cos, sa, sb = cos_ref[...], sa_ref[...], sb_ref[...]
        for g in range(PROJ_TN // LANES):
            xg = acc[:, g * LANES:(g + 1) * LANES]
            r = (xg * cos + pltpu.roll(xg, LANES - DA_QK // 2, 1) * sa
                 + pltpu.roll(xg, DA_QK // 2, 1) * sb)
            pa_ref[:, g * LANES:(g + 1) * LANES] = r.astype(BF16)

    @pl.when(jnp.logical_and(j >= n_rope, j < n_att))
    def _():
        pa_ref[...] = acc.astype(BF16)

    @pl.when(j >= n_att)
    def _():
        pm_ref[...] = acc


def _rope_tables(seq):
    half = DA_QK // 2
    inv = ROPE_THETA ** (-jnp.arange(half, dtype=F32) / half)
    ang = jnp.arange(seq, dtype=F32)[:, None] * inv[None, :]
    lane = np.arange(LANES)
    first = (lane % DA_QK) < half
    cos = jnp.cos(ang)[:, lane % half]
    sin = jnp.sin(ang)[:, lane % half]
    sa = jnp.where(first[None, :], -sin, 0.0)
    sb = jnp.where(first[None, :], 0.0, sin)
    return cos, sa, sb


def _proj(x2, mod3, g, w_main, w_gate, seq, n_att_cols, n_rope_cols):
    t, d = x2.shape
    n = w_main.shape[1]
    n_att = n_att_cols // PROJ_TN
    n_rope = n_rope_cols // PROJ_TN
    n_tiles = n // PROJ_TN
    tiles_per_seq = seq // PROJ_TM
    cos, sa, sb = _rope_tables(seq)
    tab_spec = pl.BlockSpec((PROJ_TM, LANES), lambda i, j: (i % tiles_per_seq, 0))
    return pl.pallas_call(
        functools.partial(_proj_kernel, n_rope, n_att),
        grid=(t // PROJ_TM, n_tiles),
        in_specs=[pl.BlockSpec((PROJ_TM, d), lambda i, j: (i, 0)),
                  pl.BlockSpec((1, 6, d), lambda i, j: (i // tiles_per_seq, 0, 0)),
                  pl.BlockSpec((1, d), lambda i, j: (0, 0)),
                  pl.BlockSpec((d, PROJ_TN), lambda i, j: (0, j)),
                  pl.BlockSpec((d, LANES), lambda i, j: (0, 0)),
                  tab_spec, tab_spec, tab_spec],
        out_specs=[pl.BlockSpec((PROJ_TM, PROJ_TN), lambda i, j: (i, jnp.minimum(j, n_att - 1))),
                   pl.BlockSpec((PROJ_TM, PROJ_TN), lambda i, j: (i, jnp.maximum(j - n_att, 0))),
                   pl.BlockSpec((PROJ_TM, LANES), lambda i, j: (i, 0))],
        out_shape=[jax.ShapeDtypeStruct((t, n_att_cols), BF16),
                   jax.ShapeDtypeStruct((t, n - n_att_cols), F32),
                   jax.ShapeDtypeStruct((t, LANES), F32)],
        scratch_shapes=[pltpu.VMEM((PROJ_TM, d), BF16)],
        compiler_params=_cparams(("arbitrary", "arbitrary")),
        name="proj",
    )(x2, mod3, g.reshape(1, d), w_main, w_gate, cos, sa, sb)


def _attn_kernel(lam_init, lam_ref, q_ref, k_ref, v_ref, nrm_ref, o_ref, vt_scr, m_scr, acc_scr):
    i = pl.program_id(2)
    tq = ATT_TQ
    tk = ATT_TK
    nkv = vt_scr.shape[0]

    @pl.when(i == 0)
    def _():
        for jb in range(nkv):
            vt_scr[jb, :DA_V, :] = v_ref[jb * tk:(jb + 1) * tk, :].astype(F32).T.astype(BF16)
            vt_scr[jb, DA_V:, :] = jnp.ones((ATT_ONES_ROWS, tk), BF16)

    lv = lam_ref[...]
    lam = (jnp.exp(jnp.sum(lv[0:1] * lv[1:2], axis=-1, keepdims=True))
           - jnp.exp(jnp.sum(lv[2:3] * lv[3:4], axis=-1, keepdims=True)) + lam_init)

    q = q_ref[...] * (DA_QK ** -0.5)
    lane = lax.broadcasted_iota(jnp.int32, (tq, LANES), 1)
    zero = jnp.zeros_like(q)
    q2 = jnp.concatenate([jnp.where(lane < DA_QK, q, zero), jnp.where(lane >= DA_QK, q, zero)], axis=0)

    m_scr[...] = jnp.full(m_scr.shape, NEG_INF, F32)
    acc_scr[...] = jnp.zeros(acc_scr.shape, F32)

    def step(jb, diagonal):
        start = pl.multiple_of(jb * tk, tk)
        k = k_ref[pl.ds(start, tk), :]
        s = lax.dot_general(k, q2, (((1,), (1,)), ((), ())), preferred_element_type=F32)
        if diagonal:
            kpos = lax.broadcasted_iota(jnp.int32, s.shape, 0) + jb * tk
            col = lax.broadcasted_iota(jnp.int32, s.shape, 1)
            qpos = jnp.where(col >= tq, col - tq, col) + i * tq
            s = jnp.where(kpos <= qpos, s, NEG_INF)
        m_old = m_scr[...]
        m_new = jnp.maximum(m_old, jnp.max(s, axis=0, keepdims=True))
        p = jnp.exp(s - m_new).astype(BF16)
        acc_scr[...] = jnp.exp(m_old - m_new) * acc_scr[...] + jnp.dot(
            vt_scr[jb], p, preferred_element_type=F32)
        m_scr[...] = m_new

    def body(jb, carry):
        step(jb, False)
        return carry

    last = (i * tq) // tk
    lax.fori_loop(0, last, body, 0)
    step(last, True)

    o2 = acc_scr[:DA_V, :] / acc_scr[DA_V:DA_V + 1, :]
    o = (o2[:, :tq] - lam * o2[:, tq:]).T
    o_ref[...] = (_rms(o, nrm_ref[...]) * (1.0 - lam_init)).astype(o_ref.dtype)


def _attn(pa, lam4, da_norm, batch, seq, heads, lam_init):
    t = pa.shape[0]
    nq = seq // ATT_TQ
    return pl.pallas_call(
        functools.partial(_attn_kernel, lam_init),
        grid=(batch, heads, nq),
        in_specs=[pl.BlockSpec((4, DA_QK), lambda b, h, i: (0, 0)),
                  pl.BlockSpec((ATT_TQ, LANES), lambda b, h, i: (b * nq + i, h)),
                  pl.BlockSpec((seq, LANES), lambda b, h, i: (b, heads + h)),
                  pl.BlockSpec((seq, LANES), lambda b, h, i: (b, 2 * heads + h)),
                  pl.BlockSpec((1, DA_V), lambda b, h, i: (0, 0))],
        out_specs=pl.BlockSpec((ATT_TQ, DA_V), lambda b, h, i: (b * nq + i, h)),
        out_shape=jax.ShapeDtypeStruct((t, heads * DA_V), BF16),
        scratch_shapes=[pltpu.VMEM((seq // ATT_TK, DA_V + ATT_ONES_ROWS, ATT_TK), BF16),
                        pltpu.VMEM((1, 2 * ATT_TQ), F32),
                        pltpu.VMEM((DA_V + ATT_ONES_ROWS, 2 * ATT_TQ), F32)],
        compiler_params=_cparams(("arbitrary", "arbitrary", "arbitrary")),
        name="attn",
    )(lam4, pa, pa, pa, da_norm.reshape(1, DA_V))


def _softcap(x):
    return GATE_CAP * jnp.tanh(x / GATE_CAP)


def _dwconv_silu(x, w, b):
    seq = x.shape[0]
    row = lax.broadcasted_iota(jnp.int32, x.shape, 0)
    y = x * w[CONV_W - 1:CONV_W]
    for s in range(1, CONV_W):
        xs = jnp.where(row >= s, pltpu.roll(x, s, 0), 0.0)
        y = y + xs * w[CONV_W - 1 - s:CONV_W - s]
    y = y + b
    return y * jax.nn.sigmoid(y)


def _mlstm_kernel(bi_ref, bf_ref, q_ref, k_ref, v_ref, og_ref, cwq_ref, cwk_ref, cbq_ref, cbk_ref,
                  gc_ref, nrm_ref, o_ref,
                  q_scr, k_scr, lic_scr, lfc_scr, lir_scr, lfr_scr, gt_scr, bc_scr, br_scr, c_scr, m_scr):
    hd = pl.program_id(1)
    L = ML_CHUNK
    seq = q_ref.shape[0]
    nc = seq // L
    bi = bi_ref[hd]
    bf = bf_ref[hd]

    q_scr[...] = _dwconv_silu(q_ref[...], cwq_ref[...], cbq_ref[...]).astype(BF16)
    k_scr[...] = _dwconv_silu(k_ref[...], cwk_ref[...], cbk_ref[...]) * (ML_QK ** -0.5)

    gc = gc_ref[...]
    glane = lax.broadcasted_iota(jnp.int32, gc.shape, 1)
    gi = jnp.sum(jnp.where(glane == hd, gc, 0.0), axis=1, keepdims=True)
    gf = jnp.sum(jnp.where(glane == pl.num_programs(1) + hd, gc, 0.0), axis=1, keepdims=True)
    lic_scr[...] = _softcap(gi + bi)
    lfc_scr[...] = jax.nn.log_sigmoid(_softcap(gf + bf))
    gt_scr[...] = gc.T
    li_row = _softcap(gt_scr[pl.ds(hd, 1), :] + bi)
    lf_row = jax.nn.log_sigmoid(_softcap(gt_scr[pl.ds(pl.num_programs(1) + hd, 1), :] + bf))
    for c in range(nc):
        lir_scr[c:c + 1, :] = li_row[:, c * L:(c + 1) * L]
        lfr_scr[c:c + 1, :] = lf_row[:, c * L:(c + 1) * L]

    c_scr[...] = jnp.zeros(c_scr.shape, F32)
    m_scr[...] = jnp.zeros(m_scr.shape, F32)

    r_i = lax.broadcasted_iota(jnp.int32, (L, L), 0)
    c_i = lax.broadcasted_iota(jnp.int32, (L, L), 1)
    tril = (c_i <= r_i)
    tril_b = tril.astype(BF16)
    triu_b = (r_i <= c_i).astype(BF16)
    ones_col = (lax.broadcasted_iota(jnp.int32, (L, LANES), 1) == 0).astype(BF16)
    nrm = nrm_ref[...]

    def split3(x):
        hi = x.astype(BF16)
        r = x - hi.astype(F32)
        mid = r.astype(BF16)
        return hi, mid, (r - mid.astype(F32)).astype(BF16)

    for c in range(nc):
        b_c = None
        for piece in split3(lfc_scr[c * L:(c + 1) * L, :]):
            term = jnp.dot(tril_b, jnp.broadcast_to(piece, (L, L)), preferred_element_type=F32)
            b_c = term if b_c is None else b_c + term
        b_r = None
        for piece in split3(lfr_scr[c:c + 1, :]):
            term = jnp.dot(jnp.broadcast_to(piece, (L, L)), triu_b, preferred_element_type=F32)
            b_r = term if b_r is None else b_r + term
        bc_scr[c] = b_c
        br_scr[c] = b_r

    def body(c, carry):
        start = pl.multiple_of(c * L, L)
        sl = pl.ds(start, L)
        qc = q_scr[sl, :]
        kc = k_scr[sl, :]
        vc = v_ref[sl, :].astype(BF16)
        lic = lic_scr[sl, :]
        lir = lir_scr[pl.ds(c, 1), :]
        m = m_scr[...]

        b_c = bc_scr[c]
        b_r = br_scr[c]
        dmat = jnp.where(tril, b_c - b_r + lir, NEG_INF)
        bcol = b_c[:, 0:1]
        inter = bcol + m
        m_j = jnp.maximum(inter, jnp.max(dmat, axis=-1, keepdims=True))
        w_intra = jnp.exp(dmat - m_j)
        w_inter = jnp.exp(inter - m_j)

        sqk = lax.dot_general(qc, kc.astype(BF16), (((1,), (1,)), ((), ())), preferred_element_type=F32)
        qk = sqk * w_intra
        q_c = jnp.dot(qc, c_scr[...].astype(BF16), preferred_element_type=F32)
        num = w_inter * q_c[:, :ML_V] + jnp.dot(qk.astype(BF16), vc, preferred_element_type=F32)
        den = w_inter * q_c[:, ML_V:ML_V + 1] + jnp.sum(qk, axis=-1, keepdims=True)
        hh = num / jnp.maximum(jnp.abs(den), jnp.exp(-m_j))
        og = og_ref[sl, :]
        o_ref[sl, :] = (_rms(hh, nrm) * jax.nn.sigmoid(og)).astype(o_ref.dtype)

        b_l = b_c[L - 1:L, 0:1]
        logw = b_l - bcol + lic
        m_new = jnp.maximum(b_l + m, jnp.max(logw, axis=0, keepdims=True))
        decay = jnp.exp(b_l + m - m_new)
        ws = jnp.exp(logw - m_new)
        kw = (kc * ws).astype(BF16)
        vext = jnp.concatenate([vc, ones_col], axis=1)
        upd = lax.dot_general(kw, vext, (((0,), (0,)), ((), ())), preferred_element_type=F32)
        c_scr[...] = decay * c_scr[...] + upd
        m_scr[...] = m_new
        return carry

    lax.fori_loop(0, nc, body, 0)


def _mlstm(pm, gates, conv_w, conv_b, b_ig, b_fg, ml_norm, batch, seq, heads):
    t = pm.shape[0]
    nc = seq // ML_CHUNK
    qk_w = heads * ML_QK
    vblk = 2 * qk_w // ML_V
    smem = pl.BlockSpec(memory_space=pltpu.SMEM)
    return pl.pallas_call(
        _mlstm_kernel,
        grid=(batch, heads),
        in_specs=[smem, smem,
                  pl.BlockSpec((seq, ML_QK), lambda b, h: (b, h)),
                  pl.BlockSpec((seq, ML_QK), lambda b, h: (b, heads + h)),
                  pl.BlockSpec((seq, ML_V), lambda b, h: (b, vblk + h)),
                  pl.BlockSpec((seq, ML_V), lambda b, h: (b, vblk + heads + h)),
                  pl.BlockSpec((CONV_W, ML_QK), lambda b, h: (0, h)),
                  pl.BlockSpec((CONV_W, ML_QK), lambda b, h: (0, heads + h)),
                  pl.BlockSpec((1, ML_QK), lambda b, h: (0, h)),
                  pl.BlockSpec((1, ML_QK), lambda b, h: (0, heads + h)),
                  pl.BlockSpec((seq, LANES), lambda b, h: (b, 0)),
                  pl.BlockSpec((1, ML_V), lambda b, h: (0, 0))],
        out_specs=pl.BlockSpec((seq, ML_V), lambda b, h: (b, h)),
        out_shape=jax.ShapeDtypeStruct((t, heads * ML_V), BF16),
        scratch_shapes=[pltpu.VMEM((seq, ML_QK), BF16), pltpu.VMEM((seq, ML_QK), F32),
                        pltpu.VMEM((seq, 1), F32), pltpu.VMEM((seq, 1), F32),
                        pltpu.VMEM((nc, ML_CHUNK), F32), pltpu.VMEM((nc, ML_CHUNK), F32),
                        pltpu.VMEM((LANES, seq), F32),
                        pltpu.VMEM((nc, ML_CHUNK, ML_CHUNK), F32), pltpu.VMEM((nc, ML_CHUNK, ML_CHUNK), F32),
                        pltpu.VMEM((ML_QK, ML_V + LANES), F32), pltpu.VMEM((1, 1), F32)],
        compiler_params=_cparams(("arbitrary", "arbitrary")),
        name="mlstm",
    )(b_ig, b_fg, pm, pm, pm, pm, conv_w, conv_w, conv_b.reshape(1, -1), conv_b.reshape(1, -1),
      gates, ml_norm.reshape(1, ML_V))


def _outproj_kernel(oa_ref, om_ref, w_ref, x_ref, mod_ref, g_ref, x1_ref, h2_ref):
    ka = oa_ref.shape[1]
    mixed = (jnp.dot(oa_ref[...], w_ref[:ka, :], preferred_element_type=F32)
             + jnp.dot(om_ref[...], w_ref[ka:, :], preferred_element_type=F32))
    x1 = x_ref[...] + mod_ref[0, 2:3, :] * mixed
    x1_ref[...] = x1
    h2 = _rms(x1, g_ref[...]) * (1.0 + mod_ref[0, 4:5, :]) + mod_ref[0, 3:4, :]
    h2_ref[...] = h2.astype(BF16)


def _outproj(oa, om, w_out, x2, mod3, g_ffn, seq):
    t, d = x2.shape
    tiles_per_seq = seq // OUT_TM
    return pl.pallas_call(
        _outproj_kernel,
        grid=(t // OUT_TM,),
        in_specs=[pl.BlockSpec((OUT_TM, oa.shape[1]), lambda i: (i, 0)),
                  pl.BlockSpec((OUT_TM, om.shape[1]), lambda i: (i, 0)),
                  pl.BlockSpec(w_out.shape, lambda i: (0, 0)),
                  pl.BlockSpec((OUT_TM, d), lambda i: (i, 0)),
                  pl.BlockSpec((1, 6, d), lambda i: (i // tiles_per_seq, 0, 0)),
                  pl.BlockSpec((1, d), lambda i: (0, 0))],
        out_specs=[pl.BlockSpec((OUT_TM, d), lambda i: (i, 0)),
                   pl.BlockSpec((OUT_TM, d), lambda i: (i, 0))],
        out_shape=[jax.ShapeDtypeStruct((t, d), F32), jax.ShapeDtypeStruct((t, d), BF16)],
        compiler_params=_cparams(("arbitrary",)),
        name="outproj",
    )(oa, om, w_out, x2, mod3, g_ffn.reshape(1, d))


_CAND_ROWS = 16 + 8 + 6 * 8 + 8


def _pick_max(work, iota, exact):
    mx = jnp.max(work, axis=0, keepdims=True)
    if not exact:
        return mx, work == mx
    first = jnp.min(jnp.where(work == mx, iota, float(work.shape[0])), axis=0, keepdims=True)
    return mx, iota == first


def _rank16(s, top_scr, exact):
    iota = lax.broadcasted_iota(jnp.int32, s.shape, 0).astype(F32) if exact else None
    rank = jnp.full(s.shape, float(PEER_TOPK), F32)
    for r in range(PEER_TOPK):
        mx, sel = _pick_max(s, iota, exact)
        rank = jnp.where(sel, float(r), rank)
        s = jnp.where(sel, NEG_INF, s)
        top_scr[r:r + 1, :] = mx
    return rank


def _selection_tables(s1, s2, t1_scr, t2_scr, exact):
    k = PEER_TOPK
    r1 = _rank16(s1, t1_scr, exact)
    r2 = _rank16(s2, t2_scr, exact)
    t1 = t1_scr[...]
    t2 = t2_scr[...]
    tm = s1.shape[1]

    brow = lax.broadcasted_iota(jnp.int32, (SUBLANES, tm), 0)
    pieces = [t1[0:1] + t2, t1[1:2] + t2[0:SUBLANES]]
    for a in range(2, SUBLANES):
        pieces.append(jnp.where(brow < k // (a + 1), t1[a:a + 1] + t2[0:SUBLANES], NEG_INF))
    pieces.append(t1[SUBLANES:] + t2[0:1])
    cand = jnp.concatenate(pieces, axis=0)
    cmax = cand[0:1]

    work = cand
    iota = lax.broadcasted_iota(jnp.int32, cand.shape, 0).astype(F32) if exact else None
    for _ in range(k):
        _, sel = _pick_max(work, iota, exact)
        work = jnp.where(sel, NEG_INF, work)
    taken = jnp.where(work != cand, 1.0, 0.0)
    z = jnp.sum(taken * jnp.exp(cand - cmax), axis=0, keepdims=True)

    n_rows = [jnp.sum(taken[0:16], axis=0, keepdims=True)]
    for a in range(1, SUBLANES):
        lo = 16 + (a - 1) * SUBLANES
        n_rows.append(jnp.sum(taken[lo:lo + SUBLANES], axis=0, keepdims=True))
    base = 16 + 7 * SUBLANES
    for a in range(SUBLANES, k):
        n_rows.append(taken[base + a - SUBLANES:base + a - SUBLANES + 1])
    n1 = jnp.zeros(r1.shape, F32)
    for a in range(k):
        n1 = jnp.where(r1 == float(a), n_rows[a], n1)

    extra = None
    if not exact:
        def count(x):
            return jnp.sum(x, axis=0, keepdims=True) - float(k)
        extra = (count(jnp.where(r1 < float(k), 1.0, 0.0)) + count(jnp.where(r2 < float(k), 1.0, 0.0))
                 + count(taken))
    return (r2, jnp.exp(s2 - t2[0:1]), n1, jnp.exp(s1 - t1[0:1]) / z), extra


def _peer_sel_kernel(h2_ref, w_ref, keys_ref, u_ref, v_ref, r2_ref, e2_ref, n1_ref, e1_ref, ub_ref, vtb_ref,
                     t1_scr, t2_scr):
    ub_ref[...] = u_ref[...].astype(BF16)
    vtb_ref[...] = v_ref[...].T.astype(BF16)

    half = PEER_NKEYS
    q = jnp.dot(h2_ref[...], w_ref[...], preferred_element_type=F32).astype(BF16)
    nt = (((1,), (1,)), ((), ()))
    s1 = lax.dot_general(keys_ref[0, 0], q[:, :half], nt, preferred_element_type=F32)
    s2 = lax.dot_general(keys_ref[0, 1], q[:, half:], nt, preferred_element_type=F32)

    def store(tables):
        for ref, val in zip((r2_ref, e2_ref, n1_ref, e1_ref), tables):
            ref[0] = val.astype(ref.dtype)

    tables, extra = _selection_tables(s1, s2, t1_scr, t2_scr, exact=False)
    store(tables)

    @pl.when(jnp.max(extra) > 0.0)
    def _():
        store(_selection_tables(s1, s2, t1_scr, t2_scr, exact=True)[0])


def _peer_sel(h2, w_pq, keys, peer_u, peer_v):
    t, d = h2.shape
    ne = peer_u.shape[0]
    nh = PEER_HEADS
    qd = 2 * PEER_NKEYS
    steps = (t // SEL_TM) * nh
    rows = ne // steps
    assert rows * steps == ne and rows % LANES == 0
    tab = jax.ShapeDtypeStruct((nh, PEER_NKEYS, t), F32)
    tab16 = jax.ShapeDtypeStruct((nh, PEER_NKEYS, t), BF16)
    tab_spec = pl.BlockSpec((1, PEER_NKEYS, SEL_TM), lambda i, h: (h, 0, i))
    return pl.pallas_call(
        _peer_sel_kernel,
        grid=(t // SEL_TM, nh),
        in_specs=[pl.BlockSpec((SEL_TM, d), lambda i, h: (i, 0)),
                  pl.BlockSpec((d, qd), lambda i, h: (0, h)),
                  pl.BlockSpec((1, 2, PEER_NKEYS, PEER_NKEYS), lambda i, h: (h, 0, 0, 0)),
                  pl.BlockSpec((rows, d), lambda i, h: (i * nh + h, 0)),
                  pl.BlockSpec((rows, d), lambda i, h: (i * nh + h, 0))],
        out_specs=[tab_spec, tab_spec, tab_spec, tab_spec,
                   pl.BlockSpec((rows, d), lambda i, h: (i * nh + h, 0)),
                   pl.BlockSpec((d, rows), lambda i, h: (0, i * nh + h))],
        out_shape=[tab16, tab16, tab, tab,
                   jax.ShapeDtypeStruct((ne, d), BF16), jax.ShapeDtypeStruct((d, ne), BF16)],
        scratch_shapes=[pltpu.VMEM((PEER_TOPK, SEL_TM), F32), pltpu.VMEM((PEER_TOPK, SEL_TM), F32)],
        compiler_params=_cparams(("arbitrary", "arbitrary")),
        name="peer_sel",
    )(h2, w_pq, keys, peer_u, peer_v)


def _peer_ffn_kernel(h2_ref, u0_ref, uodd_ref, unext_ref, vt_ref, r2_ref, e2_ref, n1_ref, e1_ref, x1_ref,
                     mod_ref, modo_ref, g_ref, o_ref, acc_scr, a0_scr, a1_scr):
    c = pl.program_id(1)
    nk = PEER_NKEYS
    per = FFN_EC // nk
    nt = (((1,), (1,)), ((), ()))
    zero = jnp.zeros((), BF16)

    def preact(u_ref, dst_scr, lo, hi):
        dst_scr[lo:hi, :] = lax.dot_general(u_ref[lo:hi, :], h2_ref[...], nt, preferred_element_type=F32)

    def weights(a_scr, chunk, lo, hi):
        parts = []
        for ii in range(lo // nk, hi // nk):
            i = chunk * per + ii
            a = a_scr[ii * nk:(ii + 1) * nk, :]
            gsum = None
            for hd in range(PEER_HEADS):
                n1 = n1_ref[hd, pl.ds(i, 1), :].astype(BF16)
                e1 = e1_ref[hd, pl.ds(i, 1), :].astype(BF16)
                g = jnp.where(r2_ref[hd] < n1, e2_ref[hd], zero) * e1
                gsum = g if gsum is None else gsum + g
            act = 0.5 * a * (1.0 + lax.erf(a * (2.0 ** -0.5)))
            parts.append(gsum * act.astype(BF16))
        return jnp.concatenate(parts, axis=0)

    def accumulate(w, col0, lo, hi):
        acc_scr[...] += jnp.dot(vt_ref[:, col0 + lo:col0 + hi], w, preferred_element_type=F32)

    @pl.when(c == 0)
    def _():
        acc_scr[...] = jnp.zeros(acc_scr.shape, F32)
        preact(u0_ref, a0_scr, 0, FFN_EC)

    bounds = [(p * FFN_PIECE, (p + 1) * FFN_PIECE) for p in range(FFN_EC // FFN_PIECE)]
    pieces = ([(a0_scr, 2 * c, 0, uodd_ref, a1_scr, lo, hi) for lo, hi in bounds]
              + [(a1_scr, 2 * c + 1, FFN_EC, unext_ref, a0_scr, lo, hi) for lo, hi in bounds])
    pending = None
    for src_scr, chunk, col0, u_ref, dst_scr, lo, hi in pieces:
        w = weights(src_scr, chunk, lo, hi)
        if pending is not None:
            accumulate(*pending)
        preact(u_ref, dst_scr, lo, hi)
        pending = (w, col0, lo, hi)
    accumulate(*pending)

    @pl.when(c == pl.num_programs(1) - 1)
    def _():
        y = acc_scr[...].T
        x2 = x1_ref[...] + mod_ref[0, 5:6, :] * y
        o_ref[...] = _rms(x2, g_ref[...]) * (1.0 + modo_ref[0, 1:2, :]) + modo_ref[0, 0:1, :]


def _peer_ffn(h2, u, vt, tabs, x1, mod3, modo3, g_final, seq):
    t, d = h2.shape
    ne = u.shape[0]
    tiles_per_seq = seq // FFN_TB
    tab_spec = pl.BlockSpec((PEER_HEADS, PEER_NKEYS, FFN_TB), lambda i, c: (0, 0, i))
    nchunk = ne // FFN_EC
    once = pl.Buffered(1)
    tab_once = pl.BlockSpec((PEER_HEADS, PEER_NKEYS, FFN_TB), lambda i, c: (0, 0, i), pipeline_mode=once)
    return pl.pallas_call(
        _peer_ffn_kernel,
        grid=(t // FFN_TB, nchunk // 2),
        in_specs=[pl.BlockSpec((FFN_TB, d), lambda i, c: (i, 0)),
                  pl.BlockSpec((FFN_EC, d), lambda i, c: (0, 0), pipeline_mode=once),
                  pl.BlockSpec((FFN_EC, d), lambda i, c: (2 * c + 1, 0)),
                  pl.BlockSpec((FFN_EC, d), lambda i, c: (jnp.minimum(2 * c + 2, nchunk - 1), 0)),
                  pl.BlockSpec((d, 2 * FFN_EC), lambda i, c: (0, c)),
                  tab_spec, tab_spec, tab_once, tab_once,
                  pl.BlockSpec((FFN_TB, d), lambda i, c: (i, 0), pipeline_mode=once),
                  pl.BlockSpec((1, 6, d), lambda i, c: (i // tiles_per_seq, 0, 0)),
                  pl.BlockSpec((1, 2, d), lambda i, c: (i // tiles_per_seq, 0, 0)),
                  pl.BlockSpec((1, d), lambda i, c: (0, 0))],
        out_specs=pl.BlockSpec((FFN_TB, d), lambda i, c: (i, 0)),
        out_shape=jax.ShapeDtypeStruct((t, d), F32),
        scratch_shapes=[pltpu.VMEM((d, FFN_TB), F32), pltpu.VMEM((FFN_EC, FFN_TB), F32),
                        pltpu.VMEM((FFN_EC, FFN_TB), F32)],
        compiler_params=pltpu.CompilerParams(dimension_semantics=("arbitrary", "arbitrary"),
                                             vmem_limit_bytes=FFN_VMEM_LIMIT),
        name="peer_ffn",
    )(h2, u, u, u, vt, *tabs, x1, mod3, modo3, g_final.reshape(1, d))


def kernel(x, c, w_ada, b_ada, g_mix, w_in, conv_w, conv_b, b_igate, b_fgate, lambda_q1, lambda_k1,
           lambda_q2, lambda_k2, da_norm, ml_norm, w_out, g_ffn, w_pq, sub_keys, peer_u, peer_v,
           w_ada_final, b_ada_final, g_final):
    batch, seq, d = x.shape
    depth = w_ada.shape[0]
    t = batch * seq
    da_heads = d // 256
    ml_heads = d // 512
    att_cols = 3 * da_heads * DA_V
    rope_cols = 2 * da_heads * DA_V
    main_cols = att_cols + 2 * ml_heads * ML_QK + 2 * ml_heads * ML_V
    assert batch <= SUBLANES and seq % PROJ_TM == 0 and seq % ATT_TQ == 0 and seq % ML_CHUNK == 0
    assert w_in.shape[2] == main_cols + 2 * ml_heads

    c8 = jnp.zeros((SUBLANES, d), F32).at[:batch].set(c.astype(F32))
    modo3 = _ada(c8, w_ada_final, b_ada_final)[:batch].reshape(batch, 2, d)
    xt = x.reshape(t, d)

    for l in range(depth):
        mod3 = _ada(c8, w_ada[l], b_ada[l])[:batch].reshape(batch, 6, d)
        lam_init = 0.8 - 0.6 * math.exp(-0.3 * l)

        w_main = w_in[l, :, :main_cols].astype(BF16)
        w_gate = jnp.zeros((d, LANES), BF16).at[:, :2 * ml_heads].set(w_in[l, :, main_cols:].astype(BF16))
        pa, pm, gates = _proj(xt, mod3, g_mix[l], w_main, w_gate, seq, att_cols, rope_cols)

        lam4 = jnp.stack([lambda_q1[l], lambda_k1[l], lambda_q2[l], lambda_k2[l]]).astype(F32)
        oa = _attn(pa, lam4, da_norm[l], batch, seq, da_heads, lam_init)

        om = _mlstm(pm, gates, conv_w[l], conv_b[l], b_igate[l], b_fgate[l], ml_norm[l], batch, seq, ml_heads)

        x1, h2 = _outproj(oa, om, w_out[l].astype(BF16), xt, mod3, g_ffn[l], seq)

        *tabs, u_b, vt_b = _peer_sel(h2, w_pq[l].astype(BF16), sub_keys[l].astype(BF16), peer_u[l], peer_v[l])
        assert depth == 1
        xt = _peer_ffn(h2, u_b, vt_b, tabs, x1, mod3, modo3, g_final, seq)

    return xt.reshape(batch, seq, d)
```

```python
import functools
import math

import numpy as np
import jax
import jax.numpy as jnp
from jax import lax
from jax.experimental import pallas as pl
from jax.experimental.pallas import tpu as pltpu

F32 = jnp.float32
BF16 = jnp.bfloat16

DA_QK = 64
DA_V = 128
ML_QK = 128
ML_V = 256
CONV_W = 4
GATE_CAP = 15.0
ROPE_THETA = 10000.0
PEER_HEADS = 8
PEER_NKEYS = 128
PEER_TOPK = 16
EPS = 1e-6

LANES = 128
SUBLANES = 8
VMEM_LIMIT = 56 * 1024 * 1024
FFN_VMEM_LIMIT = 60 * 1024 * 1024

ADA_TN = 1024
PROJ_TM = 1024
PROJ_TN = 512
ATT_TQ = 512
ATT_TK = 512
ATT_ONES_ROWS = 16
ML_CHUNK = 256
OUT_TM = 256
SEL_TM = 512
FFN_TB = 512
FFN_EC = 512
FFN_PIECE = 512

NEG_INF = float("-inf")


def _cparams(sem):
    return pltpu.CompilerParams(dimension_semantics=sem, vmem_limit_bytes=VMEM_LIMIT)


def _rms(x, g):
    return x * lax.rsqrt(jnp.mean(x * x, axis=-1, keepdims=True) + EPS) * g


def _ada_kernel(c_ref, w_ref, b_ref, o_ref):
    c = c_ref[...]
    cs = c * jax.nn.sigmoid(c)
    o_ref[...] = jnp.dot(cs, w_ref[...], preferred_element_type=F32,
                         precision=lax.Precision.HIGHEST) + b_ref[...]


def _ada(c8, w, b):
    d, n = w.shape
    return pl.pallas_call(
        _ada_kernel,
        grid=(n // ADA_TN,),
        in_specs=[pl.BlockSpec((SUBLANES, d), lambda j: (0, 0)),
                  pl.BlockSpec((d, ADA_TN), lambda j: (0, j)),
                  pl.BlockSpec((1, ADA_TN), lambda j: (0, j))],
        out_specs=pl.BlockSpec((SUBLANES, ADA_TN), lambda j: (0, j)),
        out_shape=jax.ShapeDtypeStruct((SUBLANES, n), F32),
        compiler_params=_cparams(("arbitrary",)),
        name="ada",
    )(c8, w, b.reshape(1, n))


def _proj_kernel(n_rope, n_att, x_ref, mod_ref, g_ref, w_ref, wg_ref, cos_ref, sa_ref, sb_ref,
                 pa_ref, pm_ref, gate_ref, h_scr):
    j = pl.program_id(1)
    nt = (((1,), (1,)), ((), ()))

    @pl.when(j == 0)
    def _():
        y = _rms(x_ref[...], g_ref[...])
        h = y * (1.0 + mod_ref[0, 1:2, :]) + mod_ref[0, 0:1, :]
        hb = h.astype(BF16)
        h_scr[...] = hb
        gate_ref[...] = lax.dot_general(hb, wg_ref[...], nt, preferred_element_type=F32)

    acc = lax.dot_general(h_scr[...], w_ref[...], nt, preferred_element_type=F32)

    @pl.when(j < n_rope)
    def _():
        cos, sa, sb = cos_ref[...], sa_ref[...], sb_ref[...]
        for g in range(PROJ_TN // LANES):
            xg = acc[:, g * LANES:(g + 1) * LANES]
            r = (xg * cos + pltpu.roll(xg, LANES - DA_QK // 2, 1) * sa
                 + pltpu.roll(xg, DA_QK // 2, 1) * sb)
            pa_ref[:, g * LANES:(g + 1) * LANES] = r.astype(BF16)

    @pl.when(jnp.logical_and(j >= n_rope, j < n_att))
    def _():
        pa_ref[...] = acc.astype(BF16)

    @pl.when(j >= n_att)
    def _():
        pm_ref[...] = acc


def _rope_tables(seq):
    half = DA_QK // 2
    inv = ROPE_THETA ** (-jnp.arange(half, dtype=F32) / half)
    ang = jnp.arange(seq, dtype=F32)[:, None] * inv[None, :]
    lane = np.arange(LANES)
    first = (lane % DA_QK) < half
    cos = jnp.cos(ang)[:, lane % half]
    sin = jnp.sin(ang)[:, lane % half]
    sa = jnp.where(first[None, :], -sin, 0.0)
    sb = jnp.where(first[None, :], 0.0, sin)
    return cos, sa, sb


def _proj(x2, mod3, g, w_main_t, w_gate_t, seq, n_att_cols, n_rope_cols):
    t, d = x2.shape
    n = w_main_t.shape[0]
    n_att = n_att_cols // PROJ_TN
    n_rope = n_rope_cols // PROJ_TN
    n_tiles = n // PROJ_TN
    tiles_per_seq = seq // PROJ_TM
    cos, sa, sb = _rope_tables(seq)
    tab_spec = pl.BlockSpec((PROJ_TM, LANES), lambda i, j: (i % tiles_per_seq, 0))
    return pl.pallas_call(
        functools.partial(_proj_kernel, n_rope, n_att),
        grid=(t // PROJ_TM, n_tiles),
        in_specs=[pl.BlockSpec((PROJ_TM, d), lambda i, j: (i, 0)),
                  pl.BlockSpec((1, 6, d), lambda i, j: (i // tiles_per_seq, 0, 0)),
                  pl.BlockSpec((1, d), lambda i, j: (0, 0)),
                  pl.BlockSpec((PROJ_TN, d), lambda i, j: (j, 0)),
                  pl.BlockSpec((LANES, d), lambda i, j: (0, 0)),
                  tab_spec, tab_spec, tab_spec],
        out_specs=[pl.BlockSpec((PROJ_TM, PROJ_TN), lambda i, j: (i, jnp.minimum(j, n_att - 1))),
                   pl.BlockSpec((PROJ_TM, PROJ_TN), lambda i, j: (i, jnp.maximum(j - n_att, 0))),
                   pl.BlockSpec((PROJ_TM, LANES), lambda i, j: (i, 0))],
        out_shape=[jax.ShapeDtypeStruct((t, n_att_cols), BF16),
                   jax.ShapeDtypeStruct((t, n - n_att_cols), F32),
                   jax.ShapeDtypeStruct((t, LANES), F32)],
        scratch_shapes=[pltpu.VMEM((PROJ_TM, d), BF16)],
        compiler_params=_cparams(("arbitrary", "arbitrary")),
        name="proj",
    )(x2, mod3, g.reshape(1, d), w_main_t, w_gate_t, cos, sa, sb)


def _attn_kernel(lam_init, lam_ref, q_ref, k_ref, v_ref, nrm_ref, o_ref, vt_scr, m_scr, acc_scr):
    i = pl.program_id(2)
    tq = ATT_TQ
    tk = ATT_TK
    nkv = vt_scr.shape[0]

    @pl.when(i == 0)
    def _():
        for jb in range(nkv):
            vt_scr[jb, :DA_V, :] = v_ref[jb * tk:(jb + 1) * tk, :].astype(F32).T.astype(BF16)
            vt_scr[jb, DA_V:, :] = jnp.ones((ATT_ONES_ROWS, tk), BF16)

    lv = lam_ref[...]
    lam = (jnp.exp(jnp.sum(lv[0:1] * lv[1:2], axis=-1, keepdims=True))
           - jnp.exp(jnp.sum(lv[2:3] * lv[3:4], axis=-1, keepdims=True)) + lam_init)

    q = q_ref[...] * (DA_QK ** -0.5)
    lane = lax.broadcasted_iota(jnp.int32, (tq, LANES), 1)
    zero = jnp.zeros_like(q)
    q2 = jnp.concatenate([jnp.where(lane < DA_QK, q, zero), jnp.where(lane >= DA_QK, q, zero)], axis=0)

    m_scr[...] = jnp.full(m_scr.shape, NEG_INF, F32)
    acc_scr[...] = jnp.zeros(acc_scr.shape, F32)

    def step(jb, diagonal):
        start = pl.multiple_of(jb * tk, tk)
        k = k_ref[pl.ds(start, tk), :]
        s = lax.dot_general(k, q2, (((1,), (1,)), ((), ())), preferred_element_type=F32)
        if diagonal:
            kpos = lax.broadcasted_iota(jnp.int32, s.shape, 0) + jb * tk
            col = lax.broadcasted_iota(jnp.int32, s.shape, 1)
            qpos = jnp.where(col >= tq, col - tq, col) + i * tq
            s = jnp.where(kpos <= qpos, s, NEG_INF)
        m_old = m_scr[...]
        m_new = jnp.maximum(m_old, jnp.max(s, axis=0, keepdims=True))
        p = jnp.exp(s - m_new).astype(BF16)
        acc_scr[...] = jnp.exp(m_old - m_new) * acc_scr[...] + jnp.dot(
            vt_scr[jb], p, preferred_element_type=F32)
        m_scr[...] = m_new

    def body(jb, carry):
        step(jb, False)
        return carry

    last = (i * tq) // tk
    lax.fori_loop(0, last, body, 0)
    step(last, True)

    o2 = acc_scr[:DA_V, :] / acc_scr[DA_V:DA_V + 1, :]
    o = (o2[:, :tq] - lam * o2[:, tq:]).T
    o_ref[...] = (_rms(o, nrm_ref[...]) * (1.0 - lam_init)).astype(o_ref.dtype)


def _attn(pa, lam4, da_norm, batch, seq, heads, lam_init):
    t = pa.shape[0]
    nq = seq // ATT_TQ
    return pl.pallas_call(
        functools.partial(_attn_kernel, lam_init),
        grid=(batch, heads, nq),
        in_specs=[pl.BlockSpec((4, DA_QK), lambda b, h, i: (0, 0)),
                  pl.BlockSpec((ATT_TQ, LANES), lambda b, h, i: (b * nq + i, h)),
                  pl.BlockSpec((seq, LANES), lambda b, h, i: (b, heads + h)),
                  pl.BlockSpec((seq, LANES), lambda b, h, i: (b, 2 * heads + h)),
                  pl.BlockSpec((1, DA_V), lambda b, h, i: (0, 0))],
        out_specs=pl.BlockSpec((ATT_TQ, DA_V), lambda b, h, i: (b * nq + i, h)),
        out_shape=jax.ShapeDtypeStruct((t, heads * DA_V), BF16),
        scratch_shapes=[pltpu.VMEM((seq // ATT_TK, DA_V + ATT_ONES_ROWS, ATT_TK), BF16),
                        pltpu.VMEM((1, 2 * ATT_TQ), F32),
                        pltpu.VMEM((DA_V + ATT_ONES_ROWS, 2 * ATT_TQ), F32)],
        compiler_params=_cparams(("arbitrary", "arbitrary", "arbitrary")),
        name="attn",
    )(lam4, pa, pa, pa, da_norm.reshape(1, DA_V))


def _softcap(x):
    return GATE_CAP * jnp.tanh(x / GATE_CAP)


def _dwconv_silu(x, w, b):
    seq = x.shape[0]
    row = lax.broadcasted_iota(jnp.int32, x.shape, 0)
    y = x * w[CONV_W - 1:CONV_W]
    for s in range(1, CONV_W):
        xs = jnp.where(row >= s, pltpu.roll(x, s, 0), 0.0)
        y = y + xs * w[CONV_W - 1 - s:CONV_W - s]
    y = y + b
    return y * jax.nn.sigmoid(y)


def _mlstm_kernel(bi_ref, bf_ref, q_ref, k_ref, v_ref, og_ref, cwq_ref, cwk_ref, cbq_ref, cbk_ref,
                  gc_ref, nrm_ref, o_ref,
                  q_scr, k_scr, lic_scr, lfc_scr, lir_scr, lfr_scr, gt_scr, bc_scr, br_scr, c_scr, m_scr):
    hd = pl.program_id(1)
    L = ML_CHUNK
    seq = q_ref.shape[0]
    nc = seq // L
    bi = bi_ref[hd]
    bf = bf_ref[hd]

    q_scr[...] = _dwconv_silu(q_ref[...], cwq_ref[...], cbq_ref[...]).astype(BF16)
    k_scr[...] = _dwconv_silu(k_ref[...], cwk_ref[...], cbk_ref[...]) * (ML_QK ** -0.5)

    gc = gc_ref[...]
    glane = lax.broadcasted_iota(jnp.int32, gc.shape, 1)
    gi = jnp.sum(jnp.where(glane == hd, gc, 0.0), axis=1, keepdims=True)
    gf = jnp.sum(jnp.where(glane == pl.num_programs(1) + hd, gc, 0.0), axis=1, keepdims=True)
    lic_scr[...] = _softcap(gi + bi)
    lfc_scr[...] = jax.nn.log_sigmoid(_softcap(gf + bf))
    gt_scr[...] = gc.T
    li_row = _softcap(gt_scr[pl.ds(hd, 1), :] + bi)
    lf_row = jax.nn.log_sigmoid(_softcap(gt_scr[pl.ds(pl.num_programs(1) + hd, 1), :] + bf))
    for c in range(nc):
        lir_scr[c:c + 1, :] = li_row[:, c * L:(c + 1) * L]
        lfr_scr[c:c + 1, :] = lf_row[:, c * L:(c + 1) * L]

    c_scr[...] = jnp.zeros(c_scr.shape, F32)
    m_scr[...] = jnp.zeros(m_scr.shape, F32)

    r_i = lax.broadcasted_iota(jnp.int32, (L, L), 0)
    c_i = lax.broadcasted_iota(jnp.int32, (L, L), 1)
    tril = (c_i <= r_i)
    tril_b = tril.astype(BF16)
    triu_b = (r_i <= c_i).astype(BF16)
    ones_col = (lax.broadcasted_iota(jnp.int32, (L, LANES), 1) == 0).astype(BF16)
    nrm = nrm_ref[...]

    def split3(x):
        hi = x.astype(BF16)
        r = x - hi.astype(F32)
        mid = r.astype(BF16)
        return hi, mid, (r - mid.astype(F32)).astype(BF16)

    for c in range(nc):
        b_c = None
        for piece in split3(lfc_scr[c * L:(c + 1) * L, :]):
            term = jnp.dot(tril_b, jnp.broadcast_to(piece, (L, L)), preferred_element_type=F32)
            b_c = term if b_c is None else b_c + term
        b_r = None
        for piece in split3(lfr_scr[c:c + 1, :]):
            term = jnp.dot(jnp.broadcast_to(piece, (L, L)), triu_b, preferred_element_type=F32)
            b_r = term if b_r is None else b_r + term
        bc_scr[c] = b_c
        br_scr[c] = b_r

    def body(c, carry):
        start = pl.multiple_of(c * L, L)
        sl = pl.ds(start, L)
        qc = q_scr[sl, :]
        kc = k_scr[sl, :]
        vc = v_ref[sl, :].astype(BF16)
        lic = lic_scr[sl, :]
        lir = lir_scr[pl.ds(c, 1), :]
        m = m_scr[...]

        b_c = bc_scr[c]
        b_r = br_scr[c]
        dmat = jnp.where(tril, b_c - b_r + lir, NEG_INF)
        bcol = b_c[:, 0:1]
        inter = bcol + m
        m_j = jnp.maximum(inter, jnp.max(dmat, axis=-1, keepdims=True))
        w_intra = jnp.exp(dmat - m_j)
        w_inter = jnp.exp(inter - m_j)

        sqk = lax.dot_general(qc, kc.astype(BF16), (((1,), (1,)), ((), ())), preferred_element_type=F32)
        qk = sqk * w_intra
        q_c = jnp.dot(qc, c_scr[...].astype(BF16), preferred_element_type=F32)
        num = w_inter * q_c[:, :ML_V] + jnp.dot(qk.astype(BF16), vc, preferred_element_type=F32)
        den = w_inter * q_c[:, ML_V:ML_V + 1] + jnp.sum(qk, axis=-1, keepdims=True)
        hh = num / jnp.maximum(jnp.abs(den), jnp.exp(-m_j))
        og = og_ref[sl, :]
        o_ref[sl, :] = (_rms(hh, nrm) * jax.nn.sigmoid(og)).astype(o_ref.dtype)

        b_l = b_c[L - 1:L, 0:1]
        logw = b_l - bcol + lic
        m_new = jnp.maximum(b_l + m, jnp.max(logw, axis=0, keepdims=True))
        decay = jnp.exp(b_l + m - m_new)
        ws = jnp.exp(logw - m_new)
        kw = (kc * ws).astype(BF16)
        vext = jnp.concatenate([vc, ones_col], axis=1)
        upd = lax.dot_general(kw, vext, (((0,), (0,)), ((), ())), preferred_element_type=F32)
        c_scr[...] = decay * c_scr[...] + upd
        m_scr[...] = m_new
        return carry

    lax.fori_loop(0, nc, body, 0)


def _mlstm(pm, gates, conv_w, conv_b, b_ig, b_fg, ml_norm, batch, seq, heads):
    t = pm.shape[0]
    nc = seq // ML_CHUNK
    qk_w = heads * ML_QK
    vblk = 2 * qk_w // ML_V
    smem = pl.BlockSpec(memory_space=pltpu.SMEM)
    return pl.pallas_call(
        _mlstm_kernel,
        grid=(batch, heads),
        in_specs=[smem, smem,
                  pl.BlockSpec((seq, ML_QK), lambda b, h: (b, h)),
                  pl.BlockSpec((seq, ML_QK), lambda b, h: (b, heads + h)),
                  pl.BlockSpec((seq, ML_V), lambda b, h: (b, vblk + h)),
                  pl.BlockSpec((seq, ML_V), lambda b, h: (b, vblk + heads + h)),
                  pl.BlockSpec((CONV_W, ML_QK), lambda b, h: (0, h)),
                  pl.BlockSpec((CONV_W, ML_QK), lambda b, h: (0, heads + h)),
                  pl.BlockSpec((1, ML_QK), lambda b, h: (0, h)),
                  pl.BlockSpec((1, ML_QK), lambda b, h: (0, heads + h)),
                  pl.BlockSpec((seq, LANES), lambda b, h: (b, 0)),
                  pl.BlockSpec((1, ML_V), lambda b, h: (0, 0))],
        out_specs=pl.BlockSpec((seq, ML_V), lambda b, h: (b, h)),
        out_shape=jax.ShapeDtypeStruct((t, heads * ML_V), BF16),
        scratch_shapes=[pltpu.VMEM((seq, ML_QK), BF16), pltpu.VMEM((seq, ML_QK), F32),
                        pltpu.VMEM((seq, 1), F32), pltpu.VMEM((seq, 1), F32),
                        pltpu.VMEM((nc, ML_CHUNK), F32), pltpu.VMEM((nc, ML_CHUNK), F32),
                        pltpu.VMEM((LANES, seq), F32),
                        pltpu.VMEM((nc, ML_CHUNK, ML_CHUNK), F32), pltpu.VMEM((nc, ML_CHUNK, ML_CHUNK), F32),
                        pltpu.VMEM((ML_QK, ML_V + LANES), F32), pltpu.VMEM((1, 1), F32)],
        compiler_params=_cparams(("arbitrary", "arbitrary")),
        name="mlstm",
    )(b_ig, b_fg, pm, pm, pm, pm, conv_w, conv_w, conv_b.reshape(1, -1), conv_b.reshape(1, -1),
      gates, ml_norm.reshape(1, ML_V))


def _outproj_kernel(oa_ref, om_ref, w_ref, x_ref, mod_ref, g_ref, x1_ref, h2_ref):
    ka = oa_ref.shape[1]
    mixed = (jnp.dot(oa_ref[...], w_ref[:ka, :], preferred_element_type=F32)
             + jnp.dot(om_ref[...], w_ref[ka:, :], preferred_element_type=F32))
    x1 = x_ref[...] + mod_ref[0, 2:3, :] * mixed
    x1_ref[...] = x1
    h2 = _rms(x1, g_ref[...]) * (1.0 + mod_ref[0, 4:5, :]) + mod_ref[0, 3:4, :]
    h2_ref[...] = h2.astype(BF16)


def _outproj(oa, om, w_out, x2, mod3, g_ffn, seq):
    t, d = x2.shape
    tiles_per_seq = seq // OUT_TM
    return pl.pallas_call(
        _outproj_kernel,
        grid=(t // OUT_TM,),
        in_specs=[pl.BlockSpec((OUT_TM, oa.shape[1]), lambda i: (i, 0)),
                  pl.BlockSpec((OUT_TM, om.shape[1]), lambda i: (i, 0)),
                  pl.BlockSpec(w_out.shape, lambda i: (0, 0)),
                  pl.BlockSpec((OUT_TM, d), lambda i: (i, 0)),
                  pl.BlockSpec((1, 6, d), lambda i: (i // tiles_per_seq, 0, 0)),
                  pl.BlockSpec((1, d), lambda i: (0, 0))],
        out_specs=[pl.BlockSpec((OUT_TM, d), lambda i: (i, 0)),
                   pl.BlockSpec((OUT_TM, d), lambda i: (i, 0))],
        out_shape=[jax.ShapeDtypeStruct((t, d), F32), jax.ShapeDtypeStruct((t, d), BF16)],
        compiler_params=_cparams(("arbitrary",)),
        name="outproj",
    )(oa, om, w_out, x2, mod3, g_ffn.reshape(1, d))


_CAND_ROWS = 16 + 8 + 6 * 8 + 8


def _pick_max(work, iota, exact):
    mx = jnp.max(work, axis=0, keepdims=True)
    if not exact:
        return mx, work == mx
    first = jnp.min(jnp.where(work == mx, iota, float(work.shape[0])), axis=0, keepdims=True)
    return mx, iota == first


def _rank16(s, top_scr, exact, want_rank=True):
    iota = lax.broadcasted_iota(jnp.int32, s.shape, 0).astype(F32) if exact else None
    rank = jnp.full(s.shape, float(PEER_TOPK), F32) if want_rank else None
    for r in range(PEER_TOPK):
        mx, sel = _pick_max(s, iota, exact)
        if want_rank:
            rank = jnp.where(sel, float(r), rank)
        s = jnp.where(sel, NEG_INF, s)
        top_scr[r:r + 1, :] = mx
    return rank, s


def _selection_tables(s1, s2, t1_scr, t2_scr, exact):
    k = PEER_TOPK
    r1, left1 = _rank16(s1, t1_scr, exact, want_rank=exact)
    r2, _ = _rank16(s2, t2_scr, exact)
    t1 = t1_scr[...]
    t2 = t2_scr[...]
    tm = s1.shape[1]

    brow = lax.broadcasted_iota(jnp.int32, (SUBLANES, tm), 0)
    pieces = [t1[0:1] + t2, t1[1:2] + t2[0:SUBLANES]]
    for a in range(2, SUBLANES):
        pieces.append(jnp.where(brow < k // (a + 1), t1[a:a + 1] + t2[0:SUBLANES], NEG_INF))
    pieces.append(t1[SUBLANES:] + t2[0:1])
    cand = jnp.concatenate(pieces, axis=0)
    cmax = cand[0:1]

    work = cand
    iota = lax.broadcasted_iota(jnp.int32, cand.shape, 0).astype(F32) if exact else None
    for _ in range(k):
        _, sel = _pick_max(work, iota, exact)
        work = jnp.where(sel, NEG_INF, work)
    taken = jnp.where(work != cand, 1.0, 0.0)
    z = jnp.sum(taken * jnp.exp(cand - cmax), axis=0, keepdims=True)

    n_rows = [jnp.sum(taken[0:16], axis=0, keepdims=True)]
    for a in range(1, SUBLANES):
        lo = 16 + (a - 1) * SUBLANES
        n_rows.append(jnp.sum(taken[lo:lo + SUBLANES], axis=0, keepdims=True))
    base = 16 + 7 * SUBLANES
    for a in range(SUBLANES, k):
        n_rows.append(taken[base + a - SUBLANES:base + a - SUBLANES + 1])
    n1 = jnp.zeros(s1.shape, F32)
    for a in range(k):
        hit = (r1 == float(a)) if exact else (s1 == t1[a:a + 1])
        n1 = jnp.where(hit, n_rows[a], n1)

    extra = None
    if not exact:
        def count(x):
            return jnp.sum(x, axis=0, keepdims=True) - float(k)
        extra = (count(jnp.where(left1 != s1, 1.0, 0.0)) + count(jnp.where(r2 < float(k), 1.0, 0.0))
                 + count(taken))
    return (r2, jnp.exp(s2 - t2[0:1]), n1, jnp.exp(s1 - t1[0:1]) / z), extra


def _peer_sel_kernel(h2_ref, w_ref, keys_ref, u_ref, v_ref, r2_ref, e2_ref, n1_ref, e1_ref, ub_ref, vtb_ref,
                     t1_scr, t2_scr):
    ub_ref[...] = u_ref[...].astype(BF16)
    vtb_ref[...] = v_ref[...].T.astype(BF16)

    half = PEER_NKEYS
    q = jnp.dot(h2_ref[...], w_ref[...], preferred_element_type=F32).astype(BF16)
    nt = (((1,), (1,)), ((), ()))
    s1 = lax.dot_general(keys_ref[0, 0], q[:, :half], nt, preferred_element_type=F32)
    s2 = lax.dot_general(keys_ref[0, 1], q[:, half:], nt, preferred_element_type=F32)

    def store(tables):
        for ref, val in zip((r2_ref, e2_ref, n1_ref, e1_ref), tables):
            ref[0] = val.astype(ref.dtype)

    tables, extra = _selection_tables(s1, s2, t1_scr, t2_scr, exact=False)
    store(tables)

    @pl.when(jnp.max(extra) > 0.0)
    def _():
        store(_selection_tables(s1, s2, t1_scr, t2_scr, exact=True)[0])


def _peer_sel(h2, w_pq, keys, peer_u, peer_v):
    t, d = h2.shape
    ne = peer_u.shape[0]
    nh = PEER_HEADS
    qd = 2 * PEER_NKEYS
    steps = (t // SEL_TM) * nh
    rows = ne // steps
    assert rows * steps == ne and rows % LANES == 0
    tab = jax.ShapeDtypeStruct((nh, PEER_NKEYS, t), F32)
    tab16 = jax.ShapeDtypeStruct((nh, PEER_NKEYS, t), BF16)
    tab_spec = pl.BlockSpec((1, PEER_NKEYS, SEL_TM), lambda i, h: (h, 0, i))
    return pl.pallas_call(
        _peer_sel_kernel,
        grid=(t // SEL_TM, nh),
        in_specs=[pl.BlockSpec((SEL_TM, d), lambda i, h: (i, 0)),
                  pl.BlockSpec((d, qd), lambda i, h: (0, h)),
                  pl.BlockSpec((1, 2, PEER_NKEYS, PEER_NKEYS), lambda i, h: (h, 0, 0, 0)),
                  pl.BlockSpec((rows, d), lambda i, h: (i * nh + h, 0)),
                  pl.BlockSpec((rows, d), lambda i, h: (i * nh + h, 0))],
        out_specs=[tab_spec, tab_spec, tab_spec, tab_spec,
                   pl.BlockSpec((rows, d), lambda i, h: (i * nh + h, 0)),
                   pl.BlockSpec((d, rows), lambda i, h: (0, i * nh + h))],
        out_shape=[tab16, tab16, tab, tab,
                   jax.ShapeDtypeStruct((ne, d), BF16), jax.ShapeDtypeStruct((d, ne), BF16)],
        scratch_shapes=[pltpu.VMEM((PEER_TOPK, SEL_TM), F32), pltpu.VMEM((PEER_TOPK, SEL_TM), F32)],
        compiler_params=_cparams(("arbitrary", "arbitrary")),
        name="peer_sel",
    )(h2, w_pq, keys, peer_u, peer_v)


def _peer_ffn_kernel(h2_ref, u0_ref, uodd_ref, unext_ref, vt_ref, r2_ref, e2_ref, n1_ref, e1_ref, x1_ref,
                     mod_ref, modo_ref, g_ref, o_ref, acc_scr, a0_scr, a1_scr):
    c = pl.program_id(1)
    nk = PEER_NKEYS
    per = FFN_EC // nk
    nt = (((1,), (1,)), ((), ()))
    zero = jnp.zeros((), BF16)

    def preact(u_ref, dst_scr, lo, hi):
        dst_scr[lo:hi, :] = lax.dot_general(u_ref[lo:hi, :], h2_ref[...], nt, preferred_element_type=F32)

    def weights(a_scr, chunk, lo, hi):
        parts = []
        for ii in range(lo // nk, hi // nk):
            i = chunk * per + ii
            a = a_scr[ii * nk:(ii + 1) * nk, :]
            gsum = None
            for hd in range(PEER_HEADS):
                n1 = n1_ref[hd, pl.ds(i, 1), :].astype(BF16)
                e1 = e1_ref[hd, pl.ds(i, 1), :].astype(BF16)
                g = jnp.where(r2_ref[hd] < n1, e2_ref[hd], zero) * e1
                gsum = g if gsum is None else gsum + g
            act = 0.5 * a * (1.0 + lax.erf(a * (2.0 ** -0.5)))
            parts.append(gsum * act.astype(BF16))
        return jnp.concatenate(parts, axis=0)

    def accumulate(w, col0, lo, hi):
        acc_scr[...] += jnp.dot(vt_ref[:, col0 + lo:col0 + hi], w, preferred_element_type=F32)

    @pl.when(c == 0)
    def _():
        acc_scr[...] = jnp.zeros(acc_scr.shape, F32)
        preact(u0_ref, a0_scr, 0, FFN_EC)

    bounds = [(p * FFN_PIECE, (p + 1) * FFN_PIECE) for p in range(FFN_EC // FFN_PIECE)]
    pieces = ([(a0_scr, 2 * c, 0, uodd_ref, a1_scr, lo, hi) for lo, hi in bounds]
              + [(a1_scr, 2 * c + 1, FFN_EC, unext_ref, a0_scr, lo, hi) for lo, hi in bounds])
    pending = None
    for src_scr, chunk, col0, u_ref, dst_scr, lo, hi in pieces:
        w = weights(src_scr, chunk, lo, hi)
        if pending is not None:
            accumulate(*pending)
        preact(u_ref, dst_scr, lo, hi)
        pending = (w, col0, lo, hi)
    accumulate(*pending)

    @pl.when(c == pl.num_programs(1) - 1)
    def _():
        y = acc_scr[...].T
        x2 = x1_ref[...] + mod_ref[0, 5:6, :] * y
        o_ref[...] = _rms(x2, g_ref[...]) * (1.0 + modo_ref[0, 1:2, :]) + modo_ref[0, 0:1, :]


def _peer_ffn(h2, u, vt, tabs, x1, mod3, modo3, g_final, seq):
    t, d = h2.shape
    ne = u.shape[0]
    tiles_per_seq = seq // FFN_TB
    tab_spec = pl.BlockSpec((PEER_HEADS, PEER_NKEYS, FFN_TB), lambda i, c: (0, 0, i))
    nchunk = ne // FFN_EC
    once = pl.Buffered(1)
    tab_once = pl.BlockSpec((PEER_HEADS, PEER_NKEYS, FFN_TB), lambda i, c: (0, 0, i), pipeline_mode=once)
    return pl.pallas_call(
        _peer_ffn_kernel,
        grid=(t // FFN_TB, nchunk // 2),
        in_specs=[pl.BlockSpec((FFN_TB, d), lambda i, c: (i, 0)),
                  pl.BlockSpec((FFN_EC, d), lambda i, c: (0, 0), pipeline_mode=once),
                  pl.BlockSpec((FFN_EC, d), lambda i, c: (2 * c + 1, 0)),
                  pl.BlockSpec((FFN_EC, d), lambda i, c: (jnp.minimum(2 * c + 2, nchunk - 1), 0)),
                  pl.BlockSpec((d, 2 * FFN_EC), lambda i, c: (0, c)),
                  tab_spec, tab_spec, tab_once, tab_once,
                  pl.BlockSpec((FFN_TB, d), lambda i, c: (i, 0), pipeline_mode=once),
                  pl.BlockSpec((1, 6, d), lambda i, c: (i // tiles_per_seq, 0, 0)),
                  pl.BlockSpec((1, 2, d), lambda i, c: (i // tiles_per_seq, 0, 0)),
                  pl.BlockSpec((1, d), lambda i, c: (0, 0))],
        out_specs=pl.BlockSpec((FFN_TB, d), lambda i, c: (i, 0)),
        out_shape=jax.ShapeDtypeStruct((t, d), F32),
        scratch_shapes=[pltpu.VMEM((d, FFN_TB), F32), pltpu.VMEM((FFN_EC, FFN_TB), F32),
                        pltpu.VMEM((FFN_EC, FFN_TB), F32)],
        compiler_params=pltpu.CompilerParams(dimension_semantics=("arbitrary", "arbitrary"),
                                             vmem_limit_bytes=FFN_VMEM_LIMIT),
        name="peer_ffn",
    )(h2, u, u, u, vt, *tabs, x1, mod3, modo3, g_final.reshape(1, d))


def kernel(x, c, w_ada, b_ada, g_mix, w_in, conv_w, conv_b, b_igate, b_fgate, lambda_q1, lambda_k1,
           lambda_q2, lambda_k2, da_norm, ml_norm, w_out, g_ffn, w_pq, sub_keys, peer_u, peer_v,
           w_ada_final, b_ada_final, g_final):
    batch, seq, d = x.shape
    depth = w_ada.shape[0]
    t = batch * seq
    da_heads = d // 256
    ml_heads = d // 512
    att_cols = 3 * da_heads * DA_V
    rope_cols = 2 * da_heads * DA_V
    main_cols = att_cols + 2 * ml_heads * ML_QK + 2 * ml_heads * ML_V
    assert batch <= SUBLANES and seq % PROJ_TM == 0 and seq % ATT_TQ == 0 and seq % ML_CHUNK == 0
    assert w_in.shape[2] == main_cols + 2 * ml_heads

    c8 = jnp.zeros((SUBLANES, d), F32).at[:batch].set(c.astype(F32))
    modo3 = _ada(c8, w_ada_final, b_ada_final)[:batch].reshape(batch, 2, d)
    xt = x.reshape(t, d)

    for l in range(depth):
        mod3 = _ada(c8, w_ada[l], b_ada[l])[:batch].reshape(batch, 6, d)
        lam_init = 0.8 - 0.6 * math.exp(-0.3 * l)

        w_in_t = jnp.swapaxes(w_in[l], 0, 1)
        w_main_t = w_in_t[:main_cols].astype(BF16)
        w_gate_t = jnp.zeros((LANES, d), BF16).at[:2 * ml_heads].set(w_in_t[main_cols:].astype(BF16))
        pa, pm, gates = _proj(xt, mod3, g_mix[l], w_main_t, w_gate_t, seq, att_cols, rope_cols)

        lam4 = jnp.stack([lambda_q1[l], lambda_k1[l], lambda_q2[l], lambda_k2[l]]).astype(F32)
        oa = _attn(pa, lam4, da_norm[l], batch, seq, da_heads, lam_init)

        om = _mlstm(pm, gates, conv_w[l], conv_b[l], b_igate[l], b_fgate[l], ml_norm[l], batch, seq, ml_heads)

        x1, h2 = _outproj(oa, om, w_out[l].astype(BF16), xt, mod3, g_ffn[l], seq)

        *tabs, u_b, vt_b = _peer_sel(h2, w_pq[l].astype(BF16), sub_keys[l].astype(BF16), peer_u[l], peer_v[l])
        assert depth == 1
        xt = _peer_ffn(h2, u_b, vt_b, tabs, x1, mod3, modo3, g_final, seq)

    return xt.reshape(batch, seq, d)
```

```python
import functools
import math

import numpy as np
import jax
import jax.numpy as jnp
from jax import lax
from jax.experimental import pallas as pl
from jax.experimental.pallas import tpu as pltpu

F32 = jnp.float32
BF16 = jnp.bfloat16

DA_QK = 64
DA_V = 128
ML_QK = 128
ML_V = 256
CONV_W = 4
GATE_CAP = 15.0
ROPE_THETA = 10000.0
PEER_HEADS = 8
PEER_NKEYS = 128
PEER_TOPK = 16
EPS = 1e-6

LANES = 128
SUBLANES = 8
VMEM_LIMIT = 56 * 1024 * 1024
FFN_VMEM_LIMIT = 60 * 1024 * 1024

ADA_TN = 1024
PROJ_TM = 1024
PROJ_TN = 512
ATT_TQ = 512
ATT_ONES_ROWS = 16
ML_CHUNK = 256
OUT_TM = 256
SEL_TM = 512
FFN_TB = 512
FFN_EC = 512
FFN_PIECE = 512

NEG_INF = float("-inf")


def _cparams(sem):
    return pltpu.CompilerParams(dimension_semantics=sem, vmem_limit_bytes=VMEM_LIMIT)


def _rms(x, g):
    return x * lax.rsqrt(jnp.mean(x * x, axis=-1, keepdims=True) + EPS) * g


def _ada_kernel(c_ref, w_ref, b_ref, o_ref):
    c = c_ref[...]
    cs = c * jax.nn.sigmoid(c)
    o_ref[...] = jnp.dot(cs, w_ref[...], preferred_element_type=F32,
                         precision=lax.Precision.HIGHEST) + b_ref[...]


def _ada(c8, w, b):
    d, n = w.shape
    return pl.pallas_call(
        _ada_kernel,
        grid=(n // ADA_TN,),
        in_specs=[pl.BlockSpec((SUBLANES, d), lambda j: (0, 0)),
                  pl.BlockSpec((d, ADA_TN), lambda j: (0, j)),
                  pl.BlockSpec((1, ADA_TN), lambda j: (0, j))],
        out_specs=pl.BlockSpec((SUBLANES, ADA_TN), lambda j: (0, j)),
        out_shape=jax.ShapeDtypeStruct((SUBLANES, n), F32),
        compiler_params=_cparams(("arbitrary",)),
        name="ada",
    )(c8, w, b.reshape(1, n))


def _proj_kernel(n_rope, n_att, x_ref, mod_ref, g_ref, w_ref, wg_ref, cos_ref, sa_ref, sb_ref,
                 pa_ref, pm_ref, gate_ref, h_scr):
    j = pl.program_id(1)
    nt = (((1,), (1,)), ((), ()))

    @pl.when(j == 0)
    def _():
        y = _rms(x_ref[...], g_ref[...])
        h = y * (1.0 + mod_ref[0, 1:2, :]) + mod_ref[0, 0:1, :]
        hb = h.astype(BF16)
        h_scr[...] = hb
        gate_ref[...] = lax.dot_general(hb, wg_ref[...], nt, preferred_element_type=F32)

    acc = lax.dot_general(h_scr[...], w_ref[...], nt, preferred_element_type=F32)

    @pl.when(j < n_rope)
    def _():
        cos, sa, sb = cos_ref[...], sa_ref[...], sb_ref[...]
        for g in range(PROJ_TN // LANES):
            xg = acc[:, g * LANES:(g + 1) * LANES]
            r = (xg * cos + pltpu.roll(xg, LANES - DA_QK // 2, 1) * sa
                 + pltpu.roll(xg, DA_QK // 2, 1) * sb)
            pa_ref[:, g * LANES:(g + 1) * LANES] = r.astype(BF16)

    @pl.when(jnp.logical_and(j >= n_rope, j < n_att))
    def _():
        pa_ref[...] = acc.astype(BF16)

    @pl.when(j >= n_att)
    def _():
        pm_ref[...] = acc


def _rope_tables(seq):
    half = DA_QK // 2
    inv = ROPE_THETA ** (-jnp.arange(half, dtype=F32) / half)
    ang = jnp.arange(seq, dtype=F32)[:, None] * inv[None, :]
    lane = np.arange(LANES)
    first = (lane % DA_QK) < half
    cos = jnp.cos(ang)[:, lane % half]
    sin = jnp.sin(ang)[:, lane % half]
    sa = jnp.where(first[None, :], -sin, 0.0)
    sb = jnp.where(first[None, :], 0.0, sin)
    return cos, sa, sb


def _proj(x2, mod3, g, w_main_t, w_gate_t, seq, n, n_att_cols, n_rope_cols):
    t, d = x2.shape
    assert n % PROJ_TN == 0 and n <= w_main_t.shape[0]
    n_att = n_att_cols // PROJ_TN
    n_rope = n_rope_cols // PROJ_TN
    n_tiles = n // PROJ_TN
    tiles_per_seq = seq // PROJ_TM
    cos, sa, sb = _rope_tables(seq)
    tab_spec = pl.BlockSpec((PROJ_TM, LANES), lambda i, j: (i % tiles_per_seq, 0))
    return pl.pallas_call(
        functools.partial(_proj_kernel, n_rope, n_att),
        grid=(t // PROJ_TM, n_tiles),
        in_specs=[pl.BlockSpec((PROJ_TM, d), lambda i, j: (i, 0)),
                  pl.BlockSpec((1, 6, d), lambda i, j: (i // tiles_per_seq, 0, 0)),
                  pl.BlockSpec((1, d), lambda i, j: (0, 0)),
                  pl.BlockSpec((PROJ_TN, d), lambda i, j: (j, 0)),
                  pl.BlockSpec((LANES, d), lambda i, j: (0, 0)),
                  tab_spec, tab_spec, tab_spec],
        out_specs=[pl.BlockSpec((PROJ_TM, PROJ_TN), lambda i, j: (i, jnp.minimum(j, n_att - 1))),
                   pl.BlockSpec((PROJ_TM, PROJ_TN), lambda i, j: (i, jnp.maximum(j - n_att, 0))),
                   pl.BlockSpec((PROJ_TM, LANES), lambda i, j: (i, 0))],
        out_shape=[jax.ShapeDtypeStruct((t, n_att_cols), BF16),
                   jax.ShapeDtypeStruct((t, n - n_att_cols), F32),
                   jax.ShapeDtypeStruct((t, LANES), F32)],
        scratch_shapes=[pltpu.VMEM((PROJ_TM, d), BF16)],
        compiler_params=_cparams(("arbitrary", "arbitrary")),
        name="proj",
    )(x2, mod3, g.reshape(1, d), w_main_t, w_gate_t, cos, sa, sb)


def _attn_kernel(lam_init, lam_ref, q_ref, k_ref, v_ref, nrm_ref, o_ref, vt_scr):
    i = pl.program_id(2)
    tq = ATT_TQ
    seq = k_ref.shape[0]
    nq = seq // tq

    @pl.when(i == 0)
    def _():
        vt_scr[:DA_V, :] = v_ref[...].astype(F32).T.astype(BF16)
        vt_scr[DA_V:, :] = jnp.ones((ATT_ONES_ROWS, seq), BF16)

    lv = lam_ref[...]
    lam = (jnp.exp(jnp.sum(lv[0:1] * lv[1:2], axis=-1, keepdims=True))
           - jnp.exp(jnp.sum(lv[2:3] * lv[3:4], axis=-1, keepdims=True)) + lam_init)

    q = q_ref[...] * (DA_QK ** -0.5)
    lane = lax.broadcasted_iota(jnp.int32, (tq, LANES), 1)
    zero = jnp.zeros_like(q)
    q2 = jnp.concatenate([jnp.where(lane < DA_QK, q, zero), jnp.where(lane >= DA_QK, q, zero)], axis=0)
    nt = (((1,), (1,)), ((), ()))

    def block(qi):
        below = qi * tq
        kpos = lax.broadcasted_iota(jnp.int32, (tq, 2 * tq), 0)
        col = lax.broadcasted_iota(jnp.int32, (tq, 2 * tq), 1)
        causal = kpos <= jnp.where(col >= tq, col - tq, col)
        s_diag = lax.dot_general(k_ref[below:below + tq, :], q2, nt, preferred_element_type=F32)
        s_diag = jnp.where(causal, s_diag, NEG_INF)
        m = jnp.max(s_diag, axis=0, keepdims=True)
        if qi > 0:
            s_low = lax.dot_general(k_ref[0:below, :], q2, nt, preferred_element_type=F32)
            m = jnp.maximum(m, jnp.max(s_low, axis=0, keepdims=True))
        acc = jnp.dot(vt_scr[:, below:below + tq], jnp.exp(s_diag - m).astype(BF16),
                      preferred_element_type=F32)
        if qi > 0:
            acc = acc + jnp.dot(vt_scr[:, 0:below], jnp.exp(s_low - m).astype(BF16),
                                preferred_element_type=F32)
        o2 = acc[:DA_V, :] / acc[DA_V:DA_V + 1, :]
        o = (o2[:, :tq] - lam * o2[:, tq:]).T
        o_ref[...] = (_rms(o, nrm_ref[...]) * (1.0 - lam_init)).astype(o_ref.dtype)

    for qi in range(nq):
        pl.when(i == qi)(functools.partial(block, qi))


def _attn(pa, lam4, da_norm, batch, seq, heads, lam_init):
    t = pa.shape[0]
    nq = seq // ATT_TQ
    return pl.pallas_call(
        functools.partial(_attn_kernel, lam_init),
        grid=(batch, heads, nq),
        in_specs=[pl.BlockSpec((4, DA_QK), lambda b, h, i: (0, 0)),
                  pl.BlockSpec((ATT_TQ, LANES), lambda b, h, i: (b * nq + i, h)),
                  pl.BlockSpec((seq, LANES), lambda b, h, i: (b, heads + h)),
                  pl.BlockSpec((seq, LANES), lambda b, h, i: (b, 2 * heads + h)),
                  pl.BlockSpec((1, DA_V), lambda b, h, i: (0, 0))],
        out_specs=pl.BlockSpec((ATT_TQ, DA_V), lambda b, h, i: (b * nq + i, h)),
        out_shape=jax.ShapeDtypeStruct((t, heads * DA_V), BF16),
        scratch_shapes=[pltpu.VMEM((DA_V + ATT_ONES_ROWS, seq), BF16)],
        compiler_params=_cparams(("arbitrary", "arbitrary", "arbitrary")),
        name="attn",
    )(lam4, pa, pa, pa, da_norm.reshape(1, DA_V))


def _softcap(x):
    return GATE_CAP * jnp.tanh(x / GATE_CAP)


def _dwconv_silu(x, w, b):
    seq = x.shape[0]
    row = lax.broadcasted_iota(jnp.int32, x.shape, 0)
    y = x * w[CONV_W - 1:CONV_W]
    for s in range(1, CONV_W):
        xs = jnp.where(row >= s, pltpu.roll(x, s, 0), 0.0)
        y = y + xs * w[CONV_W - 1 - s:CONV_W - s]
    y = y + b
    return y * jax.nn.sigmoid(y)


def _mlstm_kernel(bi_ref, bf_ref, q_ref, k_ref, v_ref, og_ref, cwq_ref, cwk_ref, cbq_ref, cbk_ref,
                  gc_ref, nrm_ref, o_ref,
                  q_scr, k_scr, lic_scr, lfc_scr, lir_scr, lfr_scr, gt_scr, bc_scr, br_scr, c_scr, m_scr):
    hd = pl.program_id(1)
    L = ML_CHUNK
    seq = q_ref.shape[0]
    nc = seq // L
    bi = bi_ref[hd]
    bf = bf_ref[hd]

    q_scr[...] = _dwconv_silu(q_ref[...], cwq_ref[...], cbq_ref[...]).astype(BF16)
    k_scr[...] = _dwconv_silu(k_ref[...], cwk_ref[...], cbk_ref[...]) * (ML_QK ** -0.5)

    gc = gc_ref[...]
    glane = lax.broadcasted_iota(jnp.int32, gc.shape, 1)
    gi = jnp.sum(jnp.where(glane == hd, gc, 0.0), axis=1, keepdims=True)
    gf = jnp.sum(jnp.where(glane == pl.num_programs(1) + hd, gc, 0.0), axis=1, keepdims=True)
    lic_scr[...] = _softcap(gi + bi)
    lfc_scr[...] = jax.nn.log_sigmoid(_softcap(gf + bf))
    gt_scr[...] = gc.T
    li_row = _softcap(gt_scr[pl.ds(hd, 1), :] + bi)
    lf_row = jax.nn.log_sigmoid(_softcap(gt_scr[pl.ds(pl.num_programs(1) + hd, 1), :] + bf))
    for c in range(nc):
        lir_scr[c:c + 1, :] = li_row[:, c * L:(c + 1) * L]
        lfr_scr[c:c + 1, :] = lf_row[:, c * L:(c + 1) * L]

    c_scr[...] = jnp.zeros(c_scr.shape, F32)
    m_scr[...] = jnp.zeros(m_scr.shape, F32)

    r_i = lax.broadcasted_iota(jnp.int32, (L, L), 0)
    c_i = lax.broadcasted_iota(jnp.int32, (L, L), 1)
    tril = (c_i <= r_i)
    tril_b = tril.astype(BF16)
    triu_b = (r_i <= c_i).astype(BF16)
    ones_col = (lax.broadcasted_iota(jnp.int32, (L, LANES), 1) == 0).astype(BF16)
    nrm = nrm_ref[...]

    def split3(x):
        hi = x.astype(BF16)
        r = x - hi.astype(F32)
        mid = r.astype(BF16)
        return hi, mid, (r - mid.astype(F32)).astype(BF16)

    for c in range(nc):
        b_c = None
        for piece in split3(lfc_scr[c * L:(c + 1) * L, :]):
            term = jnp.dot(tril_b, jnp.broadcast_to(piece, (L, L)), preferred_element_type=F32)
            b_c = term if b_c is None else b_c + term
        b_r = None
        for piece in split3(lfr_scr[c:c + 1, :]):
            term = jnp.dot(jnp.broadcast_to(piece, (L, L)), triu_b, preferred_element_type=F32)
            b_r = term if b_r is None else b_r + term
        bc_scr[c] = b_c
        br_scr[c] = b_r

    def body(c, carry):
        start = pl.multiple_of(c * L, L)
        sl = pl.ds(start, L)
        qc = q_scr[sl, :]
        kc = k_scr[sl, :]
        vc = v_ref[sl, :].astype(BF16)
        lic = lic_scr[sl, :]
        lir = lir_scr[pl.ds(c, 1), :]
        m = m_scr[...]

        b_c = bc_scr[c]
        b_r = br_scr[c]
        dmat = jnp.where(tril, b_c - b_r + lir, NEG_INF)
        bcol = b_c[:, 0:1]
        inter = bcol + m
        m_j = jnp.maximum(inter, jnp.max(dmat, axis=-1, keepdims=True))
        w_intra = jnp.exp(dmat - m_j)
        w_inter = jnp.exp(inter - m_j)

        sqk = lax.dot_general(qc, kc.astype(BF16), (((1,), (1,)), ((), ())), preferred_element_type=F32)
        qk = sqk * w_intra
        q_c = jnp.dot(qc, c_scr[...].astype(BF16), preferred_element_type=F32)
        num = w_inter * q_c[:, :ML_V] + jnp.dot(qk.astype(BF16), vc, preferred_element_type=F32)
        den = w_inter * q_c[:, ML_V:ML_V + 1] + jnp.sum(qk, axis=-1, keepdims=True)
        hh = num / jnp.maximum(jnp.abs(den), jnp.exp(-m_j))
        og = og_ref[sl, :]
        o_ref[sl, :] = (_rms(hh, nrm) * jax.nn.sigmoid(og)).astype(o_ref.dtype)

        b_l = b_c[L - 1:L, 0:1]
        logw = b_l - bcol + lic
        m_new = jnp.maximum(b_l + m, jnp.max(logw, axis=0, keepdims=True))
        decay = jnp.exp(b_l + m - m_new)
        ws = jnp.exp(logw - m_new)
        kw = (kc * ws).astype(BF16)
        vext = jnp.concatenate([vc, ones_col], axis=1)
        upd = lax.dot_general(kw, vext, (((0,), (0,)), ((), ())), preferred_element_type=F32)
        c_scr[...] = decay * c_scr[...] + upd
        m_scr[...] = m_new
        return carry

    lax.fori_loop(0, nc, body, 0)


def _mlstm(pm, gates, conv_w, conv_b, b_ig, b_fg, ml_norm, batch, seq, heads):
    t = pm.shape[0]
    nc = seq // ML_CHUNK
    qk_w = heads * ML_QK
    vblk = 2 * qk_w // ML_V
    smem = pl.BlockSpec(memory_space=pltpu.SMEM)
    return pl.pallas_call(
        _mlstm_kernel,
        grid=(batch, heads),
        in_specs=[smem, smem,
                  pl.BlockSpec((seq, ML_QK), lambda b, h: (b, h)),
                  pl.BlockSpec((seq, ML_QK), lambda b, h: (b, heads + h)),
                  pl.BlockSpec((seq, ML_V), lambda b, h: (b, vblk + h)),
                  pl.BlockSpec((seq, ML_V), lambda b, h: (b, vblk + heads + h)),
                  pl.BlockSpec((CONV_W, ML_QK), lambda b, h: (0, h)),
                  pl.BlockSpec((CONV_W, ML_QK), lambda b, h: (0, heads + h)),
                  pl.BlockSpec((1, ML_QK), lambda b, h: (0, h)),
                  pl.BlockSpec((1, ML_QK), lambda b, h: (0, heads + h)),
                  pl.BlockSpec((seq, LANES), lambda b, h: (b, 0)),
                  pl.BlockSpec((1, ML_V), lambda b, h: (0, 0))],
        out_specs=pl.BlockSpec((seq, ML_V), lambda b, h: (b, h)),
        out_shape=jax.ShapeDtypeStruct((t, heads * ML_V), BF16),
        scratch_shapes=[pltpu.VMEM((seq, ML_QK), BF16), pltpu.VMEM((seq, ML_QK), F32),
                        pltpu.VMEM((seq, 1), F32), pltpu.VMEM((seq, 1), F32),
                        pltpu.VMEM((nc, ML_CHUNK), F32), pltpu.VMEM((nc, ML_CHUNK), F32),
                        pltpu.VMEM((LANES, seq), F32),
                        pltpu.VMEM((nc, ML_CHUNK, ML_CHUNK), F32), pltpu.VMEM((nc, ML_CHUNK, ML_CHUNK), F32),
                        pltpu.VMEM((ML_QK, ML_V + LANES), F32), pltpu.VMEM((1, 1), F32)],
        compiler_params=_cparams(("arbitrary", "arbitrary")),
        name="mlstm",
    )(b_ig, b_fg, pm, pm, pm, pm, conv_w, conv_w, conv_b.reshape(1, -1), conv_b.reshape(1, -1),
      gates, ml_norm.reshape(1, ML_V))


def _outproj_kernel(oa_ref, om_ref, w_ref, x_ref, mod_ref, g_ref, x1_ref, h2_ref):
    ka = oa_ref.shape[1]
    mixed = (jnp.dot(oa_ref[...], w_ref[:ka, :], preferred_element_type=F32)
             + jnp.dot(om_ref[...], w_ref[ka:, :], preferred_element_type=F32))
    x1 = x_ref[...] + mod_ref[0, 2:3, :] * mixed
    x1_ref[...] = x1
    h2 = _rms(x1, g_ref[...]) * (1.0 + mod_ref[0, 4:5, :]) + mod_ref[0, 3:4, :]
    h2_ref[...] = h2.astype(BF16)


def _outproj(oa, om, w_out, x2, mod3, g_ffn, seq):
    t, d = x2.shape
    tiles_per_seq = seq // OUT_TM
    return pl.pallas_call(
        _outproj_kernel,
        grid=(t // OUT_TM,),
        in_specs=[pl.BlockSpec((OUT_TM, oa.shape[1]), lambda i: (i, 0)),
                  pl.BlockSpec((OUT_TM, om.shape[1]), lambda i: (i, 0)),
                  pl.BlockSpec(w_out.shape, lambda i: (0, 0)),
                  pl.BlockSpec((OUT_TM, d), lambda i: (i, 0)),
                  pl.BlockSpec((1, 6, d), lambda i: (i // tiles_per_seq, 0, 0)),
                  pl.BlockSpec((1, d), lambda i: (0, 0))],
        out_specs=[pl.BlockSpec((OUT_TM, d), lambda i: (i, 0)),
                   pl.BlockSpec((OUT_TM, d), lambda i: (i, 0))],
        out_shape=[jax.ShapeDtypeStruct((t, d), F32), jax.ShapeDtypeStruct((t, d), BF16)],
        compiler_params=_cparams(("arbitrary",)),
        name="outproj",
    )(oa, om, w_out, x2, mod3, g_ffn.reshape(1, d))


_CAND_ROWS = 16 + 8 + 6 * 8 + 8


def _pick_max(work, iota, exact):
    mx = jnp.max(work, axis=0, keepdims=True)
    if not exact:
        return mx, work == mx
    first = jnp.min(jnp.where(work == mx, iota, float(work.shape[0])), axis=0, keepdims=True)
    return mx, iota == first


def _rank16(s, top_scr, exact, want_rank=True):
    iota = lax.broadcasted_iota(jnp.int32, s.shape, 0).astype(F32) if exact else None
    rank = jnp.full(s.shape, float(PEER_TOPK), F32) if want_rank else None
    for r in range(PEER_TOPK):
        mx, sel = _pick_max(s, iota, exact)
        if want_rank:
            rank = jnp.where(sel, float(r), rank)
        s = jnp.where(sel, NEG_INF, s)
        top_scr[r:r + 1, :] = mx
    return rank, s


def _selection_tables(s1, s2, t1_scr, t2_scr, exact):
    k = PEER_TOPK
    r1, left1 = _rank16(s1, t1_scr, exact, want_rank=exact)
    r2, _ = _rank16(s2, t2_scr, exact)
    t1 = t1_scr[...]
    t2 = t2_scr[...]
    tm = s1.shape[1]

    brow = lax.broadcasted_iota(jnp.int32, (SUBLANES, tm), 0)
    pieces = [t1[0:1] + t2, t1[1:2] + t2[0:SUBLANES]]
    for a in range(2, SUBLANES):
        pieces.append(jnp.where(brow < k // (a + 1), t1[a:a + 1] + t2[0:SUBLANES], NEG_INF))
    pieces.append(t1[SUBLANES:] + t2[0:1])
    cand = jnp.concatenate(pieces, axis=0)
    cmax = cand[0:1]

    work = cand
    iota = lax.broadcasted_iota(jnp.int32, cand.shape, 0).astype(F32) if exact else None
    for _ in range(k):
        _, sel = _pick_max(work, iota, exact)
        work = jnp.where(sel, NEG_INF, work)
    taken = jnp.where(work != cand, 1.0, 0.0)
    z = jnp.sum(taken * jnp.exp(cand - cmax), axis=0, keepdims=True)

    n_rows = [jnp.sum(taken[0:16], axis=0, keepdims=True)]
    for a in range(1, SUBLANES):
        lo = 16 + (a - 1) * SUBLANES
        n_rows.append(jnp.sum(taken[lo:lo + SUBLANES], axis=0, keepdims=True))
    base = 16 + 7 * SUBLANES
    for a in range(SUBLANES, k):
        n_rows.append(taken[base + a - SUBLANES:base + a - SUBLANES + 1])
    n1 = jnp.zeros(s1.shape, F32)
    for a in range(k):
        hit = (r1 == float(a)) if exact else (s1 == t1[a:a + 1])
        n1 = jnp.where(hit, n_rows[a], n1)

    extra = None
    if not exact:
        def count(x):
            return jnp.sum(x, axis=0, keepdims=True) - float(k)
        extra = (count(jnp.where(left1 != s1, 1.0, 0.0)) + count(jnp.where(r2 < float(k), 1.0, 0.0))
                 + count(taken))
    return (r2, jnp.exp(s2 - t2[0:1]), n1, jnp.exp(s1 - t1[0:1]) / z), extra


def _peer_sel_kernel(h2_ref, w_ref, keys_ref, u_ref, v_ref, r2_ref, e2_ref, n1_ref, e1_ref, ub_ref, vtb_ref,
                     t1_scr, t2_scr):
    ub_ref[...] = u_ref[...].astype(BF16)
    vtb_ref[...] = v_ref[...].T.astype(BF16)

    half = PEER_NKEYS
    q = jnp.dot(h2_ref[...], w_ref[...], preferred_element_type=F32).astype(BF16)
    nt = (((1,), (1,)), ((), ()))
    s1 = lax.dot_general(keys_ref[0, 0], q[:, :half], nt, preferred_element_type=F32)
    s2 = lax.dot_general(keys_ref[0, 1], q[:, half:], nt, preferred_element_type=F32)

    def store(tables):
        for ref, val in zip((r2_ref, e2_ref, n1_ref, e1_ref), tables):
            ref[0] = val.astype(ref.dtype)

    tables, extra = _selection_tables(s1, s2, t1_scr, t2_scr, exact=False)
    store(tables)

    @pl.when(jnp.max(extra) > 0.0)
    def _():
        store(_selection_tables(s1, s2, t1_scr, t2_scr, exact=True)[0])


def _peer_sel(h2, w_pq, keys, peer_u, peer_v):
    t, d = h2.shape
    ne = peer_u.shape[0]
    nh = PEER_HEADS
    qd = 2 * PEER_NKEYS
    steps = (t // SEL_TM) * nh
    rows = ne // steps
    assert rows * steps == ne and rows % LANES == 0
    tab = jax.ShapeDtypeStruct((nh, PEER_NKEYS, t), F32)
    tab16 = jax.ShapeDtypeStruct((nh, PEER_NKEYS, t), BF16)
    tab_spec = pl.BlockSpec((1, PEER_NKEYS, SEL_TM), lambda i, h: (h, 0, i))
    return pl.pallas_call(
        _peer_sel_kernel,
        grid=(t // SEL_TM, nh),
        in_specs=[pl.BlockSpec((SEL_TM, d), lambda i, h: (i, 0)),
                  pl.BlockSpec((d, qd), lambda i, h: (0, h)),
                  pl.BlockSpec((1, 2, PEER_NKEYS, PEER_NKEYS), lambda i, h: (h, 0, 0, 0)),
                  pl.BlockSpec((rows, d), lambda i, h: (i * nh + h, 0)),
                  pl.BlockSpec((rows, d), lambda i, h: (i * nh + h, 0))],
        out_specs=[tab_spec, tab_spec, tab_spec, tab_spec,
                   pl.BlockSpec((rows, d), lambda i, h: (i * nh + h, 0)),
                   pl.BlockSpec((d, rows), lambda i, h: (0, i * nh + h))],
        out_shape=[tab16, tab16, tab, tab,
                   jax.ShapeDtypeStruct((ne, d), BF16), jax.ShapeDtypeStruct((d, ne), BF16)],
        scratch_shapes=[pltpu.VMEM((PEER_TOPK, SEL_TM), F32), pltpu.VMEM((PEER_TOPK, SEL_TM), F32)],
        compiler_params=_cparams(("arbitrary", "arbitrary")),
        name="peer_sel",
    )(h2, w_pq, keys, peer_u, peer_v)


def _peer_ffn_kernel(h2_ref, u0_ref, uodd_ref, unext_ref, vt_ref, r2_ref, e2_ref, n1_ref, e1_ref, x1_ref,
                     mod_ref, modo_ref, g_ref, o_ref, acc_scr, a0_scr, a1_scr):
    c = pl.program_id(1)
    nk = PEER_NKEYS
    per = FFN_EC // nk
    nt = (((1,), (1,)), ((), ()))
    zero = jnp.zeros((), BF16)

    def preact(u_ref, dst_scr, lo, hi):
        dst_scr[lo:hi, :] = lax.dot_general(u_ref[lo:hi, :], h2_ref[...], nt, preferred_element_type=F32)

    def weights(a_scr, chunk, lo, hi):
        parts = []
        for ii in range(lo // nk, hi // nk):
            i = chunk * per + ii
            a = a_scr[ii * nk:(ii + 1) * nk, :]
            gsum = None
            for hd in range(PEER_HEADS):
                n1 = n1_ref[hd, pl.ds(i, 1), :].astype(BF16)
                e1 = e1_ref[hd, pl.ds(i, 1), :].astype(BF16)
                g = jnp.where(r2_ref[hd] < n1, e2_ref[hd], zero) * e1
                gsum = g if gsum is None else gsum + g
            act = 0.5 * a * (1.0 + lax.erf(a * (2.0 ** -0.5)))
            parts.append(gsum * act.astype(BF16))
        return jnp.concatenate(parts, axis=0)

    def accumulate(w, col0, lo, hi):
        acc_scr[...] += jnp.dot(vt_ref[:, col0 + lo:col0 + hi], w, preferred_element_type=F32)

    @pl.when(c == 0)
    def _():
        acc_scr[...] = jnp.zeros(acc_scr.shape, F32)
        preact(u0_ref, a0_scr, 0, FFN_EC)

    bounds = [(p * FFN_PIECE, (p + 1) * FFN_PIECE) for p in range(FFN_EC // FFN_PIECE)]
    pieces = ([(a0_scr, 2 * c, 0, uodd_ref, a1_scr, lo, hi) for lo, hi in bounds]
              + [(a1_scr, 2 * c + 1, FFN_EC, unext_ref, a0_scr, lo, hi) for lo, hi in bounds])
    pending = None
    for src_scr, chunk, col0, u_ref, dst_scr, lo, hi in pieces:
        w = weights(src_scr, chunk, lo, hi)
        if pending is not None:
            accumulate(*pending)
        preact(u_ref, dst_scr, lo, hi)
        pending = (w, col0, lo, hi)
    accumulate(*pending)

    @pl.when(c == pl.num_programs(1) - 1)
    def _():
        y = acc_scr[...].T
        x2 = x1_ref[...] + mod_ref[0, 5:6, :] * y
        o_ref[...] = _rms(x2, g_ref[...]) * (1.0 + modo_ref[0, 1:2, :]) + modo_ref[0, 0:1, :]


def _peer_ffn(h2, u, vt, tabs, x1, mod3, modo3, g_final, seq):
    t, d = h2.shape
    ne = u.shape[0]
    tiles_per_seq = seq // FFN_TB
    tab_spec = pl.BlockSpec((PEER_HEADS, PEER_NKEYS, FFN_TB), lambda i, c: (0, 0, i))
    nchunk = ne // FFN_EC
    once = pl.Buffered(1)
    tab_once = pl.BlockSpec((PEER_HEADS, PEER_NKEYS, FFN_TB), lambda i, c: (0, 0, i), pipeline_mode=once)
    return pl.pallas_call(
        _peer_ffn_kernel,
        grid=(t // FFN_TB, nchunk // 2),
        in_specs=[pl.BlockSpec((FFN_TB, d), lambda i, c: (i, 0)),
                  pl.BlockSpec((FFN_EC, d), lambda i, c: (0, 0), pipeline_mode=once),
                  pl.BlockSpec((FFN_EC, d), lambda i, c: (2 * c + 1, 0)),
                  pl.BlockSpec((FFN_EC, d), lambda i, c: (jnp.minimum(2 * c + 2, nchunk - 1), 0)),
                  pl.BlockSpec((d, 2 * FFN_EC), lambda i, c: (0, c)),
                  tab_spec, tab_spec, tab_once, tab_once,
                  pl.BlockSpec((FFN_TB, d), lambda i, c: (i, 0), pipeline_mode=once),
                  pl.BlockSpec((1, 6, d), lambda i, c: (i // tiles_per_seq, 0, 0)),
                  pl.BlockSpec((1, 2, d), lambda i, c: (i // tiles_per_seq, 0, 0)),
                  pl.BlockSpec((1, d), lambda i, c: (0, 0))],
        out_specs=pl.BlockSpec((FFN_TB, d), lambda i, c: (i, 0)),
        out_shape=jax.ShapeDtypeStruct((t, d), F32),
        scratch_shapes=[pltpu.VMEM((d, FFN_TB), F32), pltpu.VMEM((FFN_EC, FFN_TB), F32),
                        pltpu.VMEM((FFN_EC, FFN_TB), F32)],
        compiler_params=pltpu.CompilerParams(dimension_semantics=("arbitrary", "arbitrary"),
                                             vmem_limit_bytes=FFN_VMEM_LIMIT),
        name="peer_ffn",
    )(h2, u, u, u, vt, *tabs, x1, mod3, modo3, g_final.reshape(1, d))


def kernel(x, c, w_ada, b_ada, g_mix, w_in, conv_w, conv_b, b_igate, b_fgate, lambda_q1, lambda_k1,
           lambda_q2, lambda_k2, da_norm, ml_norm, w_out, g_ffn, w_pq, sub_keys, peer_u, peer_v,
           w_ada_final, b_ada_final, g_final):
    batch, seq, d = x.shape
    depth = w_ada.shape[0]
    t = batch * seq
    da_heads = d // 256
    ml_heads = d // 512
    att_cols = 3 * da_heads * DA_V
    rope_cols = 2 * da_heads * DA_V
    main_cols = att_cols + 2 * ml_heads * ML_QK + 2 * ml_heads * ML_V
    assert batch <= SUBLANES and seq % PROJ_TM == 0 and seq % ATT_TQ == 0 and seq % ML_CHUNK == 0
    assert w_in.shape[2] == main_cols + 2 * ml_heads

    c8 = jnp.zeros((SUBLANES, d), F32).at[:batch].set(c.astype(F32))
    modo3 = _ada(c8, w_ada_final, b_ada_final)[:batch].reshape(batch, 2, d)
    xt = x.reshape(t, d)

    for l in range(depth):
        mod3 = _ada(c8, w_ada[l], b_ada[l])[:batch].reshape(batch, 6, d)
        lam_init = 0.8 - 0.6 * math.exp(-0.3 * l)

        w_in_t = jnp.swapaxes(w_in[l], 0, 1).astype(BF16)
        w_gate_t = jnp.zeros((LANES, d), BF16).at[:2 * ml_heads].set(w_in_t[main_cols:])
        pa, pm, gates = _proj(xt, mod3, g_mix[l], w_in_t, w_gate_t, seq, main_cols, att_cols, rope_cols)

        lam4 = jnp.stack([lambda_q1[l], lambda_k1[l], lambda_q2[l], lambda_k2[l]]).astype(F32)
        oa = _attn(pa, lam4, da_norm[l], batch, seq, da_heads, lam_init)

        om = _mlstm(pm, gates, conv_w[l], conv_b[l], b_igate[l], b_fgate[l], ml_norm[l], batch, seq, ml_heads)

        x1, h2 = _outproj(oa, om, w_out[l].astype(BF16), xt, mod3, g_ffn[l], seq)

        *tabs, u_b, vt_b = _peer_sel(h2, w_pq[l].astype(BF16), sub_keys[l].astype(BF16), peer_u[l], peer_v[l])
        assert depth == 1
        xt = _peer_ffn(h2, u_b, vt_b, tabs, x1, mod3, modo3, g_final, seq)

    return xt.reshape(batch, seq, d)
```

```python
import functools
import math

import numpy as np
import jax
import jax.numpy as jnp
from jax import lax
from jax.experimental import pallas as pl
from jax.experimental.pallas import tpu as pltpu

F32 = jnp.float32
BF16 = jnp.bfloat16

DA_QK = 64
DA_V = 128
ML_QK = 128
ML_V = 256
CONV_W = 4
GATE_CAP = 15.0
ROPE_THETA = 10000.0
PEER_HEADS = 8
PEER_NKEYS = 128
PEER_TOPK = 16
EPS = 1e-6

LANES = 128
SUBLANES = 8
VMEM_LIMIT = 56 * 1024 * 1024
FFN_VMEM_LIMIT = 60 * 1024 * 1024

ADA_TN = 1024
PROJ_TM = 256
PROJ_TN = 512
ATT_TQ = 512
ATT_ONES_ROWS = 16
ML_CHUNK = 256
OUT_TM = 256
SEL_TM = 512
FFN_TB = 512
FFN_EC = 512
FFN_PIECE = 512

NEG_INF = float("-inf")


def _cparams(sem):
    return pltpu.CompilerParams(dimension_semantics=sem, vmem_limit_bytes=VMEM_LIMIT)


def _rms(x, g):
    return x * lax.rsqrt(jnp.mean(x * x, axis=-1, keepdims=True) + EPS) * g


def _ada_kernel(c_ref, w_ref, b_ref, o_ref):
    c = c_ref[...]
    cs = c * jax.nn.sigmoid(c)
    o_ref[...] = jnp.dot(cs, w_ref[...], preferred_element_type=F32,
                         precision=lax.Precision.HIGHEST) + b_ref[...]


def _ada(c8, w, b):
    d, n = w.shape
    return pl.pallas_call(
        _ada_kernel,
        grid=(n // ADA_TN,),
        in_specs=[pl.BlockSpec((SUBLANES, d), lambda j: (0, 0)),
                  pl.BlockSpec((d, ADA_TN), lambda j: (0, j)),
                  pl.BlockSpec((1, ADA_TN), lambda j: (0, j))],
        out_specs=pl.BlockSpec((SUBLANES, ADA_TN), lambda j: (0, j)),
        out_shape=jax.ShapeDtypeStruct((SUBLANES, n), F32),
        compiler_params=_cparams(("arbitrary",)),
        name="ada",
    )(c8, w, b.reshape(1, n))


def _proj_kernel(n_rope, n_att, n_tiles, x_ref, mod_ref, g_ref, w_ref, wg_ref, cos_ref, sa_ref, sb_ref,
                 pa_ref, pm_ref, gate_ref):
    nt = (((1,), (1,)), ((), ()))
    tn = PROJ_TN
    y = _rms(x_ref[...], g_ref[...])
    hb = (y * (1.0 + mod_ref[0, 1:2, :]) + mod_ref[0, 0:1, :]).astype(BF16)
    gate_ref[...] = lax.dot_general(hb, wg_ref[...], nt, preferred_element_type=F32)
    cos, sa, sb = cos_ref[...], sa_ref[...], sb_ref[...]
    for j in range(n_tiles):
        acc = lax.dot_general(hb, w_ref[j * tn:(j + 1) * tn, :], nt, preferred_element_type=F32)
        if j < n_rope:
            for g in range(tn // LANES):
                xg = acc[:, g * LANES:(g + 1) * LANES]
                r = (xg * cos + pltpu.roll(xg, LANES - DA_QK // 2, 1) * sa
                     + pltpu.roll(xg, DA_QK // 2, 1) * sb)
                pa_ref[:, j * tn + g * LANES:j * tn + (g + 1) * LANES] = r.astype(BF16)
        elif j < n_att:
            pa_ref[:, j * tn:(j + 1) * tn] = acc.astype(BF16)
        else:
            pm_ref[:, (j - n_att) * tn:(j - n_att + 1) * tn] = acc


def _rope_tables(seq):
    half = DA_QK // 2
    inv = ROPE_THETA ** (-jnp.arange(half, dtype=F32) / half)
    ang = jnp.arange(seq, dtype=F32)[:, None] * inv[None, :]
    lane = np.arange(LANES)
    first = (lane % DA_QK) < half
    cos = jnp.cos(ang)[:, lane % half]
    sin = jnp.sin(ang)[:, lane % half]
    sa = jnp.where(first[None, :], -sin, 0.0)
    sb = jnp.where(first[None, :], 0.0, sin)
    return cos, sa, sb


def _proj(x2, mod3, g, w_main_t, w_gate_t, seq, n, n_att_cols, n_rope_cols):
    t, d = x2.shape
    assert n % PROJ_TN == 0 and n <= w_main_t.shape[0]
    n_att = n_att_cols // PROJ_TN
    n_rope = n_rope_cols // PROJ_TN
    n_tiles = n // PROJ_TN
    tiles_per_seq = seq // PROJ_TM
    cos, sa, sb = _rope_tables(seq)
    tab_spec = pl.BlockSpec((PROJ_TM, LANES), lambda i: (i % tiles_per_seq, 0))
    once = pl.Buffered(1)
    return pl.pallas_call(
        functools.partial(_proj_kernel, n_rope, n_att, n_tiles),
        grid=(t // PROJ_TM,),
        in_specs=[pl.BlockSpec((PROJ_TM, d), lambda i: (i, 0)),
                  pl.BlockSpec((1, 6, d), lambda i: (i // tiles_per_seq, 0, 0)),
                  pl.BlockSpec((1, d), lambda i: (0, 0)),
                  pl.BlockSpec(w_main_t.shape, lambda i: (0, 0), pipeline_mode=once),
                  pl.BlockSpec((LANES, d), lambda i: (0, 0), pipeline_mode=once),
                  tab_spec, tab_spec, tab_spec],
        out_specs=[pl.BlockSpec((PROJ_TM, n_att_cols), lambda i: (i, 0)),
                   pl.BlockSpec((PROJ_TM, n - n_att_cols), lambda i: (i, 0)),
                   pl.BlockSpec((PROJ_TM, LANES), lambda i: (i, 0))],
        out_shape=[jax.ShapeDtypeStruct((t, n_att_cols), BF16),
                   jax.ShapeDtypeStruct((t, n - n_att_cols), F32),
                   jax.ShapeDtypeStruct((t, LANES), F32)],
        compiler_params=_cparams(("arbitrary",)),
        name="proj",
    )(x2, mod3, g.reshape(1, d), w_main_t, w_gate_t, cos, sa, sb)


def _attn_kernel(lam_init, lam_ref, q_ref, k_ref, v_ref, nrm_ref, o_ref, vt_scr):
    i = pl.program_id(2)
    tq = ATT_TQ
    seq = k_ref.shape[0]
    nq = seq // tq

    @pl.when(i == 0)
    def _():
        vt_scr[:DA_V, :] = v_ref[...].astype(F32).T.astype(BF16)
        vt_scr[DA_V:, :] = jnp.ones((ATT_ONES_ROWS, seq), BF16)

    lv = lam_ref[...]
    lam = (jnp.exp(jnp.sum(lv[0:1] * lv[1:2], axis=-1, keepdims=True))
           - jnp.exp(jnp.sum(lv[2:3] * lv[3:4], axis=-1, keepdims=True)) + lam_init)

    q = q_ref[...] * (DA_QK ** -0.5)
    lane = lax.broadcasted_iota(jnp.int32, (tq, LANES), 1)
    zero = jnp.zeros_like(q)
    q2 = jnp.concatenate([jnp.where(lane < DA_QK, q, zero), jnp.where(lane >= DA_QK, q, zero)], axis=0)
    nt = (((1,), (1,)), ((), ()))

    def block(qi):
        below = qi * tq
        kpos = lax.broadcasted_iota(jnp.int32, (tq, 2 * tq), 0)
        col = lax.broadcasted_iota(jnp.int32, (tq, 2 * tq), 1)
        causal = kpos <= jnp.where(col >= tq, col - tq, col)
        s_diag = lax.dot_general(k_ref[below:below + tq, :], q2, nt, preferred_element_type=F32)
        s_diag = jnp.where(causal, s_diag, NEG_INF)
        m = jnp.max(s_diag, axis=0, keepdims=True)
        if qi > 0:
            s_low = lax.dot_general(k_ref[0:below, :], q2, nt, preferred_element_type=F32)
            m = jnp.maximum(m, jnp.max(s_low, axis=0, keepdims=True))
        acc = jnp.dot(vt_scr[:, below:below + tq], jnp.exp(s_diag - m).astype(BF16),
                      preferred_element_type=F32)
        if qi > 0:
            acc = acc + jnp.dot(vt_scr[:, 0:below], jnp.exp(s_low - m).astype(BF16),
                                preferred_element_type=F32)
        o2 = acc[:DA_V, :] / acc[DA_V:DA_V + 1, :]
        o = (o2[:, :tq] - lam * o2[:, tq:]).T
        o_ref[...] = (_rms(o, nrm_ref[...]) * (1.0 - lam_init)).astype(o_ref.dtype)

    for qi in range(nq):
        pl.when(i == qi)(functools.partial(block, qi))


def _attn(pa, lam4, da_norm, batch, seq, heads, lam_init):
    t = pa.shape[0]
    nq = seq // ATT_TQ
    return pl.pallas_call(
        functools.partial(_attn_kernel, lam_init),
        grid=(batch, heads, nq),
        in_specs=[pl.BlockSpec((4, DA_QK), lambda b, h, i: (0, 0)),
                  pl.BlockSpec((ATT_TQ, LANES), lambda b, h, i: (b * nq + i, h)),
                  pl.BlockSpec((seq, LANES), lambda b, h, i: (b, heads + h)),
                  pl.BlockSpec((seq, LANES), lambda b, h, i: (b, 2 * heads + h)),
                  pl.BlockSpec((1, DA_V), lambda b, h, i: (0, 0))],
        out_specs=pl.BlockSpec((ATT_TQ, DA_V), lambda b, h, i: (b * nq + i, h)),
        out_shape=jax.ShapeDtypeStruct((t, heads * DA_V), BF16),
        scratch_shapes=[pltpu.VMEM((DA_V + ATT_ONES_ROWS, seq), BF16)],
        compiler_params=_cparams(("arbitrary", "arbitrary", "arbitrary")),
        name="attn",
    )(lam4, pa, pa, pa, da_norm.reshape(1, DA_V))


def _softcap(x):
    return GATE_CAP * jnp.tanh(x / GATE_CAP)


def _dwconv_silu(x, w, b):
    seq = x.shape[0]
    row = lax.broadcasted_iota(jnp.int32, x.shape, 0)
    y = x * w[CONV_W - 1:CONV_W]
    for s in range(1, CONV_W):
        xs = jnp.where(row >= s, pltpu.roll(x, s, 0), 0.0)
        y = y + xs * w[CONV_W - 1 - s:CONV_W - s]
    y = y + b
    return y * jax.nn.sigmoid(y)


def _mlstm_kernel(bi_ref, bf_ref, q_ref, k_ref, v_ref, og_ref, cwq_ref, cwk_ref, cbq_ref, cbk_ref,
                  gc_ref, nrm_ref, o_ref,
                  q_scr, k_scr, lic_scr, lfc_scr, lir_scr, lfr_scr, gt_scr, bc_scr, br_scr, c_scr, m_scr):
    hd = pl.program_id(1)
    L = ML_CHUNK
    seq = q_ref.shape[0]
    nc = seq // L
    bi = bi_ref[hd]
    bf = bf_ref[hd]

    q_scr[...] = _dwconv_silu(q_ref[...], cwq_ref[...], cbq_ref[...]).astype(BF16)
    k_scr[...] = _dwconv_silu(k_ref[...], cwk_ref[...], cbk_ref[...]) * (ML_QK ** -0.5)

    gc = gc_ref[...]
    glane = lax.broadcasted_iota(jnp.int32, gc.shape, 1)
    gi = jnp.sum(jnp.where(glane == hd, gc, 0.0), axis=1, keepdims=True)
    gf = jnp.sum(jnp.where(glane == pl.num_programs(1) + hd, gc, 0.0), axis=1, keepdims=True)
    lic_scr[...] = _softcap(gi + bi)
    lfc_scr[...] = jax.nn.log_sigmoid(_softcap(gf + bf))
    gt_scr[...] = gc.T
    li_row = _softcap(gt_scr[pl.ds(hd, 1), :] + bi)
    lf_row = jax.nn.log_sigmoid(_softcap(gt_scr[pl.ds(pl.num_programs(1) + hd, 1), :] + bf))
    for c in range(nc):
        lir_scr[c:c + 1, :] = li_row[:, c * L:(c + 1) * L]
        lfr_scr[c:c + 1, :] = lf_row[:, c * L:(c + 1) * L]

    c_scr[...] = jnp.zeros(c_scr.shape, F32)
    m_scr[...] = jnp.zeros(m_scr.shape, F32)

    r_i = lax.broadcasted_iota(jnp.int32, (L, L), 0)
    c_i = lax.broadcasted_iota(jnp.int32, (L, L), 1)
    tril = (c_i <= r_i)
    tril_b = tril.astype(BF16)
    triu_b = (r_i <= c_i).astype(BF16)
    ones_col = (lax.broadcasted_iota(jnp.int32, (L, LANES), 1) == 0).astype(BF16)
    nrm = nrm_ref[...]

    def split3(x):
        hi = x.astype(BF16)
        r = x - hi.astype(F32)
        mid = r.astype(BF16)
        return hi, mid, (r - mid.astype(F32)).astype(BF16)

    for c in range(nc):
        b_c = None
        for piece in split3(lfc_scr[c * L:(c + 1) * L, :]):
            term = jnp.dot(tril_b, jnp.broadcast_to(piece, (L, L)), preferred_element_type=F32)
            b_c = term if b_c is None else b_c + term
        b_r = None
        for piece in split3(lfr_scr[c:c + 1, :]):
            term = jnp.dot(jnp.broadcast_to(piece, (L, L)), triu_b, preferred_element_type=F32)
            b_r = term if b_r is None else b_r + term
        bc_scr[c] = b_c
        br_scr[c] = b_r

    def body(c):
        sl = slice(c * L, (c + 1) * L)
        qc = q_scr[sl, :]
        kc = k_scr[sl, :]
        vc = v_ref[sl, :].astype(BF16)
        lic = lic_scr[sl, :]
        lir = lir_scr[c:c + 1, :]
        m = m_scr[...]

        b_c = bc_scr[c]
        b_r = br_scr[c]
        dmat = jnp.where(tril, b_c - b_r + lir, NEG_INF)
        bcol = b_c[:, 0:1]
        inter = bcol + m
        m_j = jnp.maximum(inter, jnp.max(dmat, axis=-1, keepdims=True))
        w_intra = jnp.exp(dmat - m_j)
        w_inter = jnp.exp(inter - m_j)

        sqk = lax.dot_general(qc, kc.astype(BF16), (((1,), (1,)), ((), ())), preferred_element_type=F32)
        qk = sqk * w_intra
        q_c = jnp.dot(qc, c_scr[...].astype(BF16), preferred_element_type=F32)
        num = w_inter * q_c[:, :ML_V] + jnp.dot(qk.astype(BF16), vc, preferred_element_type=F32)
        den = w_inter * q_c[:, ML_V:ML_V + 1] + jnp.sum(qk, axis=-1, keepdims=True)
        hh = num / jnp.maximum(jnp.abs(den), jnp.exp(-m_j))
        og = og_ref[sl, :]
        o_ref[sl, :] = (_rms(hh, nrm) * jax.nn.sigmoid(og)).astype(o_ref.dtype)

        b_l = b_c[L - 1:L, 0:1]
        logw = b_l - bcol + lic
        m_new = jnp.maximum(b_l + m, jnp.max(logw, axis=0, keepdims=True))
        decay = jnp.exp(b_l + m - m_new)
        ws = jnp.exp(logw - m_new)
        kw = (kc * ws).astype(BF16)
        vext = jnp.concatenate([vc, ones_col], axis=1)
        upd = lax.dot_general(kw, vext, (((0,), (0,)), ((), ())), preferred_element_type=F32)
        c_scr[...] = decay * c_scr[...] + upd
        m_scr[...] = m_new

    for c in range(nc):
        body(c)


def _mlstm(pm, gates, conv_w, conv_b, b_ig, b_fg, ml_norm, batch, seq, heads):
    t = pm.shape[0]
    nc = seq // ML_CHUNK
    qk_w = heads * ML_QK
    vblk = 2 * qk_w // ML_V
    smem = pl.BlockSpec(memory_space=pltpu.SMEM)
    return pl.pallas_call(
        _mlstm_kernel,
        grid=(batch, heads),
        in_specs=[smem, smem,
                  pl.BlockSpec((seq, ML_QK), lambda b, h: (b, h)),
                  pl.BlockSpec((seq, ML_QK), lambda b, h: (b, heads + h)),
                  pl.BlockSpec((seq, ML_V), lambda b, h: (b, vblk + h)),
                  pl.BlockSpec((seq, ML_V), lambda b, h: (b, vblk + heads + h)),
                  pl.BlockSpec((CONV_W, ML_QK), lambda b, h: (0, h)),
                  pl.BlockSpec((CONV_W, ML_QK), lambda b, h: (0, heads + h)),
                  pl.BlockSpec((1, ML_QK), lambda b, h: (0, h)),
                  pl.BlockSpec((1, ML_QK), lambda b, h: (0, heads + h)),
                  pl.BlockSpec((seq, LANES), lambda b, h: (b, 0)),
                  pl.BlockSpec((1, ML_V), lambda b, h: (0, 0))],
        out_specs=pl.BlockSpec((seq, ML_V), lambda b, h: (b, h)),
        out_shape=jax.ShapeDtypeStruct((t, heads * ML_V), BF16),
        scratch_shapes=[pltpu.VMEM((seq, ML_QK), BF16), pltpu.VMEM((seq, ML_QK), F32),
                        pltpu.VMEM((seq, 1), F32), pltpu.VMEM((seq, 1), F32),
                        pltpu.VMEM((nc, ML_CHUNK), F32), pltpu.VMEM((nc, ML_CHUNK), F32),
                        pltpu.VMEM((LANES, seq), F32),
                        pltpu.VMEM((nc, ML_CHUNK, ML_CHUNK), F32), pltpu.VMEM((nc, ML_CHUNK, ML_CHUNK), F32),
                        pltpu.VMEM((ML_QK, ML_V + LANES), F32), pltpu.VMEM((1, 1), F32)],
        compiler_params=_cparams(("arbitrary", "arbitrary")),
        name="mlstm",
    )(b_ig, b_fg, pm, pm, pm, pm, conv_w, conv_w, conv_b.reshape(1, -1), conv_b.reshape(1, -1),
      gates, ml_norm.reshape(1, ML_V))


def _outproj_kernel(oa_ref, om_ref, w_ref, x_ref, mod_ref, g_ref, x1_ref, h2_ref):
    ka = oa_ref.shape[1]
    mixed = (jnp.dot(oa_ref[...], w_ref[:ka, :], preferred_element_type=F32)
             + jnp.dot(om_ref[...], w_ref[ka:, :], preferred_element_type=F32))
    x1 = x_ref[...] + mod_ref[0, 2:3, :] * mixed
    x1_ref[...] = x1
    h2 = _rms(x1, g_ref[...]) * (1.0 + mod_ref[0, 4:5, :]) + mod_ref[0, 3:4, :]
    h2_ref[...] = h2.astype(BF16)


def _outproj(oa, om, w_out, x2, mod3, g_ffn, seq):
    t, d = x2.shape
    tiles_per_seq = seq // OUT_TM
    return pl.pallas_call(
        _outproj_kernel,
        grid=(t // OUT_TM,),
        in_specs=[pl.BlockSpec((OUT_TM, oa.shape[1]), lambda i: (i, 0)),
                  pl.BlockSpec((OUT_TM, om.shape[1]), lambda i: (i, 0)),
                  pl.BlockSpec(w_out.shape, lambda i: (0, 0)),
                  pl.BlockSpec((OUT_TM, d), lambda i: (i, 0)),
                  pl.BlockSpec((1, 6, d), lambda i: (i // tiles_per_seq, 0, 0)),
                  pl.BlockSpec((1, d), lambda i: (0, 0))],
        out_specs=[pl.BlockSpec((OUT_TM, d), lambda i: (i, 0)),
                   pl.BlockSpec((OUT_TM, d), lambda i: (i, 0))],
        out_shape=[jax.ShapeDtypeStruct((t, d), F32), jax.ShapeDtypeStruct((t, d), BF16)],
        compiler_params=_cparams(("arbitrary",)),
        name="outproj",
    )(oa, om, w_out, x2, mod3, g_ffn.reshape(1, d))


_CAND_ROWS = 16 + 8 + 6 * 8 + 8


def _pick_max(work, iota, exact):
    mx = jnp.max(work, axis=0, keepdims=True)
    if not exact:
        return mx, work == mx
    first = jnp.min(jnp.where(work == mx, iota, float(work.shape[0])), axis=0, keepdims=True)
    return mx, iota == first


def _rank16(s, top_scr, exact, want_rank=True):
    iota = lax.broadcasted_iota(jnp.int32, s.shape, 0).astype(F32) if exact else None
    rank = jnp.full(s.shape, float(PEER_TOPK), F32) if want_rank else None
    for r in range(PEER_TOPK):
        mx, sel = _pick_max(s, iota, exact)
        if want_rank:
            rank = jnp.where(sel, float(r), rank)
        s = jnp.where(sel, NEG_INF, s)
        top_scr[r:r + 1, :] = mx
    return rank, s


def _selection_tables(s1, s2, t1_scr, t2_scr, exact):
    k = PEER_TOPK
    r1, left1 = _rank16(s1, t1_scr, exact, want_rank=exact)
    r2, _ = _rank16(s2, t2_scr, exact)
    t1 = t1_scr[...]
    t2 = t2_scr[...]
    tm = s1.shape[1]

    brow = lax.broadcasted_iota(jnp.int32, (SUBLANES, tm), 0)
    pieces = [t1[0:1] + t2, t1[1:2] + t2[0:SUBLANES]]
    for a in range(2, SUBLANES):
        pieces.append(jnp.where(brow < k // (a + 1), t1[a:a + 1] + t2[0:SUBLANES], NEG_INF))
    pieces.append(t1[SUBLANES:] + t2[0:1])
    cand = jnp.concatenate(pieces, axis=0)
    cmax = cand[0:1]

    work = cand
    iota = lax.broadcasted_iota(jnp.int32, cand.shape, 0).astype(F32) if exact else None
    for _ in range(k):
        _, sel = _pick_max(work, iota, exact)
        work = jnp.where(sel, NEG_INF, work)
    taken = jnp.where(work != cand, 1.0, 0.0)
    z = jnp.sum(taken * jnp.exp(cand - cmax), axis=0, keepdims=True)

    n_rows = [jnp.sum(taken[0:16], axis=0, keepdims=True)]
    for a in range(1, SUBLANES):
        lo = 16 + (a - 1) * SUBLANES
        n_rows.append(jnp.sum(taken[lo:lo + SUBLANES], axis=0, keepdims=True))
    base = 16 + 7 * SUBLANES
    for a in range(SUBLANES, k):
        n_rows.append(taken[base + a - SUBLANES:base + a - SUBLANES + 1])
    n1 = jnp.zeros(s1.shape, F32)
    for a in range(k):
        hit = (r1 == float(a)) if exact else (s1 == t1[a:a + 1])
        n1 = jnp.where(hit, n_rows[a], n1)

    extra = None
    if not exact:
        def count(x):
            return jnp.sum(x, axis=0, keepdims=True) - float(k)
        extra = (count(jnp.where(left1 != s1, 1.0, 0.0)) + count(jnp.where(r2 < float(k), 1.0, 0.0))
                 + count(taken))
    return (r2, jnp.exp(s2 - t2[0:1]), n1, jnp.exp(s1 - t1[0:1]) / z), extra


def _peer_sel_kernel(h2_ref, w_ref, keys_ref, u_ref, v_ref, r2_ref, e2_ref, n1_ref, e1_ref, ub_ref, vtb_ref,
                     t1_scr, t2_scr):
    ub_ref[...] = u_ref[...].astype(BF16)
    vtb_ref[...] = v_ref[...].T.astype(BF16)

    half = PEER_NKEYS
    q = jnp.dot(h2_ref[...], w_ref[...], preferred_element_type=F32).astype(BF16)
    nt = (((1,), (1,)), ((), ()))
    s1 = lax.dot_general(keys_ref[0, 0], q[:, :half], nt, preferred_element_type=F32)
    s2 = lax.dot_general(keys_ref[0, 1], q[:, half:], nt, preferred_element_type=F32)

    def store(tables):
        for ref, val in zip((r2_ref, e2_ref, n1_ref, e1_ref), tables):
            ref[0] = val.astype(ref.dtype)

    tables, extra = _selection_tables(s1, s2, t1_scr, t2_scr, exact=False)
    store(tables)

    @pl.when(jnp.max(extra) > 0.0)
    def _():
        store(_selection_tables(s1, s2, t1_scr, t2_scr, exact=True)[0])


def _peer_sel(h2, w_pq, keys, peer_u, peer_v):
    t, d = h2.shape
    ne = peer_u.shape[0]
    nh = PEER_HEADS
    qd = 2 * PEER_NKEYS
    steps = (t // SEL_TM) * nh
    rows = ne // steps
    assert rows * steps == ne and rows % LANES == 0
    tab = jax.ShapeDtypeStruct((nh, PEER_NKEYS, t), F32)
    tab16 = jax.ShapeDtypeStruct((nh, PEER_NKEYS, t), BF16)
    tab_spec = pl.BlockSpec((1, PEER_NKEYS, SEL_TM), lambda i, h: (h, 0, i))
    return pl.pallas_call(
        _peer_sel_kernel,
        grid=(t // SEL_TM, nh),
        in_specs=[pl.BlockSpec((SEL_TM, d), lambda i, h: (i, 0)),
                  pl.BlockSpec((d, qd), lambda i, h: (0, h)),
                  pl.BlockSpec((1, 2, PEER_NKEYS, PEER_NKEYS), lambda i, h: (h, 0, 0, 0)),
                  pl.BlockSpec((rows, d), lambda i, h: (i * nh + h, 0)),
                  pl.BlockSpec((rows, d), lambda i, h: (i * nh + h, 0))],
        out_specs=[tab_spec, tab_spec, tab_spec, tab_spec,
                   pl.BlockSpec((rows, d), lambda i, h: (i * nh + h, 0)),
                   pl.BlockSpec((d, rows), lambda i, h: (0, i * nh + h))],
        out_shape=[tab16, tab16, tab, tab,
                   jax.ShapeDtypeStruct((ne, d), BF16), jax.ShapeDtypeStruct((d, ne), BF16)],
        scratch_shapes=[pltpu.VMEM((PEER_TOPK, SEL_TM), F32), pltpu.VMEM((PEER_TOPK, SEL_TM), F32)],
        compiler_params=_cparams(("arbitrary", "arbitrary")),
        name="peer_sel",
    )(h2, w_pq, keys, peer_u, peer_v)


def _peer_ffn_kernel(h2_ref, u0_ref, uodd_ref, unext_ref, vt_ref, r2_ref, e2_ref, n1_ref, e1_ref, x1_ref,
                     mod_ref, modo_ref, g_ref, o_ref, acc_scr, a0_scr, a1_scr):
    c = pl.program_id(1)
    nk = PEER_NKEYS
    per = FFN_EC // nk
    nt = (((1,), (1,)), ((), ()))
    zero = jnp.zeros((), BF16)

    def preact(u_ref, dst_scr, lo, hi):
        dst_scr[lo:hi, :] = lax.dot_general(u_ref[lo:hi, :], h2_ref[...], nt, preferred_element_type=F32)

    def weights(a_scr, chunk, lo, hi):
        parts = []
        for ii in range(lo // nk, hi // nk):
            i = chunk * per + ii
            a = a_scr[ii * nk:(ii + 1) * nk, :]
            gsum = None
            for hd in range(PEER_HEADS):
                n1 = n1_ref[hd, pl.ds(i, 1), :].astype(BF16)
                e1 = e1_ref[hd, pl.ds(i, 1), :].astype(BF16)
                g = jnp.where(r2_ref[hd] < n1, e2_ref[hd], zero) * e1
                gsum = g if gsum is None else gsum + g
            act = 0.5 * a * (1.0 + lax.erf(a * (2.0 ** -0.5)))
            parts.append(gsum * act.astype(BF16))
        return jnp.concatenate(parts, axis=0)

    def accumulate(w, col0, lo, hi):
        acc_scr[...] += jnp.dot(vt_ref[:, col0 + lo:col0 + hi], w, preferred_element_type=F32)

    @pl.when(c == 0)
    def _():
        acc_scr[...] = jnp.zeros(acc_scr.shape, F32)
        preact(u0_ref, a0_scr, 0, FFN_EC)

    bounds = [(p * FFN_PIECE, (p + 1) * FFN_PIECE) for p in range(FFN_EC // FFN_PIECE)]
    pieces = ([(a0_scr, 2 * c, 0, uodd_ref, a1_scr, lo, hi) for lo, hi in bounds]
              + [(a1_scr, 2 * c + 1, FFN_EC, unext_ref, a0_scr, lo, hi) for lo, hi in bounds])
    pending = None
    for src_scr, chunk, col0, u_ref, dst_scr, lo, hi in pieces:
        w = weights(src_scr, chunk, lo, hi)
        if pending is not None:
            accumulate(*pending)
        preact(u_ref, dst_scr, lo, hi)
        pending = (w, col0, lo, hi)
    accumulate(*pending)

    @pl.when(c == pl.num_programs(1) - 1)
    def _():
        y = acc_scr[...].T
        x2 = x1_ref[...] + mod_ref[0, 5:6, :] * y
        o_ref[...] = _rms(x2, g_ref[...]) * (1.0 + modo_ref[0, 1:2, :]) + modo_ref[0, 0:1, :]


def _peer_ffn(h2, u, vt, tabs, x1, mod3, modo3, g_final, seq):
    t, d = h2.shape
    ne = u.shape[0]
    tiles_per_seq = seq // FFN_TB
    tab_spec = pl.BlockSpec((PEER_HEADS, PEER_NKEYS, FFN_TB), lambda i, c: (0, 0, i))
    nchunk = ne // FFN_EC
    once = pl.Buffered(1)
    tab_once = pl.BlockSpec((PEER_HEADS, PEER_NKEYS, FFN_TB), lambda i, c: (0, 0, i), pipeline_mode=once)
    return pl.pallas_call(
        _peer_ffn_kernel,
        grid=(t // FFN_TB, nchunk // 2),
        in_specs=[pl.BlockSpec((FFN_TB, d), lambda i, c: (i, 0)),
                  pl.BlockSpec((FFN_EC, d), lambda i, c: (0, 0), pipeline_mode=once),
                  pl.BlockSpec((FFN_EC, d), lambda i, c: (2 * c + 1, 0)),
                  pl.BlockSpec((FFN_EC, d), lambda i, c: (jnp.minimum(2 * c + 2, nchunk - 1), 0)),
                  pl.BlockSpec((d, 2 * FFN_EC), lambda i, c: (0, c)),
                  tab_spec, tab_spec, tab_once, tab_once,
                  pl.BlockSpec((FFN_TB, d), lambda i, c: (i, 0), pipeline_mode=once),
                  pl.BlockSpec((1, 6, d), lambda i, c: (i // tiles_per_seq, 0, 0)),
                  pl.BlockSpec((1, 2, d), lambda i, c: (i // tiles_per_seq, 0, 0)),
                  pl.BlockSpec((1, d), lambda i, c: (0, 0))],
        out_specs=pl.BlockSpec((FFN_TB, d), lambda i, c: (i, 0)),
        out_shape=jax.ShapeDtypeStruct((t, d), F32),
        scratch_shapes=[pltpu.VMEM((d, FFN_TB), F32), pltpu.VMEM((FFN_EC, FFN_TB), F32),
                        pltpu.VMEM((FFN_EC, FFN_TB), F32)],
        compiler_params=pltpu.CompilerParams(dimension_semantics=("arbitrary", "arbitrary"),
                                             vmem_limit_bytes=FFN_VMEM_LIMIT),
        name="peer_ffn",
    )(h2, u, u, u, vt, *tabs, x1, mod3, modo3, g_final.reshape(1, d))


def kernel(x, c, w_ada, b_ada, g_mix, w_in, conv_w, conv_b, b_igate, b_fgate, lambda_q1, lambda_k1,
           lambda_q2, lambda_k2, da_norm, ml_norm, w_out, g_ffn, w_pq, sub_keys, peer_u, peer_v,
           w_ada_final, b_ada_final, g_final):
    batch, seq, d = x.shape
    depth = w_ada.shape[0]
    t = batch * seq
    da_heads = d // 256
    ml_heads = d // 512
    att_cols = 3 * da_heads * DA_V
    rope_cols = 2 * da_heads * DA_V
    main_cols = att_cols + 2 * ml_heads * ML_QK + 2 * ml_heads * ML_V
    assert batch <= SUBLANES and seq % PROJ_TM == 0 and seq % ATT_TQ == 0 and seq % ML_CHUNK == 0
    assert w_in.shape[2] == main_cols + 2 * ml_heads

    c8 = jnp.zeros((SUBLANES, d), F32).at[:batch].set(c.astype(F32))
    modo3 = _ada(c8, w_ada_final, b_ada_final)[:batch].reshape(batch, 2, d)
    xt = x.reshape(t, d)

    for l in range(depth):
        mod3 = _ada(c8, w_ada[l], b_ada[l])[:batch].reshape(batch, 6, d)
        lam_init = 0.8 - 0.6 * math.exp(-0.3 * l)

        w_in_t = jnp.swapaxes(w_in[l], 0, 1).astype(BF16)
        w_gate_t = jnp.zeros((LANES, d), BF16).at[:2 * ml_heads].set(w_in_t[main_cols:])
        pa, pm, gates = _proj(xt, mod3, g_mix[l], w_in_t, w_gate_t, seq, main_cols, att_cols, rope_cols)

        lam4 = jnp.stack([lambda_q1[l], lambda_k1[l], lambda_q2[l], lambda_k2[l]]).astype(F32)
        oa = _attn(pa, lam4, da_norm[l], batch, seq, da_heads, lam_init)

        om = _mlstm(pm, gates, conv_w[l], conv_b[l], b_igate[l], b_fgate[l], ml_norm[l], batch, seq, ml_heads)

        x1, h2 = _outproj(oa, om, w_out[l].astype(BF16), xt, mod3, g_ffn[l], seq)

        *tabs, u_b, vt_b = _peer_sel(h2, w_pq[l].astype(BF16), sub_keys[l].astype(BF16), peer_u[l], peer_v[l])
        assert depth == 1
        xt = _peer_ffn(h2, u_b, vt_b, tabs, x1, mod3, modo3, g_final, seq)

    return xt.reshape(batch, seq, d)
```

```python
import functools
import math

import numpy as np
import jax
import jax.numpy as jnp
from jax import lax
from jax.experimental import pallas as pl
from jax.experimental.pallas import tpu as pltpu

F32 = jnp.float32
BF16 = jnp.bfloat16

DA_QK = 64
DA_V = 128
ML_QK = 128
ML_V = 256
CONV_W = 4
GATE_CAP = 15.0
ROPE_THETA = 10000.0
PEER_HEADS = 8
PEER_NKEYS = 128
PEER_TOPK = 16
EPS = 1e-6

LANES = 128
SUBLANES = 8
VMEM_LIMIT = 56 * 1024 * 1024
FFN_VMEM_LIMIT = 60 * 1024 * 1024

ADA_TN = 1024
PROJ_TM = 256
PROJ_TN = 512
ATT_TQ = 512
ATT_ONES_ROWS = 16
ATT_HEADS_PER_STEP = 2
ML_CHUNK = 256
OUT_TM = 256
SEL_TM = 512
FFN_TB = 512
FFN_EC = 512
FFN_PIECE = 512

NEG_INF = float("-inf")


def _cparams(sem):
    return pltpu.CompilerParams(dimension_semantics=sem, vmem_limit_bytes=VMEM_LIMIT)


def _rms(x, g):
    return x * lax.rsqrt(jnp.mean(x * x, axis=-1, keepdims=True) + EPS) * g


def _ada_kernel(c_ref, w_ref, b_ref, o_ref):
    c = c_ref[...]
    cs = c * jax.nn.sigmoid(c)
    o_ref[...] = jnp.dot(cs, w_ref[...], preferred_element_type=F32,
                         precision=lax.Precision.HIGHEST) + b_ref[...]


def _ada(c8, w, b):
    d, n = w.shape
    return pl.pallas_call(
        _ada_kernel,
        grid=(n // ADA_TN,),
        in_specs=[pl.BlockSpec((SUBLANES, d), lambda j: (0, 0)),
                  pl.BlockSpec((d, ADA_TN), lambda j: (0, j)),
                  pl.BlockSpec((1, ADA_TN), lambda j: (0, j))],
        out_specs=pl.BlockSpec((SUBLANES, ADA_TN), lambda j: (0, j)),
        out_shape=jax.ShapeDtypeStruct((SUBLANES, n), F32),
        compiler_params=_cparams(("arbitrary",)),
        name="ada",
    )(c8, w, b.reshape(1, n))


def _proj_kernel(n_rope, n_att, n_tiles, x_ref, mod_ref, g_ref, w_ref, wg_ref, cos_ref, sa_ref, sb_ref,
                 pa_ref, pm_ref, gate_ref):
    nt = (((1,), (1,)), ((), ()))
    tn = PROJ_TN
    y = _rms(x_ref[...], g_ref[...])
    hb = (y * (1.0 + mod_ref[0, 1:2, :]) + mod_ref[0, 0:1, :]).astype(BF16)
    gate_ref[...] = lax.dot_general(hb, wg_ref[...], nt, preferred_element_type=F32)
    cos, sa, sb = cos_ref[...], sa_ref[...], sb_ref[...]
    for j in range(n_tiles):
        acc = lax.dot_general(hb, w_ref[j * tn:(j + 1) * tn, :], nt, preferred_element_type=F32)
        if j < n_rope:
            for g in range(tn // LANES):
                xg = acc[:, g * LANES:(g + 1) * LANES]
                r = (xg * cos + pltpu.roll(xg, LANES - DA_QK // 2, 1) * sa
                     + pltpu.roll(xg, DA_QK // 2, 1) * sb)
                pa_ref[:, j * tn + g * LANES:j * tn + (g + 1) * LANES] = r.astype(BF16)
        elif j < n_att:
            pa_ref[:, j * tn:(j + 1) * tn] = acc.astype(BF16)
        else:
            pm_ref[:, (j - n_att) * tn:(j - n_att + 1) * tn] = acc


def _rope_tables(seq):
    half = DA_QK // 2
    inv = ROPE_THETA ** (-jnp.arange(half, dtype=F32) / half)
    ang = jnp.arange(seq, dtype=F32)[:, None] * inv[None, :]
    lane = np.arange(LANES)
    first = (lane % DA_QK) < half
    cos = jnp.cos(ang)[:, lane % half]
    sin = jnp.sin(ang)[:, lane % half]
    sa = jnp.where(first[None, :], -sin, 0.0)
    sb = jnp.where(first[None, :], 0.0, sin)
    return cos, sa, sb


def _proj(x2, mod3, g, w_main_t, w_gate_t, seq, n, n_att_cols, n_rope_cols):
    t, d = x2.shape
    assert n % PROJ_TN == 0 and n <= w_main_t.shape[0]
    n_att = n_att_cols // PROJ_TN
    n_rope = n_rope_cols // PROJ_TN
    n_tiles = n // PROJ_TN
    tiles_per_seq = seq // PROJ_TM
    cos, sa, sb = _rope_tables(seq)
    tab_spec = pl.BlockSpec((PROJ_TM, LANES), lambda i: (i % tiles_per_seq, 0))
    once = pl.Buffered(1)
    return pl.pallas_call(
        functools.partial(_proj_kernel, n_rope, n_att, n_tiles),
        grid=(t // PROJ_TM,),
        in_specs=[pl.BlockSpec((PROJ_TM, d), lambda i: (i, 0)),
                  pl.BlockSpec((1, 6, d), lambda i: (i // tiles_per_seq, 0, 0)),
                  pl.BlockSpec((1, d), lambda i: (0, 0)),
                  pl.BlockSpec(w_main_t.shape, lambda i: (0, 0), pipeline_mode=once),
                  pl.BlockSpec((LANES, d), lambda i: (0, 0), pipeline_mode=once),
                  tab_spec, tab_spec, tab_spec],
        out_specs=[pl.BlockSpec((PROJ_TM, n_att_cols), lambda i: (i, 0)),
                   pl.BlockSpec((PROJ_TM, n - n_att_cols), lambda i: (i, 0)),
                   pl.BlockSpec((PROJ_TM, LANES), lambda i: (i, 0))],
        out_shape=[jax.ShapeDtypeStruct((t, n_att_cols), BF16),
                   jax.ShapeDtypeStruct((t, n - n_att_cols), F32),
                   jax.ShapeDtypeStruct((t, LANES), F32)],
        compiler_params=_cparams(("arbitrary",)),
        name="proj",
    )(x2, mod3, g.reshape(1, d), w_main_t, w_gate_t, cos, sa, sb)


def _attn_kernel(lam_init, lam_ref, q_ref, k_ref, v_ref, nrm_ref, o_ref, vt_scr):
    i = pl.program_id(2)
    tq = ATT_TQ
    nhs = ATT_HEADS_PER_STEP
    seq = k_ref.shape[0]
    nq = seq // tq

    @pl.when(i == 0)
    def _():
        for hh in range(nhs):
            vt_scr[hh, :DA_V, :] = v_ref[:, hh * DA_V:(hh + 1) * DA_V].astype(F32).T.astype(BF16)
            vt_scr[hh, DA_V:, :] = jnp.ones((ATT_ONES_ROWS, seq), BF16)

    lv = lam_ref[...]
    lam = (jnp.exp(jnp.sum(lv[0:1] * lv[1:2], axis=-1, keepdims=True))
           - jnp.exp(jnp.sum(lv[2:3] * lv[3:4], axis=-1, keepdims=True)) + lam_init)
    lane = lax.broadcasted_iota(jnp.int32, (tq, LANES), 1)
    nt = (((1,), (1,)), ((), ()))

    def block(qi):
        below = qi * tq
        kpos = lax.broadcasted_iota(jnp.int32, (tq, 2 * tq), 0)
        col = lax.broadcasted_iota(jnp.int32, (tq, 2 * tq), 1)
        causal = kpos <= jnp.where(col >= tq, col - tq, col)
        for hh in range(nhs):
            hl = slice(hh * LANES, (hh + 1) * LANES)
            q = q_ref[:, hl] * (DA_QK ** -0.5)
            zero = jnp.zeros_like(q)
            q2 = jnp.concatenate([jnp.where(lane < DA_QK, q, zero), jnp.where(lane >= DA_QK, q, zero)], axis=0)
            s_diag = lax.dot_general(k_ref[below:below + tq, hl], q2, nt, preferred_element_type=F32)
            s_diag = jnp.where(causal, s_diag, NEG_INF)
            m = jnp.max(s_diag, axis=0, keepdims=True)
            if qi > 0:
                s_low = lax.dot_general(k_ref[0:below, hl], q2, nt, preferred_element_type=F32)
                m = jnp.maximum(m, jnp.max(s_low, axis=0, keepdims=True))
            acc = jnp.dot(vt_scr[hh, :, below:below + tq], jnp.exp(s_diag - m).astype(BF16),
                          preferred_element_type=F32)
            if qi > 0:
                acc = acc + jnp.dot(vt_scr[hh, :, 0:below], jnp.exp(s_low - m).astype(BF16),
                                    preferred_element_type=F32)
            o2 = acc[:DA_V, :] / acc[DA_V:DA_V + 1, :]
            o = (o2[:, :tq] - lam * o2[:, tq:]).T
            o_ref[:, hh * DA_V:(hh + 1) * DA_V] = (_rms(o, nrm_ref[...]) * (1.0 - lam_init)).astype(o_ref.dtype)

    for qi in range(nq):
        pl.when(i == qi)(functools.partial(block, qi))


def _attn(pa, lam4, da_norm, batch, seq, heads, lam_init):
    t = pa.shape[0]
    nq = seq // ATT_TQ
    nhs = ATT_HEADS_PER_STEP
    groups = heads // nhs
    wide = nhs * LANES
    return pl.pallas_call(
        functools.partial(_attn_kernel, lam_init),
        grid=(batch, groups, nq),
        in_specs=[pl.BlockSpec((4, DA_QK), lambda b, g, i: (0, 0)),
                  pl.BlockSpec((ATT_TQ, wide), lambda b, g, i: (b * nq + i, g)),
                  pl.BlockSpec((seq, wide), lambda b, g, i: (b, groups + g)),
                  pl.BlockSpec((seq, wide), lambda b, g, i: (b, 2 * groups + g)),
                  pl.BlockSpec((1, DA_V), lambda b, g, i: (0, 0))],
        out_specs=pl.BlockSpec((ATT_TQ, wide), lambda b, g, i: (b * nq + i, g)),
        out_shape=jax.ShapeDtypeStruct((t, heads * DA_V), BF16),
        scratch_shapes=[pltpu.VMEM((nhs, DA_V + ATT_ONES_ROWS, seq), BF16)],
        compiler_params=_cparams(("arbitrary", "arbitrary", "arbitrary")),
        name="attn",
    )(lam4, pa, pa, pa, da_norm.reshape(1, DA_V))


def _softcap(x):
    return GATE_CAP * jnp.tanh(x / GATE_CAP)


def _dwconv_silu(x, w, b):
    seq = x.shape[0]
    row = lax.broadcasted_iota(jnp.int32, x.shape, 0)
    y = x * w[CONV_W - 1:CONV_W]
    for s in range(1, CONV_W):
        xs = jnp.where(row >= s, pltpu.roll(x, s, 0), 0.0)
        y = y + xs * w[CONV_W - 1 - s:CONV_W - s]
    y = y + b
    return y * jax.nn.sigmoid(y)


def _mlstm_kernel(bi_ref, bf_ref, q_ref, k_ref, v_ref, og_ref, cwq_ref, cwk_ref, cbq_ref, cbk_ref,
                  gc_ref, nrm_ref, o_ref,
                  q_scr, k_scr, lic_scr, lfc_scr, lir_scr, lfr_scr, gt_scr, bc_scr, br_scr, c_scr, m_scr):
    hd = pl.program_id(1)
    L = ML_CHUNK
    seq = q_ref.shape[0]
    nc = seq // L
    bi = bi_ref[hd]
    bf = bf_ref[hd]

    q_scr[...] = _dwconv_silu(q_ref[...], cwq_ref[...], cbq_ref[...]).astype(BF16)
    k_scr[...] = _dwconv_silu(k_ref[...], cwk_ref[...], cbk_ref[...]) * (ML_QK ** -0.5)

    gc = gc_ref[...]
    glane = lax.broadcasted_iota(jnp.int32, gc.shape, 1)
    gi = jnp.sum(jnp.where(glane == hd, gc, 0.0), axis=1, keepdims=True)
    gf = jnp.sum(jnp.where(glane == pl.num_programs(1) + hd, gc, 0.0), axis=1, keepdims=True)
    lic_scr[...] = _softcap(gi + bi)
    lfc_scr[...] = jax.nn.log_sigmoid(_softcap(gf + bf))
    gt_scr[...] = gc.T
    li_row = _softcap(gt_scr[pl.ds(hd, 1), :] + bi)
    lf_row = jax.nn.log_sigmoid(_softcap(gt_scr[pl.ds(pl.num_programs(1) + hd, 1), :] + bf))
    for c in range(nc):
        lir_scr[c:c + 1, :] = li_row[:, c * L:(c + 1) * L]
        lfr_scr[c:c + 1, :] = lf_row[:, c * L:(c + 1) * L]

    c_scr[...] = jnp.zeros(c_scr.shape, F32)
    m_scr[...] = jnp.zeros(m_scr.shape, F32)

    r_i = lax.broadcasted_iota(jnp.int32, (L, L), 0)
    c_i = lax.broadcasted_iota(jnp.int32, (L, L), 1)
    tril = (c_i <= r_i)
    tril_b = tril.astype(BF16)
    triu_b = (r_i <= c_i).astype(BF16)
    ones_col = (lax.broadcasted_iota(jnp.int32, (L, LANES), 1) == 0).astype(BF16)
    nrm = nrm_ref[...]

    def split3(x):
        hi = x.astype(BF16)
        r = x - hi.astype(F32)
        mid = r.astype(BF16)
        return hi, mid, (r - mid.astype(F32)).astype(BF16)

    for c in range(nc):
        b_c = None
        for piece in split3(lfc_scr[c * L:(c + 1) * L, :]):
            term = jnp.dot(tril_b, jnp.broadcast_to(piece, (L, L)), preferred_element_type=F32)
            b_c = term if b_c is None else b_c + term
        b_r = None
        for piece in split3(lfr_scr[c:c + 1, :]):
            term = jnp.dot(jnp.broadcast_to(piece, (L, L)), triu_b, preferred_element_type=F32)
            b_r = term if b_r is None else b_r + term
        bc_scr[c] = b_c
        br_scr[c] = b_r

    def body(c):
        sl = slice(c * L, (c + 1) * L)
        qc = q_scr[sl, :]
        kc = k_scr[sl, :]
        vc = v_ref[sl, :].astype(BF16)
        lic = lic_scr[sl, :]
        lir = lir_scr[c:c + 1, :]
        m = m_scr[...]

        b_c = bc_scr[c]
        b_r = br_scr[c]
        dmat = jnp.where(tril, b_c - b_r + lir, NEG_INF)
        bcol = b_c[:, 0:1]
        inter = bcol + m
        m_j = jnp.maximum(inter, jnp.max(dmat, axis=-1, keepdims=True))
        w_intra = jnp.exp(dmat - m_j)
        w_inter = jnp.exp(inter - m_j)

        sqk = lax.dot_general(qc, kc.astype(BF16), (((1,), (1,)), ((), ())), preferred_element_type=F32)
        qk = sqk * w_intra
        q_c = jnp.dot(qc, c_scr[...].astype(BF16), preferred_element_type=F32)
        num = w_inter * q_c[:, :ML_V] + jnp.dot(qk.astype(BF16), vc, preferred_element_type=F32)
        den = w_inter * q_c[:, ML_V:ML_V + 1] + jnp.sum(qk, axis=-1, keepdims=True)
        hh = num / jnp.maximum(jnp.abs(den), jnp.exp(-m_j))
        og = og_ref[sl, :]
        o_ref[sl, :] = (_rms(hh, nrm) * jax.nn.sigmoid(og)).astype(o_ref.dtype)

        b_l = b_c[L - 1:L, 0:1]
        logw = b_l - bcol + lic
        m_new = jnp.maximum(b_l + m, jnp.max(logw, axis=0, keepdims=True))
        decay = jnp.exp(b_l + m - m_new)
        ws = jnp.exp(logw - m_new)
        kw = (kc * ws).astype(BF16)
        vext = jnp.concatenate([vc, ones_col], axis=1)
        upd = lax.dot_general(kw, vext, (((0,), (0,)), ((), ())), preferred_element_type=F32)
        c_scr[...] = decay * c_scr[...] + upd
        m_scr[...] = m_new

    for c in range(nc):
        body(c)


def _mlstm(pm, gates, conv_w, conv_b, b_ig, b_fg, ml_norm, batch, seq, heads):
    t = pm.shape[0]
    nc = seq // ML_CHUNK
    qk_w = heads * ML_QK
    vblk = 2 * qk_w // ML_V
    smem = pl.BlockSpec(memory_space=pltpu.SMEM)
    return pl.pallas_call(
        _mlstm_kernel,
        grid=(batch, heads),
        in_specs=[smem, smem,
                  pl.BlockSpec((seq, ML_QK), lambda b, h: (b, h)),
                  pl.BlockSpec((seq, ML_QK), lambda b, h: (b, heads + h)),
                  pl.BlockSpec((seq, ML_V), lambda b, h: (b, vblk + h)),
                  pl.BlockSpec((seq, ML_V), lambda b, h: (b, vblk + heads + h)),
                  pl.BlockSpec((CONV_W, ML_QK), lambda b, h: (0, h)),
                  pl.BlockSpec((CONV_W, ML_QK), lambda b, h: (0, heads + h)),
                  pl.BlockSpec((1, ML_QK), lambda b, h: (0, h)),
                  pl.BlockSpec((1, ML_QK), lambda b, h: (0, heads + h)),
                  pl.BlockSpec((seq, LANES), lambda b, h: (b, 0)),
                  pl.BlockSpec((1, ML_V), lambda b, h: (0, 0))],
        out_specs=pl.BlockSpec((seq, ML_V), lambda b, h: (b, h)),
        out_shape=jax.ShapeDtypeStruct((t, heads * ML_V), BF16),
        scratch_shapes=[pltpu.VMEM((seq, ML_QK), BF16), pltpu.VMEM((seq, ML_QK), F32),
                        pltpu.VMEM((seq, 1), F32), pltpu.VMEM((seq, 1), F32),
                        pltpu.VMEM((nc, ML_CHUNK), F32), pltpu.VMEM((nc, ML_CHUNK), F32),
                        pltpu.VMEM((LANES, seq), F32),
                        pltpu.VMEM((nc, ML_CHUNK, ML_CHUNK), F32), pltpu.VMEM((nc, ML_CHUNK, ML_CHUNK), F32),
                        pltpu.VMEM((ML_QK, ML_V + LANES), F32), pltpu.VMEM((1, 1), F32)],
        compiler_params=_cparams(("arbitrary", "arbitrary")),
        name="mlstm",
    )(b_ig, b_fg, pm, pm, pm, pm, conv_w, conv_w, conv_b.reshape(1, -1), conv_b.reshape(1, -1),
      gates, ml_norm.reshape(1, ML_V))


def _outproj_kernel(oa_ref, om_ref, w_ref, x_ref, mod_ref, g_ref, x1_ref, h2_ref):
    ka = oa_ref.shape[1]
    mixed = (jnp.dot(oa_ref[...], w_ref[:ka, :], preferred_element_type=F32)
             + jnp.dot(om_ref[...], w_ref[ka:, :], preferred_element_type=F32))
    x1 = x_ref[...] + mod_ref[0, 2:3, :] * mixed
    x1_ref[...] = x1
    h2 = _rms(x1, g_ref[...]) * (1.0 + mod_ref[0, 4:5, :]) + mod_ref[0, 3:4, :]
    h2_ref[...] = h2.astype(BF16)


def _outproj(oa, om, w_out, x2, mod3, g_ffn, seq):
    t, d = x2.shape
    tiles_per_seq = seq // OUT_TM
    return pl.pallas_call(
        _outproj_kernel,
        grid=(t // OUT_TM,),
        in_specs=[pl.BlockSpec((OUT_TM, oa.shape[1]), lambda i: (i, 0)),
                  pl.BlockSpec((OUT_TM, om.shape[1]), lambda i: (i, 0)),
                  pl.BlockSpec(w_out.shape, lambda i: (0, 0)),
                  pl.BlockSpec((OUT_TM, d), lambda i: (i, 0)),
                  pl.BlockSpec((1, 6, d), lambda i: (i // tiles_per_seq, 0, 0)),
                  pl.BlockSpec((1, d), lambda i: (0, 0))],
        out_specs=[pl.BlockSpec((OUT_TM, d), lambda i: (i, 0)),
                   pl.BlockSpec((OUT_TM, d), lambda i: (i, 0))],
        out_shape=[jax.ShapeDtypeStruct((t, d), F32), jax.ShapeDtypeStruct((t, d), BF16)],
        compiler_params=_cparams(("arbitrary",)),
        name="outproj",
    )(oa, om, w_out, x2, mod3, g_ffn.reshape(1, d))


_CAND_ROWS = 16 + 8 + 6 * 8 + 8


def _pick_max(work, iota, exact):
    mx = jnp.max(work, axis=0, keepdims=True)
    if not exact:
        return mx, work == mx
    first = jnp.min(jnp.where(work == mx, iota, float(work.shape[0])), axis=0, keepdims=True)
    return mx, iota == first


def _rank16(s, top_scr, exact, want_rank=True):
    iota = lax.broadcasted_iota(jnp.int32, s.shape, 0).astype(F32) if exact else None
    rank = jnp.full(s.shape, float(PEER_TOPK), F32) if want_rank else None
    for r in range(PEER_TOPK):
        mx, sel = _pick_max(s, iota, exact)
        if want_rank:
            rank = jnp.where(sel, float(r), rank)
        s = jnp.where(sel, NEG_INF, s)
        top_scr[r:r + 1, :] = mx
    return rank, s


def _selection_tables(s1, s2, t1_scr, t2_scr, exact):
    k = PEER_TOPK
    r1, left1 = _rank16(s1, t1_scr, exact, want_rank=exact)
    r2, _ = _rank16(s2, t2_scr, exact)
    t1 = t1_scr[...]
    t2 = t2_scr[...]
    tm = s1.shape[1]

    brow = lax.broadcasted_iota(jnp.int32, (SUBLANES, tm), 0)
    pieces = [t1[0:1] + t2, t1[1:2] + t2[0:SUBLANES]]
    for a in range(2, SUBLANES):
        pieces.append(jnp.where(brow < k // (a + 1), t1[a:a + 1] + t2[0:SUBLANES], NEG_INF))
    pieces.append(t1[SUBLANES:] + t2[0:1])
    cand = jnp.concatenate(pieces, axis=0)
    cmax = cand[0:1]

    work = cand
    iota = lax.broadcasted_iota(jnp.int32, cand.shape, 0).astype(F32) if exact else None
    for _ in range(k):
        _, sel = _pick_max(work, iota, exact)
        work = jnp.where(sel, NEG_INF, work)
    taken = jnp.where(work != cand, 1.0, 0.0)
    z = jnp.sum(taken * jnp.exp(cand - cmax), axis=0, keepdims=True)

    n_rows = [jnp.sum(taken[0:16], axis=0, keepdims=True)]
    for a in range(1, SUBLANES):
        lo = 16 + (a - 1) * SUBLANES
        n_rows.append(jnp.sum(taken[lo:lo + SUBLANES], axis=0, keepdims=True))
    base = 16 + 7 * SUBLANES
    for a in range(SUBLANES, k):
        n_rows.append(taken[base + a - SUBLANES:base + a - SUBLANES + 1])
    n1 = jnp.zeros(s1.shape, F32)
    for a in range(k):
        hit = (r1 == float(a)) if exact else (s1 == t1[a:a + 1])
        n1 = jnp.where(hit, n_rows[a], n1)

    extra = None
    if not exact:
        def count(x):
            return jnp.sum(x, axis=0, keepdims=True) - float(k)
        extra = (count(jnp.where(left1 != s1, 1.0, 0.0)) + count(jnp.where(r2 < float(k), 1.0, 0.0))
                 + count(taken))
    return (r2, jnp.exp(s2 - t2[0:1]), n1, jnp.exp(s1 - t1[0:1]) / z), extra


def _peer_sel_kernel(h2_ref, w_ref, keys_ref, u_ref, v_ref, r2_ref, e2_ref, n1_ref, e1_ref, ub_ref, vtb_ref,
                     t1_scr, t2_scr):
    ub_ref[...] = u_ref[...].astype(BF16)
    vtb_ref[...] = v_ref[...].T.astype(BF16)

    half = PEER_NKEYS
    q = jnp.dot(h2_ref[...], w_ref[...], preferred_element_type=F32).astype(BF16)
    nt = (((1,), (1,)), ((), ()))
    s1 = lax.dot_general(keys_ref[0, 0], q[:, :half], nt, preferred_element_type=F32)
    s2 = lax.dot_general(keys_ref[0, 1], q[:, half:], nt, preferred_element_type=F32)

    def store(tables):
        for ref, val in zip((r2_ref, e2_ref, n1_ref, e1_ref), tables):
            ref[0] = val.astype(ref.dtype)

    tables, extra = _selection_tables(s1, s2, t1_scr, t2_scr, exact=False)
    store(tables)

    @pl.when(jnp.max(extra) > 0.0)
    def _():
        store(_selection_tables(s1, s2, t1_scr, t2_scr, exact=True)[0])


def _peer_sel(h2, w_pq, keys, peer_u, peer_v):
    t, d = h2.shape
    ne = peer_u.shape[0]
    nh = PEER_HEADS
    qd = 2 * PEER_NKEYS
    steps = (t // SEL_TM) * nh
    rows = ne // steps
    assert rows * steps == ne and rows % LANES == 0
    tab = jax.ShapeDtypeStruct((nh, PEER_NKEYS, t), F32)
    tab16 = jax.ShapeDtypeStruct((nh, PEER_NKEYS, t), BF16)
    tab_spec = pl.BlockSpec((1, PEER_NKEYS, SEL_TM), lambda i, h: (h, 0, i))
    return pl.pallas_call(
        _peer_sel_kernel,
        grid=(t // SEL_TM, nh),
        in_specs=[pl.BlockSpec((SEL_TM, d), lambda i, h: (i, 0)),
                  pl.BlockSpec((d, qd), lambda i, h: (0, h)),
                  pl.BlockSpec((1, 2, PEER_NKEYS, PEER_NKEYS), lambda i, h: (h, 0, 0, 0)),
                  pl.BlockSpec((rows, d), lambda i, h: (i * nh + h, 0)),
                  pl.BlockSpec((rows, d), lambda i, h: (i * nh + h, 0))],
        out_specs=[tab_spec, tab_spec, tab_spec, tab_spec,
                   pl.BlockSpec((rows, d), lambda i, h: (i * nh + h, 0)),
                   pl.BlockSpec((d, rows), lambda i, h: (0, i * nh + h))],
        out_shape=[tab16, tab16, tab, tab,
                   jax.ShapeDtypeStruct((ne, d), BF16), jax.ShapeDtypeStruct((d, ne), BF16)],
        scratch_shapes=[pltpu.VMEM((PEER_TOPK, SEL_TM), F32), pltpu.VMEM((PEER_TOPK, SEL_TM), F32)],
        compiler_params=_cparams(("arbitrary", "arbitrary")),
        name="peer_sel",
    )(h2, w_pq, keys, peer_u, peer_v)


def _peer_ffn_kernel(h2_ref, u0_ref, uodd_ref, unext_ref, vt_ref, r2_ref, e2_ref, n1_ref, e1_ref, x1_ref,
                     mod_ref, modo_ref, g_ref, o_ref, acc_scr, a0_scr, a1_scr):
    c = pl.program_id(1)
    nk = PEER_NKEYS
    per = FFN_EC // nk
    nt = (((1,), (1,)), ((), ()))
    zero = jnp.zeros((), BF16)

    def preact(u_ref, dst_scr, lo, hi):
        dst_scr[lo:hi, :] = lax.dot_general(u_ref[lo:hi, :], h2_ref[...], nt, preferred_element_type=F32)

    def weights(a_scr, chunk, lo, hi):
        parts = []
        for ii in range(lo // nk, hi // nk):
            i = chunk * per + ii
            a = a_scr[ii * nk:(ii + 1) * nk, :]
            gsum = None
            for hd in range(PEER_HEADS):
                n1 = n1_ref[hd, pl.ds(i, 1), :].astype(BF16)
                e1 = e1_ref[hd, pl.ds(i, 1), :].astype(BF16)
                g = jnp.where(r2_ref[hd] < n1, e2_ref[hd], zero) * e1
                gsum = g if gsum is None else gsum + g
            act = 0.5 * a * (1.0 + lax.erf(a * (2.0 ** -0.5)))
            parts.append(gsum * act.astype(BF16))
        return jnp.concatenate(parts, axis=0)

    def accumulate(w, col0, lo, hi):
        acc_scr[...] += jnp.dot(vt_ref[:, col0 + lo:col0 + hi], w, preferred_element_type=F32)

    @pl.when(c == 0)
    def _():
        acc_scr[...] = jnp.zeros(acc_scr.shape, F32)
        preact(u0_ref, a0_scr, 0, FFN_EC)

    bounds = [(p * FFN_PIECE, (p + 1) * FFN_PIECE) for p in range(FFN_EC // FFN_PIECE)]
    pieces = ([(a0_scr, 2 * c, 0, uodd_ref, a1_scr, lo, hi) for lo, hi in bounds]
              + [(a1_scr, 2 * c + 1, FFN_EC, unext_ref, a0_scr, lo, hi) for lo, hi in bounds])
    pending = None
    for src_scr, chunk, col0, u_ref, dst_scr, lo, hi in pieces:
        w = weights(src_scr, chunk, lo, hi)
        if pending is not None:
            accumulate(*pending)
        preact(u_ref, dst_scr, lo, hi)
        pending = (w, col0, lo, hi)
    accumulate(*pending)

    @pl.when(c == pl.num_programs(1) - 1)
    def _():
        y = acc_scr[...].T
        x2 = x1_ref[...] + mod_ref[0, 5:6, :] * y
        o_ref[...] = _rms(x2, g_ref[...]) * (1.0 + modo_ref[0, 1:2, :]) + modo_ref[0, 0:1, :]


def _peer_ffn(h2, u, vt, tabs, x1, mod3, modo3, g_final, seq):
    t, d = h2.shape
    ne = u.shape[0]
    tiles_per_seq = seq // FFN_TB
    tab_spec = pl.BlockSpec((PEER_HEADS, PEER_NKEYS, FFN_TB), lambda i, c: (0, 0, i))
    nchunk = ne // FFN_EC
    once = pl.Buffered(1)
    tab_once = pl.BlockSpec((PEER_HEADS, PEER_NKEYS, FFN_TB), lambda i, c: (0, 0, i), pipeline_mode=once)
    return pl.pallas_call(
        _peer_ffn_kernel,
        grid=(t // FFN_TB, nchunk // 2),
        in_specs=[pl.BlockSpec((FFN_TB, d), lambda i, c: (i, 0)),
                  pl.BlockSpec((FFN_EC, d), lambda i, c: (0, 0), pipeline_mode=once),
                  pl.BlockSpec((FFN_EC, d), lambda i, c: (2 * c + 1, 0)),
                  pl.BlockSpec((FFN_EC, d), lambda i, c: (jnp.minimum(2 * c + 2, nchunk - 1), 0)),
                  pl.BlockSpec((d, 2 * FFN_EC), lambda i, c: (0, c)),
                  tab_spec, tab_spec, tab_once, tab_once,
                  pl.BlockSpec((FFN_TB, d), lambda i, c: (i, 0), pipeline_mode=once),
                  pl.BlockSpec((1, 6, d), lambda i, c: (i // tiles_per_seq, 0, 0)),
                  pl.BlockSpec((1, 2, d), lambda i, c: (i // tiles_per_seq, 0, 0)),
                  pl.BlockSpec((1, d), lambda i, c: (0, 0))],
        out_specs=pl.BlockSpec((FFN_TB, d), lambda i, c: (i, 0)),
        out_shape=jax.ShapeDtypeStruct((t, d), F32),
        scratch_shapes=[pltpu.VMEM((d, FFN_TB), F32), pltpu.VMEM((FFN_EC, FFN_TB), F32),
                        pltpu.VMEM((FFN_EC, FFN_TB), F32)],
        compiler_params=pltpu.CompilerParams(dimension_semantics=("arbitrary", "arbitrary"),
                                             vmem_limit_bytes=FFN_VMEM_LIMIT),
        name="peer_ffn",
    )(h2, u, u, u, vt, *tabs, x1, mod3, modo3, g_final.reshape(1, d))


def kernel(x, c, w_ada, b_ada, g_mix, w_in, conv_w, conv_b, b_igate, b_fgate, lambda_q1, lambda_k1,
           lambda_q2, lambda_k2, da_norm, ml_norm, w_out, g_ffn, w_pq, sub_keys, peer_u, peer_v,
           w_ada_final, b_ada_final, g_final):
    batch, seq, d = x.shape
    depth = w_ada.shape[0]
    t = batch * seq
    da_heads = d // 256
    ml_heads = d // 512
    att_cols = 3 * da_heads * DA_V
    rope_cols = 2 * da_heads * DA_V
    main_cols = att_cols + 2 * ml_heads * ML_QK + 2 * ml_heads * ML_V
    assert batch <= SUBLANES and seq % PROJ_TM == 0 and seq % ATT_TQ == 0 and seq % ML_CHUNK == 0
    assert w_in.shape[2] == main_cols + 2 * ml_heads

    c8 = jnp.zeros((SUBLANES, d), F32).at[:batch].set(c.astype(F32))
    modo3 = _ada(c8, w_ada_final, b_ada_final)[:batch].reshape(batch, 2, d)
    xt = x.reshape(t, d)

    for l in range(depth):
        mod3 = _ada(c8, w_ada[l], b_ada[l])[:batch].reshape(batch, 6, d)
        lam_init = 0.8 - 0.6 * math.exp(-0.3 * l)

        w_in_t = jnp.swapaxes(w_in[l], 0, 1).astype(BF16)
        w_gate_t = jnp.zeros((LANES, d), BF16).at[:2 * ml_heads].set(w_in_t[main_cols:])
        pa, pm, gates = _proj(xt, mod3, g_mix[l], w_in_t, w_gate_t, seq, main_cols, att_cols, rope_cols)

        lam4 = jnp.stack([lambda_q1[l], lambda_k1[l], lambda_q2[l], lambda_k2[l]]).astype(F32)
        oa = _attn(pa, lam4, da_norm[l], batch, seq, da_heads, lam_init)

        om = _mlstm(pm, gates, conv_w[l], conv_b[l], b_igate[l], b_fgate[l], ml_norm[l], batch, seq, ml_heads)

        x1, h2 = _outproj(oa, om, w_out[l].astype(BF16), xt, mod3, g_ffn[l], seq)

        *tabs, u_b, vt_b = _peer_sel(h2, w_pq[l].astype(BF16), sub_keys[l].astype(BF16), peer_u[l], peer_v[l])
        assert depth == 1
        xt = _peer_ffn(h2, u_b, vt_b, tabs, x1, mod3, modo3, g_final, seq)

    return xt.reshape(batch, seq, d)
```

```python
import functools
import math

import numpy as np
import jax
import jax.numpy as jnp
from jax import lax
from jax.experimental import pallas as pl
from jax.experimental.pallas import tpu as pltpu

F32 = jnp.float32
BF16 = jnp.bfloat16

DA_QK = 64
DA_V = 128
ML_QK = 128
ML_V = 256
CONV_W = 4
GATE_CAP = 15.0
ROPE_THETA = 10000.0
PEER_HEADS = 8
PEER_NKEYS = 128
PEER_TOPK = 16
EPS = 1e-6

LANES = 128
SUBLANES = 8
VMEM_LIMIT = 56 * 1024 * 1024
FFN_VMEM_LIMIT = 60 * 1024 * 1024

ADA_TN = 1024
PROJ_TM = 256
PROJ_TN = 512
ATT_TQ = 512
ATT_ONES_ROWS = 16
ATT_HEADS_PER_STEP = 2
ML_CHUNK = 256
OUT_TM = 256
SEL_TM = 512
FFN_TB = 512
FFN_EC = 512
FFN_PIECE = 512

NEG_INF = float("-inf")


def _cparams(sem):
    return pltpu.CompilerParams(dimension_semantics=sem, vmem_limit_bytes=VMEM_LIMIT)


def _rms(x, g):
    return x * lax.rsqrt(jnp.mean(x * x, axis=-1, keepdims=True) + EPS) * g


def _ada_kernel(c_ref, w_ref, b_ref, o_ref):
    c = c_ref[...]
    cs = c * jax.nn.sigmoid(c)
    o_ref[...] = jnp.dot(cs, w_ref[...], preferred_element_type=F32,
                         precision=lax.Precision.HIGHEST) + b_ref[...]


def _ada(c8, w, b):
    d, n = w.shape
    return pl.pallas_call(
        _ada_kernel,
        grid=(n // ADA_TN,),
        in_specs=[pl.BlockSpec((SUBLANES, d), lambda j: (0, 0)),
                  pl.BlockSpec((d, ADA_TN), lambda j: (0, j)),
                  pl.BlockSpec((1, ADA_TN), lambda j: (0, j))],
        out_specs=pl.BlockSpec((SUBLANES, ADA_TN), lambda j: (0, j)),
        out_shape=jax.ShapeDtypeStruct((SUBLANES, n), F32),
        compiler_params=_cparams(("arbitrary",)),
        name="ada",
    )(c8, w, b.reshape(1, n))


def _proj_kernel(n_rope, n_att, n_tiles, x_ref, mod_ref, g_ref, w_ref, wg_ref, cos_ref, sa_ref, sb_ref,
                 pa_ref, pm_ref, gate_ref):
    nt = (((1,), (1,)), ((), ()))
    tn = PROJ_TN
    y = _rms(x_ref[...], g_ref[...])
    hb = (y * (1.0 + mod_ref[0, 1:2, :]) + mod_ref[0, 0:1, :]).astype(BF16)
    gate_ref[...] = lax.dot_general(hb, wg_ref[...], nt, preferred_element_type=F32)
    cos, sa, sb = cos_ref[...], sa_ref[...], sb_ref[...]
    for j in range(n_tiles):
        acc = lax.dot_general(hb, w_ref[j * tn:(j + 1) * tn, :], nt, preferred_element_type=F32)
        if j < n_rope:
            for g in range(tn // LANES):
                xg = acc[:, g * LANES:(g + 1) * LANES]
                r = (xg * cos + pltpu.roll(xg, LANES - DA_QK // 2, 1) * sa
                     + pltpu.roll(xg, DA_QK // 2, 1) * sb)
                pa_ref[:, j * tn + g * LANES:j * tn + (g + 1) * LANES] = r.astype(BF16)
        elif j < n_att:
            pa_ref[:, j * tn:(j + 1) * tn] = acc.astype(BF16)
        else:
            pm_ref[:, (j - n_att) * tn:(j - n_att + 1) * tn] = acc


def _rope_tables(seq):
    half = DA_QK // 2
    inv = ROPE_THETA ** (-jnp.arange(half, dtype=F32) / half)
    ang = jnp.arange(seq, dtype=F32)[:, None] * inv[None, :]
    lane = np.arange(LANES)
    first = (lane % DA_QK) < half
    cos = jnp.cos(ang)[:, lane % half]
    sin = jnp.sin(ang)[:, lane % half]
    sa = jnp.where(first[None, :], -sin, 0.0)
    sb = jnp.where(first[None, :], 0.0, sin)
    return cos, sa, sb


def _proj(x2, mod3, g, w_main_t, w_gate_t, seq, n, n_att_cols, n_rope_cols):
    t, d = x2.shape
    assert n % PROJ_TN == 0 and n <= w_main_t.shape[0]
    n_att = n_att_cols // PROJ_TN
    n_rope = n_rope_cols // PROJ_TN
    n_tiles = n // PROJ_TN
    tiles_per_seq = seq // PROJ_TM
    cos, sa, sb = _rope_tables(seq)
    tab_spec = pl.BlockSpec((PROJ_TM, LANES), lambda i: (i % tiles_per_seq, 0))
    once = pl.Buffered(1)
    return pl.pallas_call(
        functools.partial(_proj_kernel, n_rope, n_att, n_tiles),
        grid=(t // PROJ_TM,),
        in_specs=[pl.BlockSpec((PROJ_TM, d), lambda i: (i, 0)),
                  pl.BlockSpec((1, 6, d), lambda i: (i // tiles_per_seq, 0, 0)),
                  pl.BlockSpec((1, d), lambda i: (0, 0)),
                  pl.BlockSpec(w_main_t.shape, lambda i: (0, 0), pipeline_mode=once),
                  pl.BlockSpec((LANES, d), lambda i: (0, 0), pipeline_mode=once),
                  tab_spec, tab_spec, tab_spec],
        out_specs=[pl.BlockSpec((PROJ_TM, n_att_cols), lambda i: (i, 0)),
                   pl.BlockSpec((PROJ_TM, n - n_att_cols), lambda i: (i, 0)),
                   pl.BlockSpec((PROJ_TM, LANES), lambda i: (i, 0))],
        out_shape=[jax.ShapeDtypeStruct((t, n_att_cols), BF16),
                   jax.ShapeDtypeStruct((t, n - n_att_cols), F32),
                   jax.ShapeDtypeStruct((t, LANES), F32)],
        compiler_params=_cparams(("arbitrary",)),
        name="proj",
    )(x2, mod3, g.reshape(1, d), w_main_t, w_gate_t, cos, sa, sb)


def _attn_kernel(lam_init, lam_ref, q_ref, k_ref, v_ref, nrm_ref, o_ref, vt_scr):
    i = pl.program_id(2)
    tq = ATT_TQ
    nhs = ATT_HEADS_PER_STEP
    seq = k_ref.shape[0]
    nq = seq // tq

    @pl.when(i == 0)
    def _():
        for hh in range(nhs):
            vt_scr[hh, :DA_V, :] = v_ref[:, hh * DA_V:(hh + 1) * DA_V].astype(F32).T.astype(BF16)
            vt_scr[hh, DA_V:, :] = jnp.ones((ATT_ONES_ROWS, seq), BF16)

    lv = lam_ref[...]
    lam = (jnp.exp(jnp.sum(lv[0:1] * lv[1:2], axis=-1, keepdims=True))
           - jnp.exp(jnp.sum(lv[2:3] * lv[3:4], axis=-1, keepdims=True)) + lam_init)
    lane = lax.broadcasted_iota(jnp.int32, (tq, LANES), 1)
    nt = (((1,), (1,)), ((), ()))

    def block(qi):
        below = qi * tq
        kpos = lax.broadcasted_iota(jnp.int32, (tq, 2 * tq), 0)
        col = lax.broadcasted_iota(jnp.int32, (tq, 2 * tq), 1)
        causal = kpos <= jnp.where(col >= tq, col - tq, col)
        for hh in range(nhs):
            hl = slice(hh * LANES, (hh + 1) * LANES)
            q = q_ref[:, hl] * (DA_QK ** -0.5)
            zero = jnp.zeros_like(q)
            q2 = jnp.concatenate([jnp.where(lane < DA_QK, q, zero), jnp.where(lane >= DA_QK, q, zero)], axis=0)
            s_diag = lax.dot_general(k_ref[below:below + tq, hl], q2, nt, preferred_element_type=F32)
            s_diag = jnp.where(causal, s_diag, NEG_INF)
            m = jnp.max(s_diag, axis=0, keepdims=True)
            if qi > 0:
                s_low = lax.dot_general(k_ref[0:below, hl], q2, nt, preferred_element_type=F32)
                m = jnp.maximum(m, jnp.max(s_low, axis=0, keepdims=True))
            acc = jnp.dot(vt_scr[hh, :, below:below + tq], jnp.exp(s_diag - m).astype(BF16),
                          preferred_element_type=F32)
            if qi > 0:
                acc = acc + jnp.dot(vt_scr[hh, :, 0:below], jnp.exp(s_low - m).astype(BF16),
                                    preferred_element_type=F32)
            o2 = acc[:DA_V, :] / acc[DA_V:DA_V + 1, :]
            o = (o2[:, :tq] - lam * o2[:, tq:]).T
            o_ref[:, hh * DA_V:(hh + 1) * DA_V] = (_rms(o, nrm_ref[...]) * (1.0 - lam_init)).astype(o_ref.dtype)

    for qi in range(nq):
        pl.when(i == qi)(functools.partial(block, qi))


def _attn(pa, lam4, da_norm, batch, seq, heads, lam_init):
    t = pa.shape[0]
    nq = seq // ATT_TQ
    nhs = ATT_HEADS_PER_STEP
    groups = heads // nhs
    wide = nhs * LANES
    return pl.pallas_call(
        functools.partial(_attn_kernel, lam_init),
        grid=(batch, groups, nq),
        in_specs=[pl.BlockSpec((4, DA_QK), lambda b, g, i: (0, 0)),
                  pl.BlockSpec((ATT_TQ, wide), lambda b, g, i: (b * nq + i, g)),
                  pl.BlockSpec((seq, wide), lambda b, g, i: (b, groups + g)),
                  pl.BlockSpec((seq, wide), lambda b, g, i: (b, 2 * groups + g)),
                  pl.BlockSpec((1, DA_V), lambda b, g, i: (0, 0))],
        out_specs=pl.BlockSpec((ATT_TQ, wide), lambda b, g, i: (b * nq + i, g)),
        out_shape=jax.ShapeDtypeStruct((t, heads * DA_V), BF16),
        scratch_shapes=[pltpu.VMEM((nhs, DA_V + ATT_ONES_ROWS, seq), BF16)],
        compiler_params=_cparams(("arbitrary", "arbitrary", "arbitrary")),
        name="attn",
    )(lam4, pa, pa, pa, da_norm.reshape(1, DA_V))


def _softcap(x):
    return GATE_CAP * jnp.tanh(x / GATE_CAP)


def _dwconv_silu(x, w, b):
    seq = x.shape[0]
    row = lax.broadcasted_iota(jnp.int32, x.shape, 0)
    y = x * w[CONV_W - 1:CONV_W]
    for s in range(1, CONV_W):
        xs = jnp.where(row >= s, pltpu.roll(x, s, 0), 0.0)
        y = y + xs * w[CONV_W - 1 - s:CONV_W - s]
    y = y + b
    return y * jax.nn.sigmoid(y)


def _mlstm_kernel(bi_ref, bf_ref, q_ref, k_ref, v_ref, og_ref, cwq_ref, cwk_ref, cbq_ref, cbk_ref,
                  gc_ref, nrm_ref, o_ref,
                  q_scr, k_scr, lic_scr, lfc_scr, lir_scr, lfr_scr, gt_scr, bc_scr, br_scr, c_scr, m_scr):
    hd = pl.program_id(1)
    L = ML_CHUNK
    seq = q_ref.shape[0]
    nc = seq // L
    bi = bi_ref[hd]
    bf = bf_ref[hd]

    q_scr[...] = _dwconv_silu(q_ref[...], cwq_ref[...], cbq_ref[...]).astype(BF16)
    k_scr[...] = _dwconv_silu(k_ref[...], cwk_ref[...], cbk_ref[...]) * (ML_QK ** -0.5)

    gc = gc_ref[...]
    glane = lax.broadcasted_iota(jnp.int32, gc.shape, 1)
    gi = jnp.sum(jnp.where(glane == hd, gc, 0.0), axis=1, keepdims=True)
    gf = jnp.sum(jnp.where(glane == pl.num_programs(1) + hd, gc, 0.0), axis=1, keepdims=True)
    lic_scr[...] = _softcap(gi + bi)
    lfc_scr[...] = jax.nn.log_sigmoid(_softcap(gf + bf))
    gt_scr[...] = gc.T
    li_row = _softcap(gt_scr[pl.ds(hd, 1), :] + bi)
    lf_row = jax.nn.log_sigmoid(_softcap(gt_scr[pl.ds(pl.num_programs(1) + hd, 1), :] + bf))
    for c in range(nc):
        lir_scr[c:c + 1, :] = li_row[:, c * L:(c + 1) * L]
        lfr_scr[c:c + 1, :] = lf_row[:, c * L:(c + 1) * L]

    c_scr[...] = jnp.zeros(c_scr.shape, F32)
    m_scr[...] = jnp.zeros(m_scr.shape, F32)

    r_i = lax.broadcasted_iota(jnp.int32, (L, L), 0)
    c_i = lax.broadcasted_iota(jnp.int32, (L, L), 1)
    tril = (c_i <= r_i)
    tril_b = tril.astype(BF16)
    triu_b = (r_i <= c_i).astype(BF16)
    ones_col = (lax.broadcasted_iota(jnp.int32, (L, LANES), 1) == 0).astype(BF16)
    nrm = nrm_ref[...]

    def split3(x):
        hi = x.astype(BF16)
        r = x - hi.astype(F32)
        mid = r.astype(BF16)
        return hi, mid, (r - mid.astype(F32)).astype(BF16)

    for c in range(nc):
        b_c = None
        for piece in split3(lfc_scr[c * L:(c + 1) * L, :]):
            term = jnp.dot(tril_b, jnp.broadcast_to(piece, (L, L)), preferred_element_type=F32)
            b_c = term if b_c is None else b_c + term
        b_r = None
        for piece in split3(lfr_scr[c:c + 1, :]):
            term = jnp.dot(jnp.broadcast_to(piece, (L, L)), triu_b, preferred_element_type=F32)
            b_r = term if b_r is None else b_r + term
        bc_scr[c] = b_c
        br_scr[c] = b_r

    def body(c):
        sl = slice(c * L, (c + 1) * L)
        qc = q_scr[sl, :]
        kc = k_scr[sl, :]
        vc = v_ref[sl, :].astype(BF16)
        lic = lic_scr[sl, :]
        lir = lir_scr[c:c + 1, :]
        m = m_scr[...]

        b_c = bc_scr[c]
        b_r = br_scr[c]
        dmat = jnp.where(tril, b_c - b_r + lir, NEG_INF)
        bcol = b_c[:, 0:1]
        inter = bcol + m
        m_j = jnp.maximum(inter, jnp.max(dmat, axis=-1, keepdims=True))
        w_intra = jnp.exp(dmat - m_j)
        w_inter = jnp.exp(inter - m_j)

        sqk = lax.dot_general(qc, kc.astype(BF16), (((1,), (1,)), ((), ())), preferred_element_type=F32)
        qk = sqk * w_intra
        q_c = jnp.dot(qc, c_scr[...].astype(BF16), preferred_element_type=F32)
        num = w_inter * q_c[:, :ML_V] + jnp.dot(qk.astype(BF16), vc, preferred_element_type=F32)
        den = w_inter * q_c[:, ML_V:ML_V + 1] + jnp.sum(qk, axis=-1, keepdims=True)
        hh = num / jnp.maximum(jnp.abs(den), jnp.exp(-m_j))
        og = og_ref[sl, :]
        o_ref[sl, :] = (_rms(hh, nrm) * jax.nn.sigmoid(og)).astype(o_ref.dtype)

        b_l = b_c[L - 1:L, 0:1]
        logw = b_l - bcol + lic
        m_new = jnp.maximum(b_l + m, jnp.max(logw, axis=0, keepdims=True))
        decay = jnp.exp(b_l + m - m_new)
        ws = jnp.exp(logw - m_new)
        kw = (kc * ws).astype(BF16)
        vext = jnp.concatenate([vc, ones_col], axis=1)
        upd = lax.dot_general(kw, vext, (((0,), (0,)), ((), ())), preferred_element_type=F32)
        c_scr[...] = decay * c_scr[...] + upd
        m_scr[...] = m_new

    for c in range(nc):
        body(c)


def _mlstm(pm, gates, conv_w, conv_b, b_ig, b_fg, ml_norm, batch, seq, heads):
    t = pm.shape[0]
    nc = seq // ML_CHUNK
    qk_w = heads * ML_QK
    vblk = 2 * qk_w // ML_V
    smem = pl.BlockSpec(memory_space=pltpu.SMEM)
    return pl.pallas_call(
        _mlstm_kernel,
        grid=(batch, heads),
        in_specs=[smem, smem,
                  pl.BlockSpec((seq, ML_QK), lambda b, h: (b, h)),
                  pl.BlockSpec((seq, ML_QK), lambda b, h: (b, heads + h)),
                  pl.BlockSpec((seq, ML_V), lambda b, h: (b, vblk + h)),
                  pl.BlockSpec((seq, ML_V), lambda b, h: (b, vblk + heads + h)),
                  pl.BlockSpec((CONV_W, ML_QK), lambda b, h: (0, h)),
                  pl.BlockSpec((CONV_W, ML_QK), lambda b, h: (0, heads + h)),
                  pl.BlockSpec((1, ML_QK), lambda b, h: (0, h)),
                  pl.BlockSpec((1, ML_QK), lambda b, h: (0, heads + h)),
                  pl.BlockSpec((seq, LANES), lambda b, h: (b, 0)),
                  pl.BlockSpec((1, ML_V), lambda b, h: (0, 0))],
        out_specs=pl.BlockSpec((seq, ML_V), lambda b, h: (b, h)),
        out_shape=jax.ShapeDtypeStruct((t, heads * ML_V), BF16),
        scratch_shapes=[pltpu.VMEM((seq, ML_QK), BF16), pltpu.VMEM((seq, ML_QK), F32),
                        pltpu.VMEM((seq, 1), F32), pltpu.VMEM((seq, 1), F32),
                        pltpu.VMEM((nc, ML_CHUNK), F32), pltpu.VMEM((nc, ML_CHUNK), F32),
                        pltpu.VMEM((LANES, seq), F32),
                        pltpu.VMEM((nc, ML_CHUNK, ML_CHUNK), F32), pltpu.VMEM((nc, ML_CHUNK, ML_CHUNK), F32),
                        pltpu.VMEM((ML_QK, ML_V + LANES), F32), pltpu.VMEM((1, 1), F32)],
        compiler_params=_cparams(("arbitrary", "arbitrary")),
        name="mlstm",
    )(b_ig, b_fg, pm, pm, pm, pm, conv_w, conv_w, conv_b.reshape(1, -1), conv_b.reshape(1, -1),
      gates, ml_norm.reshape(1, ML_V))


def _outproj_kernel(oa_ref, om_ref, w_ref, x_ref, mod_ref, g_ref, x1_ref, h2_ref):
    ka = oa_ref.shape[1]
    mixed = (jnp.dot(oa_ref[...], w_ref[:ka, :], preferred_element_type=F32)
             + jnp.dot(om_ref[...], w_ref[ka:, :], preferred_element_type=F32))
    x1 = x_ref[...] + mod_ref[0, 2:3, :] * mixed
    x1_ref[...] = x1
    h2 = _rms(x1, g_ref[...]) * (1.0 + mod_ref[0, 4:5, :]) + mod_ref[0, 3:4, :]
    h2_ref[...] = h2.astype(BF16)


def _outproj(oa, om, w_out, x2, mod3, g_ffn, seq):
    t, d = x2.shape
    tiles_per_seq = seq // OUT_TM
    return pl.pallas_call(
        _outproj_kernel,
        grid=(t // OUT_TM,),
        in_specs=[pl.BlockSpec((OUT_TM, oa.shape[1]), lambda i: (i, 0)),
                  pl.BlockSpec((OUT_TM, om.shape[1]), lambda i: (i, 0)),
                  pl.BlockSpec(w_out.shape, lambda i: (0, 0)),
                  pl.BlockSpec((OUT_TM, d), lambda i: (i, 0)),
                  pl.BlockSpec((1, 6, d), lambda i: (i // tiles_per_seq, 0, 0)),
                  pl.BlockSpec((1, d), lambda i: (0, 0))],
        out_specs=[pl.BlockSpec((OUT_TM, d), lambda i: (i, 0)),
                   pl.BlockSpec((OUT_TM, d), lambda i: (i, 0))],
        out_shape=[jax.ShapeDtypeStruct((t, d), F32), jax.ShapeDtypeStruct((t, d), BF16)],
        compiler_params=_cparams(("arbitrary",)),
        name="outproj",
    )(oa, om, w_out, x2, mod3, g_ffn.reshape(1, d))


_CAND_ROWS = 16 + 8 + 6 * 8 + 8


def _pick_max(work, iota, exact):
    mx = jnp.max(work, axis=0, keepdims=True)
    if not exact:
        return mx, work == mx
    first = jnp.min(jnp.where(work == mx, iota, float(work.shape[0])), axis=0, keepdims=True)
    return mx, iota == first


def _rank16(s, top_scr, exact, want_rank=True):
    iota = lax.broadcasted_iota(jnp.int32, s.shape, 0).astype(F32) if exact else None
    rank = jnp.full(s.shape, float(PEER_TOPK), F32) if want_rank else None
    for r in range(PEER_TOPK):
        mx, sel = _pick_max(s, iota, exact)
        if want_rank:
            rank = jnp.where(sel, float(r), rank)
        s = jnp.where(sel, NEG_INF, s)
        top_scr[r:r + 1, :] = mx
    return rank, s


def _selection_tables(s1, s2, t1_scr, t2_scr, exact):
    k = PEER_TOPK
    r1, left1 = _rank16(s1, t1_scr, exact, want_rank=exact)
    r2, _ = _rank16(s2, t2_scr, exact)
    t1 = t1_scr[...]
    t2 = t2_scr[...]
    tm = s1.shape[1]

    brow = lax.broadcasted_iota(jnp.int32, (SUBLANES, tm), 0)
    pieces = [t1[0:1] + t2, t1[1:2] + t2[0:SUBLANES]]
    for a in range(2, SUBLANES):
        pieces.append(jnp.where(brow < k // (a + 1), t1[a:a + 1] + t2[0:SUBLANES], NEG_INF))
    pieces.append(t1[SUBLANES:] + t2[0:1])
    cand = jnp.concatenate(pieces, axis=0)
    cmax = cand[0:1]

    work = cand
    iota = lax.broadcasted_iota(jnp.int32, cand.shape, 0).astype(F32) if exact else None
    for _ in range(k):
        _, sel = _pick_max(work, iota, exact)
        work = jnp.where(sel, NEG_INF, work)
    taken = jnp.where(work != cand, 1.0, 0.0)
    z = jnp.sum(taken * jnp.exp(cand - cmax), axis=0, keepdims=True)

    n_rows = [jnp.sum(taken[0:16], axis=0, keepdims=True)]
    for a in range(1, SUBLANES):
        lo = 16 + (a - 1) * SUBLANES
        n_rows.append(jnp.sum(taken[lo:lo + SUBLANES], axis=0, keepdims=True))
    base = 16 + 7 * SUBLANES
    for a in range(SUBLANES, k):
        n_rows.append(taken[base + a - SUBLANES:base + a - SUBLANES + 1])
    n1 = jnp.zeros(s1.shape, F32)
    for a in range(k):
        hit = (r1 == float(a)) if exact else (s1 == t1[a:a + 1])
        n1 = jnp.where(hit, n_rows[a], n1)

    extra = None
    if not exact:
        def count(x):
            return jnp.sum(x, axis=0, keepdims=True) - float(k)
        extra = (count(jnp.where(left1 != s1, 1.0, 0.0)) + count(jnp.where(r2 < float(k), 1.0, 0.0))
                 + count(taken))
    return (r2, jnp.exp(s2 - t2[0:1]), n1, jnp.exp(s1 - t1[0:1]) / z), extra


def _peer_sel_kernel(h2_ref, w_ref, keys_ref, u_ref, v_ref, r2_ref, e2_ref, n1_ref, e1_ref, ub_ref, vtb_ref,
                     t1_scr, t2_scr):
    ub_ref[...] = u_ref[...].astype(BF16)
    vtb_ref[...] = v_ref[...].T.astype(BF16)

    half = PEER_NKEYS
    q = jnp.dot(h2_ref[...], w_ref[...], preferred_element_type=F32).astype(BF16)
    nt = (((1,), (1,)), ((), ()))
    s1 = lax.dot_general(keys_ref[0, 0], q[:, :half], nt, preferred_element_type=F32)
    s2 = lax.dot_general(keys_ref[0, 1], q[:, half:], nt, preferred_element_type=F32)

    def store(tables):
        for ref, val in zip((r2_ref, e2_ref, n1_ref, e1_ref), tables):
            ref[0] = val.astype(ref.dtype)

    tables, extra = _selection_tables(s1, s2, t1_scr, t2_scr, exact=False)
    store(tables)

    @pl.when(jnp.max(extra) > 0.0)
    def _():
        store(_selection_tables(s1, s2, t1_scr, t2_scr, exact=True)[0])


def _peer_sel(h2, w_pq, keys, peer_u, peer_v):
    t, d = h2.shape
    ne = peer_u.shape[0]
    nh = PEER_HEADS
    qd = 2 * PEER_NKEYS
    steps = (t // SEL_TM) * nh
    rows = ne // steps
    assert rows * steps == ne and rows % LANES == 0
    tab = jax.ShapeDtypeStruct((nh, PEER_NKEYS, t), F32)
    tab16 = jax.ShapeDtypeStruct((nh, PEER_NKEYS, t), BF16)
    tab_spec = pl.BlockSpec((1, PEER_NKEYS, SEL_TM), lambda i, h: (h, 0, i))
    return pl.pallas_call(
        _peer_sel_kernel,
        grid=(t // SEL_TM, nh),
        in_specs=[pl.BlockSpec((SEL_TM, d), lambda i, h: (i, 0)),
                  pl.BlockSpec((d, qd), lambda i, h: (0, h)),
                  pl.BlockSpec((1, 2, PEER_NKEYS, PEER_NKEYS), lambda i, h: (h, 0, 0, 0)),
                  pl.BlockSpec((rows, d), lambda i, h: (i * nh + h, 0)),
                  pl.BlockSpec((rows, d), lambda i, h: (i * nh + h, 0))],
        out_specs=[tab_spec, tab_spec, tab_spec, tab_spec,
                   pl.BlockSpec((rows, d), lambda i, h: (i * nh + h, 0)),
                   pl.BlockSpec((d, rows), lambda i, h: (0, i * nh + h))],
        out_shape=[tab16, tab16, tab, tab,
                   jax.ShapeDtypeStruct((ne, d), BF16), jax.ShapeDtypeStruct((d, ne), BF16)],
        scratch_shapes=[pltpu.VMEM((PEER_TOPK, SEL_TM), F32), pltpu.VMEM((PEER_TOPK, SEL_TM), F32)],
        compiler_params=_cparams(("arbitrary", "arbitrary")),
        name="peer_sel",
    )(h2, w_pq, keys, peer_u, peer_v)


def _peer_ffn_kernel(h2_ref, u0_ref, uodd_ref, unext_ref, vt_ref, r2_ref, e2_ref, n1_ref, e1_ref, x1_ref,
                     mod_ref, modo_ref, g_ref, o_ref, acc_scr, a0_scr, a1_scr):
    c = pl.program_id(1)
    nk = PEER_NKEYS
    per = FFN_EC // nk
    nt = (((1,), (1,)), ((), ()))
    zero = jnp.zeros((), BF16)

    def preact(u_ref, dst_scr, lo, hi):
        dst_scr[lo:hi, :] = lax.dot_general(u_ref[lo:hi, :], h2_ref[...], nt, preferred_element_type=F32)

    def weights(a_scr, chunk, lo, hi):
        parts = []
        for ii in range(lo // nk, hi // nk):
            i = chunk * per + ii
            a = a_scr[ii * nk:(ii + 1) * nk, :]
            gsum = None
            for hd in range(PEER_HEADS):
                n1 = n1_ref[hd, pl.ds(i, 1), :].astype(BF16)
                e1 = e1_ref[hd, pl.ds(i, 1), :].astype(BF16)
                g = jnp.where(r2_ref[hd] < n1, e2_ref[hd], zero) * e1
                gsum = g if gsum is None else gsum + g
            act = 0.5 * a * (1.0 + lax.erf(a * (2.0 ** -0.5)))
            parts.append(gsum * act.astype(BF16))
        return jnp.concatenate(parts, axis=0)

    def accumulate(w, col0, lo, hi):
        acc_scr[...] += jnp.dot(vt_ref[:, col0 + lo:col0 + hi], w, preferred_element_type=F32)

    @pl.when(c == 0)
    def _():
        acc_scr[...] = jnp.zeros(acc_scr.shape, F32)
        preact(u0_ref, a0_scr, 0, FFN_EC)

    bounds = [(p * FFN_PIECE, (p + 1) * FFN_PIECE) for p in range(FFN_EC // FFN_PIECE)]
    pieces = ([(a0_scr, 2 * c, 0, uodd_ref, a1_scr, lo, hi) for lo, hi in bounds]
              + [(a1_scr, 2 * c + 1, FFN_EC, unext_ref, a0_scr, lo, hi) for lo, hi in bounds])
    pending = None
    for src_scr, chunk, col0, u_ref, dst_scr, lo, hi in pieces:
        w = weights(src_scr, chunk, lo, hi)
        if pending is not None:
            accumulate(*pending)
        preact(u_ref, dst_scr, lo, hi)
        pending = (w, col0, lo, hi)
    accumulate(*pending)

    @pl.when(c == pl.num_programs(1) - 1)
    def _():
        y = acc_scr[...].T
        x2 = x1_ref[...] + mod_ref[0, 5:6, :] * y
        o_ref[...] = _rms(x2, g_ref[...]) * (1.0 + modo_ref[0, 1:2, :]) + modo_ref[0, 0:1, :]


def _peer_ffn(h2, u, vt, tabs, x1, mod3, modo3, g_final, seq):
    t, d = h2.shape
    ne = u.shape[0]
    tiles_per_seq = seq // FFN_TB
    tab_spec = pl.BlockSpec((PEER_HEADS, PEER_NKEYS, FFN_TB), lambda i, c: (0, 0, i))
    nchunk = ne // FFN_EC
    once = pl.Buffered(1)
    return pl.pallas_call(
        _peer_ffn_kernel,
        grid=(t // FFN_TB, nchunk // 2),
        in_specs=[pl.BlockSpec((FFN_TB, d), lambda i, c: (i, 0)),
                  pl.BlockSpec((FFN_EC, d), lambda i, c: (0, 0), pipeline_mode=once),
                  pl.BlockSpec((FFN_EC, d), lambda i, c: (2 * c + 1, 0)),
                  pl.BlockSpec((FFN_EC, d), lambda i, c: (jnp.minimum(2 * c + 2, nchunk - 1), 0)),
                  pl.BlockSpec((d, 2 * FFN_EC), lambda i, c: (0, c)),
                  tab_spec, tab_spec, tab_spec, tab_spec,
                  pl.BlockSpec((FFN_TB, d), lambda i, c: (i, 0), pipeline_mode=once),
                  pl.BlockSpec((1, 6, d), lambda i, c: (i // tiles_per_seq, 0, 0)),
                  pl.BlockSpec((1, 2, d), lambda i, c: (i // tiles_per_seq, 0, 0)),
                  pl.BlockSpec((1, d), lambda i, c: (0, 0))],
        out_specs=pl.BlockSpec((FFN_TB, d), lambda i, c: (i, 0)),
        out_shape=jax.ShapeDtypeStruct((t, d), F32),
        scratch_shapes=[pltpu.VMEM((d, FFN_TB), F32), pltpu.VMEM((FFN_EC, FFN_TB), F32),
                        pltpu.VMEM((FFN_EC, FFN_TB), F32)],
        compiler_params=pltpu.CompilerParams(dimension_semantics=("arbitrary", "arbitrary"),
                                             vmem_limit_bytes=FFN_VMEM_LIMIT),
        name="peer_ffn",
    )(h2, u, u, u, vt, *tabs, x1, mod3, modo3, g_final.reshape(1, d))


def kernel(x, c, w_ada, b_ada, g_mix, w_in, conv_w, conv_b, b_igate, b_fgate, lambda_q1, lambda_k1,
           lambda_q2, lambda_k2, da_norm, ml_norm, w_out, g_ffn, w_pq, sub_keys, peer_u, peer_v,
           w_ada_final, b_ada_final, g_final):
    batch, seq, d = x.shape
    depth = w_ada.shape[0]
    t = batch * seq
    da_heads = d // 256
    ml_heads = d // 512
    att_cols = 3 * da_heads * DA_V
    rope_cols = 2 * da_heads * DA_V
    main_cols = att_cols + 2 * ml_heads * ML_QK + 2 * ml_heads * ML_V
    assert batch <= SUBLANES and seq % PROJ_TM == 0 and seq % ATT_TQ == 0 and seq % ML_CHUNK == 0
    assert w_in.shape[2] == main_cols + 2 * ml_heads

    c8 = jnp.zeros((SUBLANES, d), F32).at[:batch].set(c.astype(F32))
    modo3 = _ada(c8, w_ada_final, b_ada_final)[:batch].reshape(batch, 2, d)
    xt = x.reshape(t, d)

    for l in range(depth):
        mod3 = _ada(c8, w_ada[l], b_ada[l])[:batch].reshape(batch, 6, d)
        lam_init = 0.8 - 0.6 * math.exp(-0.3 * l)

        w_in_t = jnp.swapaxes(w_in[l], 0, 1).astype(BF16)
        w_gate_t = jnp.zeros((LANES, d), BF16).at[:2 * ml_heads].set(w_in_t[main_cols:])
        pa, pm, gates = _proj(xt, mod3, g_mix[l], w_in_t, w_gate_t, seq, main_cols, att_cols, rope_cols)

        lam4 = jnp.stack([lambda_q1[l], lambda_k1[l], lambda_q2[l], lambda_k2[l]]).astype(F32)
        oa = _attn(pa, lam4, da_norm[l], batch, seq, da_heads, lam_init)

        om = _mlstm(pm, gates, conv_w[l], conv_b[l], b_igate[l], b_fgate[l], ml_norm[l], batch, seq, ml_heads)

        x1, h2 = _outproj(oa, om, w_out[l].astype(BF16), xt, mod3, g_ffn[l], seq)

        *tabs, u_b, vt_b = _peer_sel(h2, w_pq[l].astype(BF16), sub_keys[l].astype(BF16), peer_u[l], peer_v[l])
        assert depth == 1
        xt = _peer_ffn(h2, u_b, vt_b, tabs, x1, mod3, modo3, g_final, seq)

    return xt.reshape(batch, seq, d)
```

```python
import functools
import math

import numpy as np
import jax
import jax.numpy as jnp
from jax import lax
from jax.experimental import pallas as pl
from jax.experimental.pallas import tpu as pltpu

F32 = jnp.float32
BF16 = jnp.bfloat16

DA_QK = 64
DA_V = 128
ML_QK = 128
ML_V = 256
CONV_W = 4
GATE_CAP = 15.0
ROPE_THETA = 10000.0
PEER_HEADS = 8
PEER_NKEYS = 128
PEER_TOPK = 16
EPS = 1e-6

LANES = 128
SUBLANES = 8
VMEM_LIMIT = 56 * 1024 * 1024
FFN_VMEM_LIMIT = 60 * 1024 * 1024

ADA_TN = 1024
PROJ_TM = 256
PROJ_TN = 512
ATT_TQ = 512
ATT_ONES_ROWS = 16
ATT_HEADS_PER_STEP = 2
ML_CHUNK = 256
OUT_TM = 256
SEL_TM = 512
FFN_TB = 512
FFN_EC = 512

NEG_INF = float("-inf")


def _cparams(sem):
    return pltpu.CompilerParams(dimension_semantics=sem, vmem_limit_bytes=VMEM_LIMIT)


def _rms(x, g):
    return x * lax.rsqrt(jnp.mean(x * x, axis=-1, keepdims=True) + EPS) * g


def _ada_kernel(c_ref, w_ref, b_ref, o_ref):
    c = c_ref[...]
    cs = c * jax.nn.sigmoid(c)
    o_ref[...] = jnp.dot(cs, w_ref[...], preferred_element_type=F32,
                         precision=lax.Precision.HIGHEST) + b_ref[...]


def _ada(c8, w, b):
    d, n = w.shape
    return pl.pallas_call(
        _ada_kernel,
        grid=(n // ADA_TN,),
        in_specs=[pl.BlockSpec((SUBLANES, d), lambda j: (0, 0)),
                  pl.BlockSpec((d, ADA_TN), lambda j: (0, j)),
                  pl.BlockSpec((1, ADA_TN), lambda j: (0, j))],
        out_specs=pl.BlockSpec((SUBLANES, ADA_TN), lambda j: (0, j)),
        out_shape=jax.ShapeDtypeStruct((SUBLANES, n), F32),
        compiler_params=_cparams(("arbitrary",)),
        name="ada",
    )(c8, w, b.reshape(1, n))


def _proj_kernel(n_rope, n_att, n_tiles, x_ref, mod_ref, g_ref, w_ref, wg_ref, cos_ref, sa_ref, sb_ref,
                 pa_ref, pm_ref, gate_ref):
    nt = (((1,), (1,)), ((), ()))
    tn = PROJ_TN
    y = _rms(x_ref[...], g_ref[...])
    hb = (y * (1.0 + mod_ref[0, 1:2, :]) + mod_ref[0, 0:1, :]).astype(BF16)
    gate_ref[...] = lax.dot_general(hb, wg_ref[...], nt, preferred_element_type=F32)
    cos, sa, sb = cos_ref[...], sa_ref[...], sb_ref[...]
    for j in range(n_tiles):
        acc = lax.dot_general(hb, w_ref[j * tn:(j + 1) * tn, :], nt, preferred_element_type=F32)
        if j < n_rope:
            for g in range(tn // LANES):
                xg = acc[:, g * LANES:(g + 1) * LANES]
                r = (xg * cos + pltpu.roll(xg, LANES - DA_QK // 2, 1) * sa
                     + pltpu.roll(xg, DA_QK // 2, 1) * sb)
                pa_ref[:, j * tn + g * LANES:j * tn + (g + 1) * LANES] = r.astype(BF16)
        elif j < n_att:
            pa_ref[:, j * tn:(j + 1) * tn] = acc.astype(BF16)
        else:
            pm_ref[:, (j - n_att) * tn:(j - n_att + 1) * tn] = acc


def _rope_tables(seq):
    half = DA_QK // 2
    inv = ROPE_THETA ** (-jnp.arange(half, dtype=F32) / half)
    ang = jnp.arange(seq, dtype=F32)[:, None] * inv[None, :]
    lane = np.arange(LANES)
    first = (lane % DA_QK) < half
    cos = jnp.cos(ang)[:, lane % half]
    sin = jnp.sin(ang)[:, lane % half]
    sa = jnp.where(first[None, :], -sin, 0.0)
    sb = jnp.where(first[None, :], 0.0, sin)
    return cos, sa, sb


def _proj(x2, mod3, g, w_main_t, w_gate_t, seq, n, n_att_cols, n_rope_cols):
    t, d = x2.shape
    assert n % PROJ_TN == 0 and n <= w_main_t.shape[0]
    n_att = n_att_cols // PROJ_TN
    n_rope = n_rope_cols // PROJ_TN
    n_tiles = n // PROJ_TN
    tiles_per_seq = seq // PROJ_TM
    cos, sa, sb = _rope_tables(seq)
    tab_spec = pl.BlockSpec((PROJ_TM, LANES), lambda i: (i % tiles_per_seq, 0))
    once = pl.Buffered(1)
    return pl.pallas_call(
        functools.partial(_proj_kernel, n_rope, n_att, n_tiles),
        grid=(t // PROJ_TM,),
        in_specs=[pl.BlockSpec((PROJ_TM, d), lambda i: (i, 0)),
                  pl.BlockSpec((1, 6, d), lambda i: (i // tiles_per_seq, 0, 0)),
                  pl.BlockSpec((1, d), lambda i: (0, 0)),
                  pl.BlockSpec(w_main_t.shape, lambda i: (0, 0), pipeline_mode=once),
                  pl.BlockSpec((LANES, d), lambda i: (0, 0), pipeline_mode=once),
                  tab_spec, tab_spec, tab_spec],
        out_specs=[pl.BlockSpec((PROJ_TM, n_att_cols), lambda i: (i, 0)),
                   pl.BlockSpec((PROJ_TM, n - n_att_cols), lambda i: (i, 0)),
                   pl.BlockSpec((PROJ_TM, LANES), lambda i: (i, 0))],
        out_shape=[jax.ShapeDtypeStruct((t, n_att_cols), BF16),
                   jax.ShapeDtypeStruct((t, n - n_att_cols), F32),
                   jax.ShapeDtypeStruct((t, LANES), F32)],
        compiler_params=_cparams(("arbitrary",)),
        name="proj",
    )(x2, mod3, g.reshape(1, d), w_main_t, w_gate_t, cos, sa, sb)


def _attn_kernel(lam_init, lam_ref, q_ref, k_ref, v_ref, nrm_ref, o_ref, vt_scr):
    i = pl.program_id(2)
    tq = ATT_TQ
    nhs = ATT_HEADS_PER_STEP
    seq = k_ref.shape[0]
    nq = seq // tq

    @pl.when(i == 0)
    def _():
        for hh in range(nhs):
            vt_scr[hh, :DA_V, :] = v_ref[:, hh * DA_V:(hh + 1) * DA_V].astype(F32).T.astype(BF16)
            vt_scr[hh, DA_V:, :] = jnp.ones((ATT_ONES_ROWS, seq), BF16)

    lv = lam_ref[...]
    lam = (jnp.exp(jnp.sum(lv[0:1] * lv[1:2], axis=-1, keepdims=True))
           - jnp.exp(jnp.sum(lv[2:3] * lv[3:4], axis=-1, keepdims=True)) + lam_init)
    lane = lax.broadcasted_iota(jnp.int32, (tq, LANES), 1)
    nt = (((1,), (1,)), ((), ()))

    def block(qi):
        below = qi * tq
        kpos = lax.broadcasted_iota(jnp.int32, (tq, 2 * tq), 0)
        col = lax.broadcasted_iota(jnp.int32, (tq, 2 * tq), 1)
        causal = kpos <= jnp.where(col >= tq, col - tq, col)
        for hh in range(nhs):
            hl = slice(hh * LANES, (hh + 1) * LANES)
            q = q_ref[:, hl] * (DA_QK ** -0.5)
            zero = jnp.zeros_like(q)
            q2 = jnp.concatenate([jnp.where(lane < DA_QK, q, zero), jnp.where(lane >= DA_QK, q, zero)], axis=0)
            s_diag = lax.dot_general(k_ref[below:below + tq, hl], q2, nt, preferred_element_type=F32)
            s_diag = jnp.where(causal, s_diag, NEG_INF)
            m = jnp.max(s_diag, axis=0, keepdims=True)
            if qi > 0:
                s_low = lax.dot_general(k_ref[0:below, hl], q2, nt, preferred_element_type=F32)
                m = jnp.maximum(m, jnp.max(s_low, axis=0, keepdims=True))
            acc = jnp.dot(vt_scr[hh, :, below:below + tq], jnp.exp(s_diag - m).astype(BF16),
                          preferred_element_type=F32)
            if qi > 0:
                acc = acc + jnp.dot(vt_scr[hh, :, 0:below], jnp.exp(s_low - m).astype(BF16),
                                    preferred_element_type=F32)
            o2 = acc[:DA_V, :] / acc[DA_V:DA_V + 1, :]
            o = (o2[:, :tq] - lam * o2[:, tq:]).T
            o_ref[:, hh * DA_V:(hh + 1) * DA_V] = (_rms(o, nrm_ref[...]) * (1.0 - lam_init)).astype(o_ref.dtype)

    for qi in range(nq):
        pl.when(i == qi)(functools.partial(block, qi))


def _attn(pa, lam4, da_norm, batch, seq, heads, lam_init):
    t = pa.shape[0]
    nq = seq // ATT_TQ
    nhs = ATT_HEADS_PER_STEP
    groups = heads // nhs
    wide = nhs * LANES
    return pl.pallas_call(
        functools.partial(_attn_kernel, lam_init),
        grid=(batch, groups, nq),
        in_specs=[pl.BlockSpec((4, DA_QK), lambda b, g, i: (0, 0)),
                  pl.BlockSpec((ATT_TQ, wide), lambda b, g, i: (b * nq + i, g)),
                  pl.BlockSpec((seq, wide), lambda b, g, i: (b, groups + g)),
                  pl.BlockSpec((seq, wide), lambda b, g, i: (b, 2 * groups + g)),
                  pl.BlockSpec((1, DA_V), lambda b, g, i: (0, 0))],
        out_specs=pl.BlockSpec((ATT_TQ, wide), lambda b, g, i: (b * nq + i, g)),
        out_shape=jax.ShapeDtypeStruct((t, heads * DA_V), BF16),
        scratch_shapes=[pltpu.VMEM((nhs, DA_V + ATT_ONES_ROWS, seq), BF16)],
        compiler_params=_cparams(("arbitrary", "arbitrary", "arbitrary")),
        name="attn",
    )(lam4, pa, pa, pa, da_norm.reshape(1, DA_V))


def _softcap(x):
    return GATE_CAP * jnp.tanh(x / GATE_CAP)


def _dwconv_silu(x, w, b):
    seq = x.shape[0]
    row = lax.broadcasted_iota(jnp.int32, x.shape, 0)
    y = x * w[CONV_W - 1:CONV_W]
    for s in range(1, CONV_W):
        xs = jnp.where(row >= s, pltpu.roll(x, s, 0), 0.0)
        y = y + xs * w[CONV_W - 1 - s:CONV_W - s]
    y = y + b
    return y * jax.nn.sigmoid(y)


def _mlstm_kernel(bi_ref, bf_ref, q_ref, k_ref, v_ref, og_ref, cwq_ref, cwk_ref, cbq_ref, cbk_ref,
                  gc_ref, nrm_ref, o_ref,
                  q_scr, k_scr, lic_scr, lfc_scr, lir_scr, lfr_scr, gt_scr, bc_scr, br_scr, c_scr, m_scr):
    hd = pl.program_id(1)
    L = ML_CHUNK
    seq = q_ref.shape[0]
    nc = seq // L
    bi = bi_ref[hd]
    bf = bf_ref[hd]

    q_scr[...] = _dwconv_silu(q_ref[...], cwq_ref[...], cbq_ref[...]).astype(BF16)
    k_scr[...] = _dwconv_silu(k_ref[...], cwk_ref[...], cbk_ref[...]) * (ML_QK ** -0.5)

    gc = gc_ref[...]
    glane = lax.broadcasted_iota(jnp.int32, gc.shape, 1)
    gi = jnp.sum(jnp.where(glane == hd, gc, 0.0), axis=1, keepdims=True)
    gf = jnp.sum(jnp.where(glane == pl.num_programs(1) + hd, gc, 0.0), axis=1, keepdims=True)
    lic_scr[...] = _softcap(gi + bi)
    lfc_scr[...] = jax.nn.log_sigmoid(_softcap(gf + bf))
    gt_scr[...] = gc.T
    li_row = _softcap(gt_scr[pl.ds(hd, 1), :] + bi)
    lf_row = jax.nn.log_sigmoid(_softcap(gt_scr[pl.ds(pl.num_programs(1) + hd, 1), :] + bf))
    for c in range(nc):
        lir_scr[c:c + 1, :] = li_row[:, c * L:(c + 1) * L]
        lfr_scr[c:c + 1, :] = lf_row[:, c * L:(c + 1) * L]

    c_scr[...] = jnp.zeros(c_scr.shape, F32)
    m_scr[...] = jnp.zeros(m_scr.shape, F32)

    r_i = lax.broadcasted_iota(jnp.int32, (L, L), 0)
    c_i = lax.broadcasted_iota(jnp.int32, (L, L), 1)
    tril = (c_i <= r_i)
    tril_b = tril.astype(BF16)
    triu_b = (r_i <= c_i).astype(BF16)
    ones_col = (lax.broadcasted_iota(jnp.int32, (L, LANES), 1) == 0).astype(BF16)
    nrm = nrm_ref[...]

    def split3(x):
        hi = x.astype(BF16)
        r = x - hi.astype(F32)
        mid = r.astype(BF16)
        return hi, mid, (r - mid.astype(F32)).astype(BF16)

    for c in range(nc):
        b_c = None
        for piece in split3(lfc_scr[c * L:(c + 1) * L, :]):
            term = jnp.dot(tril_b, jnp.broadcast_to(piece, (L, L)), preferred_element_type=F32)
            b_c = term if b_c is None else b_c + term
        b_r = None
        for piece in split3(lfr_scr[c:c + 1, :]):
            term = jnp.dot(jnp.broadcast_to(piece, (L, L)), triu_b, preferred_element_type=F32)
            b_r = term if b_r is None else b_r + term
        bc_scr[c] = b_c
        br_scr[c] = b_r

    def body(c):
        sl = slice(c * L, (c + 1) * L)
        qc = q_scr[sl, :]
        kc = k_scr[sl, :]
        vc = v_ref[sl, :].astype(BF16)
        lic = lic_scr[sl, :]
        lir = lir_scr[c:c + 1, :]
        m = m_scr[...]

        b_c = bc_scr[c]
        b_r = br_scr[c]
        dmat = jnp.where(tril, b_c - b_r + lir, NEG_INF)
        bcol = b_c[:, 0:1]
        inter = bcol + m
        m_j = jnp.maximum(inter, jnp.max(dmat, axis=-1, keepdims=True))
        w_intra = jnp.exp(dmat - m_j)
        w_inter = jnp.exp(inter - m_j)

        sqk = lax.dot_general(qc, kc.astype(BF16), (((1,), (1,)), ((), ())), preferred_element_type=F32)
        qk = sqk * w_intra
        q_c = jnp.dot(qc, c_scr[...].astype(BF16), preferred_element_type=F32)
        num = w_inter * q_c[:, :ML_V] + jnp.dot(qk.astype(BF16), vc, preferred_element_type=F32)
        den = w_inter * q_c[:, ML_V:ML_V + 1] + jnp.sum(qk, axis=-1, keepdims=True)
        hh = num / jnp.maximum(jnp.abs(den), jnp.exp(-m_j))
        og = og_ref[sl, :]
        o_ref[sl, :] = (_rms(hh, nrm) * jax.nn.sigmoid(og)).astype(o_ref.dtype)

        b_l = b_c[L - 1:L, 0:1]
        logw = b_l - bcol + lic
        m_new = jnp.maximum(b_l + m, jnp.max(logw, axis=0, keepdims=True))
        decay = jnp.exp(b_l + m - m_new)
        ws = jnp.exp(logw - m_new)
        kw = (kc * ws).astype(BF16)
        vext = jnp.concatenate([vc, ones_col], axis=1)
        upd = lax.dot_general(kw, vext, (((0,), (0,)), ((), ())), preferred_element_type=F32)
        c_scr[...] = decay * c_scr[...] + upd
        m_scr[...] = m_new

    for c in range(nc):
        body(c)


def _mlstm(pm, gates, conv_w, conv_b, b_ig, b_fg, ml_norm, batch, seq, heads):
    t = pm.shape[0]
    nc = seq // ML_CHUNK
    qk_w = heads * ML_QK
    vblk = 2 * qk_w // ML_V
    smem = pl.BlockSpec(memory_space=pltpu.SMEM)
    return pl.pallas_call(
        _mlstm_kernel,
        grid=(batch, heads),
        in_specs=[smem, smem,
                  pl.BlockSpec((seq, ML_QK), lambda b, h: (b, h)),
                  pl.BlockSpec((seq, ML_QK), lambda b, h: (b, heads + h)),
                  pl.BlockSpec((seq, ML_V), lambda b, h: (b, vblk + h)),
                  pl.BlockSpec((seq, ML_V), lambda b, h: (b, vblk + heads + h)),
                  pl.BlockSpec((CONV_W, ML_QK), lambda b, h: (0, h)),
                  pl.BlockSpec((CONV_W, ML_QK), lambda b, h: (0, heads + h)),
                  pl.BlockSpec((1, ML_QK), lambda b, h: (0, h)),
                  pl.BlockSpec((1, ML_QK), lambda b, h: (0, heads + h)),
                  pl.BlockSpec((seq, LANES), lambda b, h: (b, 0)),
                  pl.BlockSpec((1, ML_V), lambda b, h: (0, 0))],
        out_specs=pl.BlockSpec((seq, ML_V), lambda b, h: (b, h)),
        out_shape=jax.ShapeDtypeStruct((t, heads * ML_V), BF16),
        scratch_shapes=[pltpu.VMEM((seq, ML_QK), BF16), pltpu.VMEM((seq, ML_QK), F32),
                        pltpu.VMEM((seq, 1), F32), pltpu.VMEM((seq, 1), F32),
                        pltpu.VMEM((nc, ML_CHUNK), F32), pltpu.VMEM((nc, ML_CHUNK), F32),
                        pltpu.VMEM((LANES, seq), F32),
                        pltpu.VMEM((nc, ML_CHUNK, ML_CHUNK), F32), pltpu.VMEM((nc, ML_CHUNK, ML_CHUNK), F32),
                        pltpu.VMEM((ML_QK, ML_V + LANES), F32), pltpu.VMEM((1, 1), F32)],
        compiler_params=_cparams(("arbitrary", "arbitrary")),
        name="mlstm",
    )(b_ig, b_fg, pm, pm, pm, pm, conv_w, conv_w, conv_b.reshape(1, -1), conv_b.reshape(1, -1),
      gates, ml_norm.reshape(1, ML_V))


def _outproj_kernel(oa_ref, om_ref, w_ref, x_ref, mod_ref, g_ref, x1_ref, h2_ref):
    ka = oa_ref.shape[1]
    mixed = (jnp.dot(oa_ref[...], w_ref[:ka, :], preferred_element_type=F32)
             + jnp.dot(om_ref[...], w_ref[ka:, :], preferred_element_type=F32))
    x1 = x_ref[...] + mod_ref[0, 2:3, :] * mixed
    x1_ref[...] = x1
    h2 = _rms(x1, g_ref[...]) * (1.0 + mod_ref[0, 4:5, :]) + mod_ref[0, 3:4, :]
    h2_ref[...] = h2.astype(BF16)


def _outproj(oa, om, w_out, x2, mod3, g_ffn, seq):
    t, d = x2.shape
    tiles_per_seq = seq // OUT_TM
    return pl.pallas_call(
        _outproj_kernel,
        grid=(t // OUT_TM,),
        in_specs=[pl.BlockSpec((OUT_TM, oa.shape[1]), lambda i: (i, 0)),
                  pl.BlockSpec((OUT_TM, om.shape[1]), lambda i: (i, 0)),
                  pl.BlockSpec(w_out.shape, lambda i: (0, 0)),
                  pl.BlockSpec((OUT_TM, d), lambda i: (i, 0)),
                  pl.BlockSpec((1, 6, d), lambda i: (i // tiles_per_seq, 0, 0)),
                  pl.BlockSpec((1, d), lambda i: (0, 0))],
        out_specs=[pl.BlockSpec((OUT_TM, d), lambda i: (i, 0)),
                   pl.BlockSpec((OUT_TM, d), lambda i: (i, 0))],
        out_shape=[jax.ShapeDtypeStruct((t, d), F32), jax.ShapeDtypeStruct((t, d), BF16)],
        compiler_params=_cparams(("arbitrary",)),
        name="outproj",
    )(oa, om, w_out, x2, mod3, g_ffn.reshape(1, d))


_CAND_ROWS = 16 + 8 + 6 * 8 + 8


def _pick_max(work, iota, exact):
    mx = jnp.max(work, axis=0, keepdims=True)
    if not exact:
        return mx, work == mx
    first = jnp.min(jnp.where(work == mx, iota, float(work.shape[0])), axis=0, keepdims=True)
    return mx, iota == first


def _rank16(s, top_scr, exact, want_rank=True):
    iota = lax.broadcasted_iota(jnp.int32, s.shape, 0).astype(F32) if exact else None
    rank = jnp.full(s.shape, float(PEER_TOPK), F32) if want_rank else None
    for r in range(PEER_TOPK):
        mx, sel = _pick_max(s, iota, exact)
        if want_rank:
            rank = jnp.where(sel, float(r), rank)
        s = jnp.where(sel, NEG_INF, s)
        top_scr[r:r + 1, :] = mx
    return rank, s


def _selection_tables(s1, s2, t1_scr, t2_scr, exact):
    k = PEER_TOPK
    r1, left1 = _rank16(s1, t1_scr, exact, want_rank=exact)
    r2, _ = _rank16(s2, t2_scr, exact)
    t1 = t1_scr[...]
    t2 = t2_scr[...]
    tm = s1.shape[1]

    brow = lax.broadcasted_iota(jnp.int32, (SUBLANES, tm), 0)
    pieces = [t1[0:1] + t2, t1[1:2] + t2[0:SUBLANES]]
    for a in range(2, SUBLANES):
        pieces.append(jnp.where(brow < k // (a + 1), t1[a:a + 1] + t2[0:SUBLANES], NEG_INF))
    pieces.append(t1[SUBLANES:] + t2[0:1])
    cand = jnp.concatenate(pieces, axis=0)
    cmax = cand[0:1]

    work = cand
    iota = lax.broadcasted_iota(jnp.int32, cand.shape, 0).astype(F32) if exact else None
    for _ in range(k):
        _, sel = _pick_max(work, iota, exact)
        work = jnp.where(sel, NEG_INF, work)
    taken = jnp.where(work != cand, 1.0, 0.0)
    z = jnp.sum(taken * jnp.exp(cand - cmax), axis=0, keepdims=True)

    n_rows = [jnp.sum(taken[0:16], axis=0, keepdims=True)]
    for a in range(1, SUBLANES):
        lo = 16 + (a - 1) * SUBLANES
        n_rows.append(jnp.sum(taken[lo:lo + SUBLANES], axis=0, keepdims=True))
    base = 16 + 7 * SUBLANES
    for a in range(SUBLANES, k):
        n_rows.append(taken[base + a - SUBLANES:base + a - SUBLANES + 1])
    n1 = jnp.zeros(s1.shape, F32)
    for a in range(k):
        hit = (r1 == float(a)) if exact else (s1 == t1[a:a + 1])
        n1 = jnp.where(hit, n_rows[a], n1)

    extra = None
    if not exact:
        def count(x):
            return jnp.sum(x, axis=0, keepdims=True) - float(k)
        extra = (count(jnp.where(left1 != s1, 1.0, 0.0)) + count(jnp.where(r2 < float(k), 1.0, 0.0))
                 + count(taken))
    return (r2, jnp.exp(s2 - t2[0:1]), n1, jnp.exp(s1 - t1[0:1]) / z), extra


def _peer_sel_kernel(h2_ref, w_ref, keys_ref, u_ref, v_ref, r2_ref, e2_ref, n1_ref, e1_ref, ub_ref, vtb_ref,
                     t1_scr, t2_scr):
    ub_ref[...] = u_ref[...].astype(BF16)
    vtb_ref[...] = v_ref[...].T.astype(BF16)

    half = PEER_NKEYS
    q = jnp.dot(h2_ref[...], w_ref[...], preferred_element_type=F32).astype(BF16)
    nt = (((1,), (1,)), ((), ()))
    s1 = lax.dot_general(keys_ref[0, 0], q[:, :half], nt, preferred_element_type=F32)
    s2 = lax.dot_general(keys_ref[0, 1], q[:, half:], nt, preferred_element_type=F32)

    def store(tables):
        for ref, val in zip((r2_ref, e2_ref, n1_ref, e1_ref), tables):
            ref[0] = val.astype(ref.dtype)

    tables, extra = _selection_tables(s1, s2, t1_scr, t2_scr, exact=False)
    store(tables)

    @pl.when(jnp.max(extra) > 0.0)
    def _():
        store(_selection_tables(s1, s2, t1_scr, t2_scr, exact=True)[0])


def _peer_sel(h2, w_pq, keys, peer_u, peer_v):
    t, d = h2.shape
    ne = peer_u.shape[0]
    nh = PEER_HEADS
    qd = 2 * PEER_NKEYS
    steps = (t // SEL_TM) * nh
    rows = ne // steps
    assert rows * steps == ne and rows % LANES == 0
    tab = jax.ShapeDtypeStruct((nh, PEER_NKEYS, t), F32)
    tab16 = jax.ShapeDtypeStruct((nh, PEER_NKEYS, t), BF16)
    tab_spec = pl.BlockSpec((1, PEER_NKEYS, SEL_TM), lambda i, h: (h, 0, i))
    return pl.pallas_call(
        _peer_sel_kernel,
        grid=(t // SEL_TM, nh),
        in_specs=[pl.BlockSpec((SEL_TM, d), lambda i, h: (i, 0)),
                  pl.BlockSpec((d, qd), lambda i, h: (0, h)),
                  pl.BlockSpec((1, 2, PEER_NKEYS, PEER_NKEYS), lambda i, h: (h, 0, 0, 0)),
                  pl.BlockSpec((rows, d), lambda i, h: (i * nh + h, 0)),
                  pl.BlockSpec((rows, d), lambda i, h: (i * nh + h, 0))],
        out_specs=[tab_spec, tab_spec, tab_spec, tab_spec,
                   pl.BlockSpec((rows, d), lambda i, h: (i * nh + h, 0)),
                   pl.BlockSpec((d, rows), lambda i, h: (0, i * nh + h))],
        out_shape=[tab16, tab16, tab, tab,
                   jax.ShapeDtypeStruct((ne, d), BF16), jax.ShapeDtypeStruct((d, ne), BF16)],
        scratch_shapes=[pltpu.VMEM((PEER_TOPK, SEL_TM), F32), pltpu.VMEM((PEER_TOPK, SEL_TM), F32)],
        compiler_params=_cparams(("arbitrary", "arbitrary")),
        name="peer_sel",
    )(h2, w_pq, keys, peer_u, peer_v)


def _peer_ffn_kernel(h2_ref, u0_ref, uodd_ref, unext_ref, vt_ref, r2_ref, e2_ref, n1_ref, e1_ref, x1_ref,
                     mod_ref, modo_ref, g_ref, o_ref, acc_scr, a0_scr, a1_scr):
    c = pl.program_id(1)
    nk = PEER_NKEYS
    per = FFN_EC // nk
    nt = (((1,), (1,)), ((), ()))
    zero = jnp.zeros((), BF16)

    def preact(u_ref, dst_scr):
        dst_scr[...] = lax.dot_general(u_ref[...], h2_ref[...], nt, preferred_element_type=F32)

    def weights(a_scr, chunk):
        parts = []
        for ii in range(per):
            i = chunk * per + ii
            a = a_scr[ii * nk:(ii + 1) * nk, :]
            gsum = None
            for hd in range(PEER_HEADS):
                n1 = n1_ref[hd, pl.ds(i, 1), :].astype(BF16)
                e1 = e1_ref[hd, pl.ds(i, 1), :].astype(BF16)
                g = jnp.where(r2_ref[hd] < n1, e2_ref[hd], zero) * e1
                gsum = g if gsum is None else gsum + g
            act = 0.5 * a * (1.0 + lax.erf(a * (2.0 ** -0.5)))
            parts.append(gsum * act.astype(BF16))
        return jnp.concatenate(parts, axis=0)

    def accumulate(w, col0):
        acc_scr[...] += jnp.dot(vt_ref[:, col0:col0 + FFN_EC], w, preferred_element_type=F32)

    @pl.when(c == 0)
    def _():
        acc_scr[...] = jnp.zeros(acc_scr.shape, F32)
        preact(u0_ref, a0_scr)

    preact(uodd_ref, a1_scr)
    accumulate(weights(a0_scr, 2 * c), 0)
    preact(unext_ref, a0_scr)
    accumulate(weights(a1_scr, 2 * c + 1), FFN_EC)

    @pl.when(c == pl.num_programs(1) - 1)
    def _():
        y = acc_scr[...].T
        x2 = x1_ref[...] + mod_ref[0, 5:6, :] * y
        o_ref[...] = _rms(x2, g_ref[...]) * (1.0 + modo_ref[0, 1:2, :]) + modo_ref[0, 0:1, :]


def _peer_ffn(h2, u, vt, tabs, x1, mod3, modo3, g_final, seq):
    t, d = h2.shape
    ne = u.shape[0]
    tiles_per_seq = seq // FFN_TB
    tab_spec = pl.BlockSpec((PEER_HEADS, PEER_NKEYS, FFN_TB), lambda i, c: (0, 0, i))
    nchunk = ne // FFN_EC
    once = pl.Buffered(1)
    return pl.pallas_call(
        _peer_ffn_kernel,
        grid=(t // FFN_TB, nchunk // 2),
        in_specs=[pl.BlockSpec((FFN_TB, d), lambda i, c: (i, 0)),
                  pl.BlockSpec((FFN_EC, d), lambda i, c: (0, 0), pipeline_mode=once),
                  pl.BlockSpec((FFN_EC, d), lambda i, c: (2 * c + 1, 0)),
                  pl.BlockSpec((FFN_EC, d), lambda i, c: (jnp.minimum(2 * c + 2, nchunk - 1), 0)),
                  pl.BlockSpec((d, 2 * FFN_EC), lambda i, c: (0, c)),
                  tab_spec, tab_spec, tab_spec, tab_spec,
                  pl.BlockSpec((FFN_TB, d), lambda i, c: (i, 0), pipeline_mode=once),
                  pl.BlockSpec((1, 6, d), lambda i, c: (i // tiles_per_seq, 0, 0)),
                  pl.BlockSpec((1, 2, d), lambda i, c: (i // tiles_per_seq, 0, 0)),
                  pl.BlockSpec((1, d), lambda i, c: (0, 0))],
        out_specs=pl.BlockSpec((FFN_TB, d), lambda i, c: (i, 0)),
        out_shape=jax.ShapeDtypeStruct((t, d), F32),
        scratch_shapes=[pltpu.VMEM((d, FFN_TB), F32), pltpu.VMEM((FFN_EC, FFN_TB), F32),
                        pltpu.VMEM((FFN_EC, FFN_TB), F32)],
        compiler_params=pltpu.CompilerParams(dimension_semantics=("arbitrary", "arbitrary"),
                                             vmem_limit_bytes=FFN_VMEM_LIMIT),
        name="peer_ffn",
    )(h2, u, u, u, vt, *tabs, x1, mod3, modo3, g_final.reshape(1, d))


def kernel(x, c, w_ada, b_ada, g_mix, w_in, conv_w, conv_b, b_igate, b_fgate, lambda_q1, lambda_k1,
           lambda_q2, lambda_k2, da_norm, ml_norm, w_out, g_ffn, w_pq, sub_keys, peer_u, peer_v,
           w_ada_final, b_ada_final, g_final):
    batch, seq, d = x.shape
    depth = w_ada.shape[0]
    t = batch * seq
    da_heads = d // 256
    ml_heads = d // 512
    att_cols = 3 * da_heads * DA_V
    rope_cols = 2 * da_heads * DA_V
    main_cols = att_cols + 2 * ml_heads * ML_QK + 2 * ml_heads * ML_V
    assert batch <= SUBLANES and seq % PROJ_TM == 0 and seq % ATT_TQ == 0 and seq % ML_CHUNK == 0
    assert w_in.shape[2] == main_cols + 2 * ml_heads

    c8 = jnp.zeros((SUBLANES, d), F32).at[:batch].set(c.astype(F32))
    modo3 = _ada(c8, w_ada_final, b_ada_final)[:batch].reshape(batch, 2, d)
    xt = x.reshape(t, d)

    for l in range(depth):
        mod3 = _ada(c8, w_ada[l], b_ada[l])[:batch].reshape(batch, 6, d)
        lam_init = 0.8 - 0.6 * math.exp(-0.3 * l)

        w_in_t = jnp.swapaxes(w_in[l], 0, 1).astype(BF16)
        w_gate_t = jnp.zeros((LANES, d), BF16).at[:2 * ml_heads].set(w_in_t[main_cols:])
        pa, pm, gates = _proj(xt, mod3, g_mix[l], w_in_t, w_gate_t, seq, main_cols, att_cols, rope_cols)

        lam4 = jnp.stack([lambda_q1[l], lambda_k1[l], lambda_q2[l], lambda_k2[l]]).astype(F32)
        oa = _attn(pa, lam4, da_norm[l], batch, seq, da_heads, lam_init)

        om = _mlstm(pm, gates, conv_w[l], conv_b[l], b_igate[l], b_fgate[l], ml_norm[l], batch, seq, ml_heads)

        x1, h2 = _outproj(oa, om, w_out[l].astype(BF16), xt, mod3, g_ffn[l], seq)

        *tabs, u_b, vt_b = _peer_sel(h2, w_pq[l].astype(BF16), sub_keys[l].astype(BF16), peer_u[l], peer_v[l])
        assert depth == 1
        xt = _peer_ffn(h2, u_b, vt_b, tabs, x1, mod3, modo3, g_final, seq)

    return xt.reshape(batch, seq, d)
```

```python
import functools
import math

import numpy as np
import jax
import jax.numpy as jnp
from jax import lax
from jax.experimental import pallas as pl
from jax.experimental.pallas import tpu as pltpu

F32 = jnp.float32
BF16 = jnp.bfloat16

DA_QK = 64
DA_V = 128
ML_QK = 128
ML_V = 256
CONV_W = 4
GATE_CAP = 15.0
ROPE_THETA = 10000.0
PEER_HEADS = 8
PEER_NKEYS = 128
PEER_TOPK = 16
EPS = 1e-6

LANES = 128
SUBLANES = 8
VMEM_LIMIT = 56 * 1024 * 1024
FFN_VMEM_LIMIT = 60 * 1024 * 1024

ADA_TN = 1024
PROJ_TM = 256
PROJ_TN = 512
ATT_TQ = 512
ATT_ONES_ROWS = 16
ATT_HEADS_PER_STEP = 4
ML_CHUNK = 256
OUT_TM = 256
SEL_TM = 512
FFN_TB = 512
FFN_EC = 512

NEG_INF = float("-inf")


def _cparams(sem):
    return pltpu.CompilerParams(dimension_semantics=sem, vmem_limit_bytes=VMEM_LIMIT)


def _rms(x, g):
    return x * lax.rsqrt(jnp.mean(x * x, axis=-1, keepdims=True) + EPS) * g


def _ada_kernel(c_ref, w_ref, b_ref, o_ref):
    c = c_ref[...]
    cs = c * jax.nn.sigmoid(c)
    o_ref[...] = jnp.dot(cs, w_ref[...], preferred_element_type=F32,
                         precision=lax.Precision.HIGHEST) + b_ref[...]


def _ada(c8, w, b):
    d, n = w.shape
    return pl.pallas_call(
        _ada_kernel,
        grid=(n // ADA_TN,),
        in_specs=[pl.BlockSpec((SUBLANES, d), lambda j: (0, 0)),
                  pl.BlockSpec((d, ADA_TN), lambda j: (0, j)),
                  pl.BlockSpec((1, ADA_TN), lambda j: (0, j))],
        out_specs=pl.BlockSpec((SUBLANES, ADA_TN), lambda j: (0, j)),
        out_shape=jax.ShapeDtypeStruct((SUBLANES, n), F32),
        compiler_params=_cparams(("arbitrary",)),
        name="ada",
    )(c8, w, b.reshape(1, n))


def _proj_kernel(n_rope, n_att, n_tiles, x_ref, mod_ref, g_ref, w_ref, wg_ref, cos_ref, sa_ref, sb_ref,
                 pa_ref, pm_ref, gate_ref):
    nt = (((1,), (1,)), ((), ()))
    tn = PROJ_TN
    y = _rms(x_ref[...], g_ref[...])
    hb = (y * (1.0 + mod_ref[0, 1:2, :]) + mod_ref[0, 0:1, :]).astype(BF16)
    gate_ref[...] = lax.dot_general(hb, wg_ref[...], nt, preferred_element_type=F32)
    cos, sa, sb = cos_ref[...], sa_ref[...], sb_ref[...]
    for j in range(n_tiles):
        acc = lax.dot_general(hb, w_ref[j * tn:(j + 1) * tn, :], nt, preferred_element_type=F32)
        if j < n_rope:
            for g in range(tn // LANES):
                xg = acc[:, g * LANES:(g + 1) * LANES]
                r = (xg * cos + pltpu.roll(xg, LANES - DA_QK // 2, 1) * sa
                     + pltpu.roll(xg, DA_QK // 2, 1) * sb)
                pa_ref[:, j * tn + g * LANES:j * tn + (g + 1) * LANES] = r.astype(BF16)
        elif j < n_att:
            pa_ref[:, j * tn:(j + 1) * tn] = acc.astype(BF16)
        else:
            pm_ref[:, (j - n_att) * tn:(j - n_att + 1) * tn] = acc


def _rope_tables(seq):
    half = DA_QK // 2
    inv = ROPE_THETA ** (-jnp.arange(half, dtype=F32) / half)
    ang = jnp.arange(seq, dtype=F32)[:, None] * inv[None, :]
    lane = np.arange(LANES)
    first = (lane % DA_QK) < half
    cos = jnp.cos(ang)[:, lane % half]
    sin = jnp.sin(ang)[:, lane % half]
    sa = jnp.where(first[None, :], -sin, 0.0)
    sb = jnp.where(first[None, :], 0.0, sin)
    return cos, sa, sb


def _proj(x2, mod3, g, w_main_t, w_gate_t, seq, n, n_att_cols, n_rope_cols):
    t, d = x2.shape
    assert n % PROJ_TN == 0 and n <= w_main_t.shape[0]
    n_att = n_att_cols // PROJ_TN
    n_rope = n_rope_cols // PROJ_TN
    n_tiles = n // PROJ_TN
    tiles_per_seq = seq // PROJ_TM
    cos, sa, sb = _rope_tables(seq)
    tab_spec = pl.BlockSpec((PROJ_TM, LANES), lambda i: (i % tiles_per_seq, 0))
    once = pl.Buffered(1)
    return pl.pallas_call(
        functools.partial(_proj_kernel, n_rope, n_att, n_tiles),
        grid=(t // PROJ_TM,),
        in_specs=[pl.BlockSpec((PROJ_TM, d), lambda i: (i, 0)),
                  pl.BlockSpec((1, 6, d), lambda i: (i // tiles_per_seq, 0, 0)),
                  pl.BlockSpec((1, d), lambda i: (0, 0)),
                  pl.BlockSpec(w_main_t.shape, lambda i: (0, 0), pipeline_mode=once),
                  pl.BlockSpec((LANES, d), lambda i: (0, 0), pipeline_mode=once),
                  tab_spec, tab_spec, tab_spec],
        out_specs=[pl.BlockSpec((PROJ_TM, n_att_cols), lambda i: (i, 0)),
                   pl.BlockSpec((PROJ_TM, n - n_att_cols), lambda i: (i, 0)),
                   pl.BlockSpec((PROJ_TM, LANES), lambda i: (i, 0))],
        out_shape=[jax.ShapeDtypeStruct((t, n_att_cols), BF16),
                   jax.ShapeDtypeStruct((t, n - n_att_cols), F32),
                   jax.ShapeDtypeStruct((t, LANES), F32)],
        compiler_params=_cparams(("arbitrary",)),
        name="proj",
    )(x2, mod3, g.reshape(1, d), w_main_t, w_gate_t, cos, sa, sb)


def _attn_kernel(lam_init, lam_ref, q_ref, k_ref, v_ref, nrm_ref, o_ref, vt_scr):
    i = pl.program_id(2)
    tq = ATT_TQ
    nhs = ATT_HEADS_PER_STEP
    seq = k_ref.shape[0]
    nq = seq // tq

    @pl.when(i == 0)
    def _():
        for hh in range(nhs):
            vt_scr[hh, :DA_V, :] = v_ref[:, hh * DA_V:(hh + 1) * DA_V].astype(F32).T.astype(BF16)
            vt_scr[hh, DA_V:, :] = jnp.ones((ATT_ONES_ROWS, seq), BF16)

    lv = lam_ref[...]
    lam = (jnp.exp(jnp.sum(lv[0:1] * lv[1:2], axis=-1, keepdims=True))
           - jnp.exp(jnp.sum(lv[2:3] * lv[3:4], axis=-1, keepdims=True)) + lam_init)
    lane = lax.broadcasted_iota(jnp.int32, (tq, LANES), 1)
    nt = (((1,), (1,)), ((), ()))

    def block(qi):
        below = qi * tq
        kpos = lax.broadcasted_iota(jnp.int32, (tq, 2 * tq), 0)
        col = lax.broadcasted_iota(jnp.int32, (tq, 2 * tq), 1)
        causal = kpos <= jnp.where(col >= tq, col - tq, col)
        for hh in range(nhs):
            hl = slice(hh * LANES, (hh + 1) * LANES)
            q = q_ref[:, hl] * (DA_QK ** -0.5)
            zero = jnp.zeros_like(q)
            q2 = jnp.concatenate([jnp.where(lane < DA_QK, q, zero), jnp.where(lane >= DA_QK, q, zero)], axis=0)
            s_diag = lax.dot_general(k_ref[below:below + tq, hl], q2, nt, preferred_element_type=F32)
            s_diag = jnp.where(causal, s_diag, NEG_INF)
            m = jnp.max(s_diag, axis=0, keepdims=True)
            if qi > 0:
                s_low = lax.dot_general(k_ref[0:below, hl], q2, nt, preferred_element_type=F32)
                m = jnp.maximum(m, jnp.max(s_low, axis=0, keepdims=True))
            acc = jnp.dot(vt_scr[hh, :, below:below + tq], jnp.exp(s_diag - m).astype(BF16),
                          preferred_element_type=F32)
            if qi > 0:
                acc = acc + jnp.dot(vt_scr[hh, :, 0:below], jnp.exp(s_low - m).astype(BF16),
                                    preferred_element_type=F32)
            o2 = acc[:DA_V, :] / acc[DA_V:DA_V + 1, :]
            o = (o2[:, :tq] - lam * o2[:, tq:]).T
            o_ref[:, hh * DA_V:(hh + 1) * DA_V] = (_rms(o, nrm_ref[...]) * (1.0 - lam_init)).astype(o_ref.dtype)

    for qi in range(nq):
        pl.when(i == qi)(functools.partial(block, qi))


def _attn(pa, lam4, da_norm, batch, seq, heads, lam_init):
    t = pa.shape[0]
    nq = seq // ATT_TQ
    nhs = ATT_HEADS_PER_STEP
    groups = heads // nhs
    wide = nhs * LANES
    return pl.pallas_call(
        functools.partial(_attn_kernel, lam_init),
        grid=(batch, groups, nq),
        in_specs=[pl.BlockSpec((4, DA_QK), lambda b, g, i: (0, 0)),
                  pl.BlockSpec((ATT_TQ, wide), lambda b, g, i: (b * nq + i, g)),
                  pl.BlockSpec((seq, wide), lambda b, g, i: (b, groups + g)),
                  pl.BlockSpec((seq, wide), lambda b, g, i: (b, 2 * groups + g)),
                  pl.BlockSpec((1, DA_V), lambda b, g, i: (0, 0))],
        out_specs=pl.BlockSpec((ATT_TQ, wide), lambda b, g, i: (b * nq + i, g)),
        out_shape=jax.ShapeDtypeStruct((t, heads * DA_V), BF16),
        scratch_shapes=[pltpu.VMEM((nhs, DA_V + ATT_ONES_ROWS, seq), BF16)],
        compiler_params=_cparams(("arbitrary", "arbitrary", "arbitrary")),
        name="attn",
    )(lam4, pa, pa, pa, da_norm.reshape(1, DA_V))


def _softcap(x):
    return GATE_CAP * jnp.tanh(x / GATE_CAP)


def _dwconv_silu(x, w, b):
    seq = x.shape[0]
    row = lax.broadcasted_iota(jnp.int32, x.shape, 0)
    y = x * w[CONV_W - 1:CONV_W]
    for s in range(1, CONV_W):
        xs = jnp.where(row >= s, pltpu.roll(x, s, 0), 0.0)
        y = y + xs * w[CONV_W - 1 - s:CONV_W - s]
    y = y + b
    return y * jax.nn.sigmoid(y)


def _mlstm_kernel(bi_ref, bf_ref, q_ref, k_ref, v_ref, og_ref, cwq_ref, cwk_ref, cbq_ref, cbk_ref,
                  gc_ref, nrm_ref, o_ref,
                  q_scr, k_scr, lic_scr, lfc_scr, lir_scr, lfr_scr, gt_scr, bc_scr, br_scr, c_scr, m_scr):
    hd = pl.program_id(1)
    L = ML_CHUNK
    seq = q_ref.shape[0]
    nc = seq // L
    bi = bi_ref[hd]
    bf = bf_ref[hd]

    q_scr[...] = _dwconv_silu(q_ref[...], cwq_ref[...], cbq_ref[...]).astype(BF16)
    k_scr[...] = _dwconv_silu(k_ref[...], cwk_ref[...], cbk_ref[...]) * (ML_QK ** -0.5)

    gc = gc_ref[...]
    glane = lax.broadcasted_iota(jnp.int32, gc.shape, 1)
    gi = jnp.sum(jnp.where(glane == hd, gc, 0.0), axis=1, keepdims=True)
    gf = jnp.sum(jnp.where(glane == pl.num_programs(1) + hd, gc, 0.0), axis=1, keepdims=True)
    lic_scr[...] = _softcap(gi + bi)
    lfc_scr[...] = jax.nn.log_sigmoid(_softcap(gf + bf))
    gt_scr[...] = gc.T
    li_row = _softcap(gt_scr[pl.ds(hd, 1), :] + bi)
    lf_row = jax.nn.log_sigmoid(_softcap(gt_scr[pl.ds(pl.num_programs(1) + hd, 1), :] + bf))
    for c in range(nc):
        lir_scr[c:c + 1, :] = li_row[:, c * L:(c + 1) * L]
        lfr_scr[c:c + 1, :] = lf_row[:, c * L:(c + 1) * L]

    c_scr[...] = jnp.zeros(c_scr.shape, F32)
    m_scr[...] = jnp.zeros(m_scr.shape, F32)

    r_i = lax.broadcasted_iota(jnp.int32, (L, L), 0)
    c_i = lax.broadcasted_iota(jnp.int32, (L, L), 1)
    tril = (c_i <= r_i)
    tril_b = tril.astype(BF16)
    triu_b = (r_i <= c_i).astype(BF16)
    ones_col = (lax.broadcasted_iota(jnp.int32, (L, LANES), 1) == 0).astype(BF16)
    nrm = nrm_ref[...]

    def split3(x):
        hi = x.astype(BF16)
        r = x - hi.astype(F32)
        mid = r.astype(BF16)
        return hi, mid, (r - mid.astype(F32)).astype(BF16)

    for c in range(nc):
        b_c = None
        for piece in split3(lfc_scr[c * L:(c + 1) * L, :]):
            term = jnp.dot(tril_b, jnp.broadcast_to(piece, (L, L)), preferred_element_type=F32)
            b_c = term if b_c is None else b_c + term
        b_r = None
        for piece in split3(lfr_scr[c:c + 1, :]):
            term = jnp.dot(jnp.broadcast_to(piece, (L, L)), triu_b, preferred_element_type=F32)
            b_r = term if b_r is None else b_r + term
        bc_scr[c] = b_c
        br_scr[c] = b_r

    def body(c):
        sl = slice(c * L, (c + 1) * L)
        qc = q_scr[sl, :]
        kc = k_scr[sl, :]
        vc = v_ref[sl, :].astype(BF16)
        lic = lic_scr[sl, :]
        lir = lir_scr[c:c + 1, :]
        m = m_scr[...]

        b_c = bc_scr[c]
        b_r = br_scr[c]
        dmat = jnp.where(tril, b_c - b_r + lir, NEG_INF)
        bcol = b_c[:, 0:1]
        inter = bcol + m
        m_j = jnp.maximum(inter, jnp.max(dmat, axis=-1, keepdims=True))
        w_intra = jnp.exp(dmat - m_j)
        w_inter = jnp.exp(inter - m_j)

        sqk = lax.dot_general(qc, kc.astype(BF16), (((1,), (1,)), ((), ())), preferred_element_type=F32)
        qk = sqk * w_intra
        q_c = jnp.dot(qc, c_scr[...].astype(BF16), preferred_element_type=F32)
        num = w_inter * q_c[:, :ML_V] + jnp.dot(qk.astype(BF16), vc, preferred_element_type=F32)
        den = w_inter * q_c[:, ML_V:ML_V + 1] + jnp.sum(qk, axis=-1, keepdims=True)
        hh = num / jnp.maximum(jnp.abs(den), jnp.exp(-m_j))
        og = og_ref[sl, :]
        o_ref[sl, :] = (_rms(hh, nrm) * jax.nn.sigmoid(og)).astype(o_ref.dtype)

        b_l = b_c[L - 1:L, 0:1]
        logw = b_l - bcol + lic
        m_new = jnp.maximum(b_l + m, jnp.max(logw, axis=0, keepdims=True))
        decay = jnp.exp(b_l + m - m_new)
        ws = jnp.exp(logw - m_new)
        kw = (kc * ws).astype(BF16)
        vext = jnp.concatenate([vc, ones_col], axis=1)
        upd = lax.dot_general(kw, vext, (((0,), (0,)), ((), ())), preferred_element_type=F32)
        c_scr[...] = decay * c_scr[...] + upd
        m_scr[...] = m_new

    for c in range(nc):
        body(c)


def _mlstm(pm, gates, conv_w, conv_b, b_ig, b_fg, ml_norm, batch, seq, heads):
    t = pm.shape[0]
    nc = seq // ML_CHUNK
    qk_w = heads * ML_QK
    vblk = 2 * qk_w // ML_V
    smem = pl.BlockSpec(memory_space=pltpu.SMEM)
    return pl.pallas_call(
        _mlstm_kernel,
        grid=(batch, heads),
        in_specs=[smem, smem,
                  pl.BlockSpec((seq, ML_QK), lambda b, h: (b, h)),
                  pl.BlockSpec((seq, ML_QK), lambda b, h: (b, heads + h)),
                  pl.BlockSpec((seq, ML_V), lambda b, h: (b, vblk + h)),
                  pl.BlockSpec((seq, ML_V), lambda b, h: (b, vblk + heads + h)),
                  pl.BlockSpec((CONV_W, ML_QK), lambda b, h: (0, h)),
                  pl.BlockSpec((CONV_W, ML_QK), lambda b, h: (0, heads + h)),
                  pl.BlockSpec((1, ML_QK), lambda b, h: (0, h)),
                  pl.BlockSpec((1, ML_QK), lambda b, h: (0, heads + h)),
                  pl.BlockSpec((seq, LANES), lambda b, h: (b, 0)),
                  pl.BlockSpec((1, ML_V), lambda b, h: (0, 0))],
        out_specs=pl.BlockSpec((seq, ML_V), lambda b, h: (b, h)),
        out_shape=jax.ShapeDtypeStruct((t, heads * ML_V), BF16),
        scratch_shapes=[pltpu.VMEM((seq, ML_QK), BF16), pltpu.VMEM((seq, ML_QK), F32),
                        pltpu.VMEM((seq, 1), F32), pltpu.VMEM((seq, 1), F32),
                        pltpu.VMEM((nc, ML_CHUNK), F32), pltpu.VMEM((nc, ML_CHUNK), F32),
                        pltpu.VMEM((LANES, seq), F32),
                        pltpu.VMEM((nc, ML_CHUNK, ML_CHUNK), F32), pltpu.VMEM((nc, ML_CHUNK, ML_CHUNK), F32),
                        pltpu.VMEM((ML_QK, ML_V + LANES), F32), pltpu.VMEM((1, 1), F32)],
        compiler_params=_cparams(("arbitrary", "arbitrary")),
        name="mlstm",
    )(b_ig, b_fg, pm, pm, pm, pm, conv_w, conv_w, conv_b.reshape(1, -1), conv_b.reshape(1, -1),
      gates, ml_norm.reshape(1, ML_V))


def _outproj_kernel(oa_ref, om_ref, w_ref, x_ref, mod_ref, g_ref, x1_ref, h2_ref):
    ka = oa_ref.shape[1]
    mixed = (jnp.dot(oa_ref[...], w_ref[:ka, :], preferred_element_type=F32)
             + jnp.dot(om_ref[...], w_ref[ka:, :], preferred_element_type=F32))
    x1 = x_ref[...] + mod_ref[0, 2:3, :] * mixed
    x1_ref[...] = x1
    h2 = _rms(x1, g_ref[...]) * (1.0 + mod_ref[0, 4:5, :]) + mod_ref[0, 3:4, :]
    h2_ref[...] = h2.astype(BF16)


def _outproj(oa, om, w_out, x2, mod3, g_ffn, seq):
    t, d = x2.shape
    tiles_per_seq = seq // OUT_TM
    return pl.pallas_call(
        _outproj_kernel,
        grid=(t // OUT_TM,),
        in_specs=[pl.BlockSpec((OUT_TM, oa.shape[1]), lambda i: (i, 0)),
                  pl.BlockSpec((OUT_TM, om.shape[1]), lambda i: (i, 0)),
                  pl.BlockSpec(w_out.shape, lambda i: (0, 0)),
                  pl.BlockSpec((OUT_TM, d), lambda i: (i, 0)),
                  pl.BlockSpec((1, 6, d), lambda i: (i // tiles_per_seq, 0, 0)),
                  pl.BlockSpec((1, d), lambda i: (0, 0))],
        out_specs=[pl.BlockSpec((OUT_TM, d), lambda i: (i, 0)),
                   pl.BlockSpec((OUT_TM, d), lambda i: (i, 0))],
        out_shape=[jax.ShapeDtypeStruct((t, d), F32), jax.ShapeDtypeStruct((t, d), BF16)],
        compiler_params=_cparams(("arbitrary",)),
        name="outproj",
    )(oa, om, w_out, x2, mod3, g_ffn.reshape(1, d))


_CAND_ROWS = 16 + 8 + 6 * 8 + 8


def _pick_max(work, iota, exact):
    mx = jnp.max(work, axis=0, keepdims=True)
    if not exact:
        return mx, work == mx
    first = jnp.min(jnp.where(work == mx, iota, float(work.shape[0])), axis=0, keepdims=True)
    return mx, iota == first


def _rank16(s, top_scr, exact, want_rank=True):
    iota = lax.broadcasted_iota(jnp.int32, s.shape, 0).astype(F32) if exact else None
    rank = jnp.full(s.shape, float(PEER_TOPK), F32) if want_rank else None
    for r in range(PEER_TOPK):
        mx, sel = _pick_max(s, iota, exact)
        if want_rank:
            rank = jnp.where(sel, float(r), rank)
        s = jnp.where(sel, NEG_INF, s)
        top_scr[r:r + 1, :] = mx
    return rank, s


def _selection_tables(s1, s2, t1_scr, t2_scr, exact):
    k = PEER_TOPK
    r1, left1 = _rank16(s1, t1_scr, exact, want_rank=exact)
    r2, _ = _rank16(s2, t2_scr, exact)
    t1 = t1_scr[...]
    t2 = t2_scr[...]
    tm = s1.shape[1]

    brow = lax.broadcasted_iota(jnp.int32, (SUBLANES, tm), 0)
    pieces = [t1[0:1] + t2, t1[1:2] + t2[0:SUBLANES]]
    for a in range(2, SUBLANES):
        pieces.append(jnp.where(brow < k // (a + 1), t1[a:a + 1] + t2[0:SUBLANES], NEG_INF))
    pieces.append(t1[SUBLANES:] + t2[0:1])
    cand = jnp.concatenate(pieces, axis=0)
    cmax = cand[0:1]

    work = cand
    iota = lax.broadcasted_iota(jnp.int32, cand.shape, 0).astype(F32) if exact else None
    for _ in range(k):
        _, sel = _pick_max(work, iota, exact)
        work = jnp.where(sel, NEG_INF, work)
    taken = jnp.where(work != cand, 1.0, 0.0)
    z = jnp.sum(taken * jnp.exp(cand - cmax), axis=0, keepdims=True)

    n_rows = [jnp.sum(taken[0:16], axis=0, keepdims=True)]
    for a in range(1, SUBLANES):
        lo = 16 + (a - 1) * SUBLANES
        n_rows.append(jnp.sum(taken[lo:lo + SUBLANES], axis=0, keepdims=True))
    base = 16 + 7 * SUBLANES
    for a in range(SUBLANES, k):
        n_rows.append(taken[base + a - SUBLANES:base + a - SUBLANES + 1])
    n1 = jnp.zeros(s1.shape, F32)
    for a in range(k):
        hit = (r1 == float(a)) if exact else (s1 == t1[a:a + 1])
        n1 = jnp.where(hit, n_rows[a], n1)

    extra = None
    if not exact:
        def count(x):
            return jnp.sum(x, axis=0, keepdims=True) - float(k)
        extra = (count(jnp.where(left1 != s1, 1.0, 0.0)) + count(jnp.where(r2 < float(k), 1.0, 0.0))
                 + count(taken))
    return (r2, jnp.exp(s2 - t2[0:1]), n1, jnp.exp(s1 - t1[0:1]) / z), extra


def _peer_sel_kernel(h2_ref, w_ref, keys_ref, u_ref, v_ref, r2_ref, e2_ref, n1_ref, e1_ref, ub_ref, vtb_ref,
                     t1_scr, t2_scr):
    ub_ref[...] = u_ref[...].astype(BF16)
    vtb_ref[...] = v_ref[...].T.astype(BF16)

    half = PEER_NKEYS
    q = jnp.dot(h2_ref[...], w_ref[...], preferred_element_type=F32).astype(BF16)
    nt = (((1,), (1,)), ((), ()))
    s1 = lax.dot_general(keys_ref[0, 0], q[:, :half], nt, preferred_element_type=F32)
    s2 = lax.dot_general(keys_ref[0, 1], q[:, half:], nt, preferred_element_type=F32)

    def store(tables):
        for ref, val in zip((r2_ref, e2_ref, n1_ref, e1_ref), tables):
            ref[0] = val.astype(ref.dtype)

    tables, extra = _selection_tables(s1, s2, t1_scr, t2_scr, exact=False)
    store(tables)

    @pl.when(jnp.max(extra) > 0.0)
    def _():
        store(_selection_tables(s1, s2, t1_scr, t2_scr, exact=True)[0])


def _peer_sel(h2, w_pq, keys, peer_u, peer_v):
    t, d = h2.shape
    ne = peer_u.shape[0]
    nh = PEER_HEADS
    qd = 2 * PEER_NKEYS
    steps = (t // SEL_TM) * nh
    rows = ne // steps
    assert rows * steps == ne and rows % LANES == 0
    tab = jax.ShapeDtypeStruct((nh, PEER_NKEYS, t), F32)
    tab16 = jax.ShapeDtypeStruct((nh, PEER_NKEYS, t), BF16)
    tab_spec = pl.BlockSpec((1, PEER_NKEYS, SEL_TM), lambda i, h: (h, 0, i))
    return pl.pallas_call(
        _peer_sel_kernel,
        grid=(t // SEL_TM, nh),
        in_specs=[pl.BlockSpec((SEL_TM, d), lambda i, h: (i, 0)),
                  pl.BlockSpec((d, qd), lambda i, h: (0, h)),
                  pl.BlockSpec((1, 2, PEER_NKEYS, PEER_NKEYS), lambda i, h: (h, 0, 0, 0)),
                  pl.BlockSpec((rows, d), lambda i, h: (i * nh + h, 0)),
                  pl.BlockSpec((rows, d), lambda i, h: (i * nh + h, 0))],
        out_specs=[tab_spec, tab_spec, tab_spec, tab_spec,
                   pl.BlockSpec((rows, d), lambda i, h: (i * nh + h, 0)),
                   pl.BlockSpec((d, rows), lambda i, h: (0, i * nh + h))],
        out_shape=[tab16, tab16, tab, tab,
                   jax.ShapeDtypeStruct((ne, d), BF16), jax.ShapeDtypeStruct((d, ne), BF16)],
        scratch_shapes=[pltpu.VMEM((PEER_TOPK, SEL_TM), F32), pltpu.VMEM((PEER_TOPK, SEL_TM), F32)],
        compiler_params=_cparams(("arbitrary", "arbitrary")),
        name="peer_sel",
    )(h2, w_pq, keys, peer_u, peer_v)


def _peer_ffn_kernel(h2_ref, u0_ref, uodd_ref, unext_ref, vt_ref, r2_ref, e2_ref, n1_ref, e1_ref, x1_ref,
                     mod_ref, modo_ref, g_ref, o_ref, acc_scr, a0_scr, a1_scr):
    c = pl.program_id(1)
    nk = PEER_NKEYS
    per = FFN_EC // nk
    nt = (((1,), (1,)), ((), ()))
    zero = jnp.zeros((), BF16)

    def preact(u_ref, dst_scr):
        dst_scr[...] = lax.dot_general(u_ref[...], h2_ref[...], nt, preferred_element_type=F32)

    def weights(a_scr, chunk):
        parts = []
        for ii in range(per):
            i = chunk * per + ii
            a = a_scr[ii * nk:(ii + 1) * nk, :]
            gsum = None
            for hd in range(PEER_HEADS):
                n1 = n1_ref[hd, pl.ds(i, 1), :].astype(BF16)
                e1 = e1_ref[hd, pl.ds(i, 1), :].astype(BF16)
                g = jnp.where(r2_ref[hd] < n1, e2_ref[hd], zero) * e1
                gsum = g if gsum is None else gsum + g
            act = 0.5 * a * (1.0 + lax.erf(a * (2.0 ** -0.5)))
            parts.append(gsum * act.astype(BF16))
        return jnp.concatenate(parts, axis=0)

    def accumulate(w, col0):
        acc_scr[...] += jnp.dot(vt_ref[:, col0:col0 + FFN_EC], w, preferred_element_type=F32)

    @pl.when(c == 0)
    def _():
        acc_scr[...] = jnp.zeros(acc_scr.shape, F32)
        preact(u0_ref, a0_scr)

    preact(uodd_ref, a1_scr)
    accumulate(weights(a0_scr, 2 * c), 0)
    preact(unext_ref, a0_scr)
    accumulate(weights(a1_scr, 2 * c + 1), FFN_EC)

    @pl.when(c == pl.num_programs(1) - 1)
    def _():
        y = acc_scr[...].T
        x2 = x1_ref[...] + mod_ref[0, 5:6, :] * y
        o_ref[...] = _rms(x2, g_ref[...]) * (1.0 + modo_ref[0, 1:2, :]) + modo_ref[0, 0:1, :]


def _peer_ffn(h2, u, vt, tabs, x1, mod3, modo3, g_final, seq):
    t, d = h2.shape
    ne = u.shape[0]
    tiles_per_seq = seq // FFN_TB
    tab_spec = pl.BlockSpec((PEER_HEADS, PEER_NKEYS, FFN_TB), lambda i, c: (0, 0, i))
    nchunk = ne // FFN_EC
    once = pl.Buffered(1)
    return pl.pallas_call(
        _peer_ffn_kernel,
        grid=(t // FFN_TB, nchunk // 2),
        in_specs=[pl.BlockSpec((FFN_TB, d), lambda i, c: (i, 0)),
                  pl.BlockSpec((FFN_EC, d), lambda i, c: (0, 0), pipeline_mode=once),
                  pl.BlockSpec((FFN_EC, d), lambda i, c: (2 * c + 1, 0)),
                  pl.BlockSpec((FFN_EC, d), lambda i, c: (jnp.minimum(2 * c + 2, nchunk - 1), 0)),
                  pl.BlockSpec((d, 2 * FFN_EC), lambda i, c: (0, c)),
                  tab_spec, tab_spec, tab_spec, tab_spec,
                  pl.BlockSpec((FFN_TB, d), lambda i, c: (i, 0), pipeline_mode=once),
                  pl.BlockSpec((1, 6, d), lambda i, c: (i // tiles_per_seq, 0, 0)),
                  pl.BlockSpec((1, 2, d), lambda i, c: (i // tiles_per_seq, 0, 0)),
                  pl.BlockSpec((1, d), lambda i, c: (0, 0))],
        out_specs=pl.BlockSpec((FFN_TB, d), lambda i, c: (i, 0)),
        out_shape=jax.ShapeDtypeStruct((t, d), F32),
        scratch_shapes=[pltpu.VMEM((d, FFN_TB), F32), pltpu.VMEM((FFN_EC, FFN_TB), F32),
                        pltpu.VMEM((FFN_EC, FFN_TB), F32)],
        compiler_params=pltpu.CompilerParams(dimension_semantics=("arbitrary", "arbitrary"),
                                             vmem_limit_bytes=FFN_VMEM_LIMIT),
        name="peer_ffn",
    )(h2, u, u, u, vt, *tabs, x1, mod3, modo3, g_final.reshape(1, d))


def kernel(x, c, w_ada, b_ada, g_mix, w_in, conv_w, conv_b, b_igate, b_fgate, lambda_q1, lambda_k1,
           lambda_q2, lambda_k2, da_norm, ml_norm, w_out, g_ffn, w_pq, sub_keys, peer_u, peer_v,
           w_ada_final, b_ada_final, g_final):
    batch, seq, d = x.shape
    depth = w_ada.shape[0]
    t = batch * seq
    da_heads = d // 256
    ml_heads = d // 512
    att_cols = 3 * da_heads * DA_V
    rope_cols = 2 * da_heads * DA_V
    main_cols = att_cols + 2 * ml_heads * ML_QK + 2 * ml_heads * ML_V
    assert batch <= SUBLANES and seq % PROJ_TM == 0 and seq % ATT_TQ == 0 and seq % ML_CHUNK == 0
    assert w_in.shape[2] == main_cols + 2 * ml_heads

    c8 = jnp.zeros((SUBLANES, d), F32).at[:batch].set(c.astype(F32))
    modo3 = _ada(c8, w_ada_final, b_ada_final)[:batch].reshape(batch, 2, d)
    xt = x.reshape(t, d)

    for l in range(depth):
        mod3 = _ada(c8, w_ada[l], b_ada[l])[:batch].reshape(batch, 6, d)
        lam_init = 0.8 - 0.6 * math.exp(-0.3 * l)

        w_in_t = jnp.swapaxes(w_in[l], 0, 1).astype(BF16)
        w_gate_t = jnp.zeros((LANES, d), BF16).at[:2 * ml_heads].set(w_in_t[main_cols:])
        pa, pm, gates = _proj(xt, mod3, g_mix[l], w_in_t, w_gate_t, seq, main_cols, att_cols, rope_cols)

        lam4 = jnp.stack([lambda_q1[l], lambda_k1[l], lambda_q2[l], lambda_k2[l]]).astype(F32)
        oa = _attn(pa, lam4, da_norm[l], batch, seq, da_heads, lam_init)

        om = _mlstm(pm, gates, conv_w[l], conv_b[l], b_igate[l], b_fgate[l], ml_norm[l], batch, seq, ml_heads)

        x1, h2 = _outproj(oa, om, w_out[l].astype(BF16), xt, mod3, g_ffn[l], seq)

        *tabs, u_b, vt_b = _peer_sel(h2, w_pq[l].astype(BF16), sub_keys[l].astype(BF16), peer_u[l], peer_v[l])
        assert depth == 1
        xt = _peer_ffn(h2, u_b, vt_b, tabs, x1, mod3, modo3, g_final, seq)

    return xt.reshape(batch, seq, d)
```

```python
import functools
import math

import numpy as np
import jax
import jax.numpy as jnp
from jax import lax
from jax.experimental import pallas as pl
from jax.experimental.pallas import tpu as pltpu

F32 = jnp.float32
BF16 = jnp.bfloat16

DA_QK = 64
DA_V = 128
ML_QK = 128
ML_V = 256
CONV_W = 4
GATE_CAP = 15.0
ROPE_THETA = 10000.0
PEER_HEADS = 8
PEER_NKEYS = 128
PEER_TOPK = 16
EPS = 1e-6

LANES = 128
SUBLANES = 8
VMEM_LIMIT = 56 * 1024 * 1024
FFN_VMEM_LIMIT = 60 * 1024 * 1024

ADA_TN = 1024
PROJ_TM = 256
PROJ_TN = 512
ATT_TQ = 512
ATT_ONES_ROWS = 16
ATT_HEADS_PER_STEP = 4
ML_CHUNK = 256
OUT_TM = 512
OUT_SUB = 256
SEL_TM = 512
FFN_TB = 512
FFN_EC = 512

NEG_INF = float("-inf")


def _cparams(sem):
    return pltpu.CompilerParams(dimension_semantics=sem, vmem_limit_bytes=VMEM_LIMIT)


def _rms(x, g):
    return x * lax.rsqrt(jnp.mean(x * x, axis=-1, keepdims=True) + EPS) * g


def _ada_kernel(c_ref, w_ref, b_ref, o_ref):
    c = c_ref[...]
    cs = c * jax.nn.sigmoid(c)
    o_ref[...] = jnp.dot(cs, w_ref[...], preferred_element_type=F32,
                         precision=lax.Precision.HIGHEST) + b_ref[...]


def _ada(c8, w, b):
    d, n = w.shape
    return pl.pallas_call(
        _ada_kernel,
        grid=(n // ADA_TN,),
        in_specs=[pl.BlockSpec((SUBLANES, d), lambda j: (0, 0)),
                  pl.BlockSpec((d, ADA_TN), lambda j: (0, j)),
                  pl.BlockSpec((1, ADA_TN), lambda j: (0, j))],
        out_specs=pl.BlockSpec((SUBLANES, ADA_TN), lambda j: (0, j)),
        out_shape=jax.ShapeDtypeStruct((SUBLANES, n), F32),
        compiler_params=_cparams(("arbitrary",)),
        name="ada",
    )(c8, w, b.reshape(1, n))


def _proj_kernel(n_rope, n_att, n_tiles, x_ref, mod_ref, g_ref, w_ref, wg_ref, cos_ref, sa_ref, sb_ref,
                 pa_ref, pm_ref, gate_ref):
    nt = (((1,), (1,)), ((), ()))
    tn = PROJ_TN
    y = _rms(x_ref[...], g_ref[...])
    hb = (y * (1.0 + mod_ref[0, 1:2, :]) + mod_ref[0, 0:1, :]).astype(BF16)
    gate_ref[...] = lax.dot_general(hb, wg_ref[...], nt, preferred_element_type=F32)
    cos, sa, sb = cos_ref[...], sa_ref[...], sb_ref[...]
    for j in range(n_tiles):
        acc = lax.dot_general(hb, w_ref[j * tn:(j + 1) * tn, :], nt, preferred_element_type=F32)
        if j < n_rope:
            for g in range(tn // LANES):
                xg = acc[:, g * LANES:(g + 1) * LANES]
                r = (xg * cos + pltpu.roll(xg, LANES - DA_QK // 2, 1) * sa
                     + pltpu.roll(xg, DA_QK // 2, 1) * sb)
                pa_ref[:, j * tn + g * LANES:j * tn + (g + 1) * LANES] = r.astype(BF16)
        elif j < n_att:
            pa_ref[:, j * tn:(j + 1) * tn] = acc.astype(BF16)
        else:
            pm_ref[:, (j - n_att) * tn:(j - n_att + 1) * tn] = acc


def _rope_tables(seq):
    half = DA_QK // 2
    inv = ROPE_THETA ** (-jnp.arange(half, dtype=F32) / half)
    ang = jnp.arange(seq, dtype=F32)[:, None] * inv[None, :]
    lane = np.arange(LANES)
    first = (lane % DA_QK) < half
    cos = jnp.cos(ang)[:, lane % half]
    sin = jnp.sin(ang)[:, lane % half]
    sa = jnp.where(first[None, :], -sin, 0.0)
    sb = jnp.where(first[None, :], 0.0, sin)
    return cos, sa, sb


def _proj(x2, mod3, g, w_main_t, w_gate_t, seq, n, n_att_cols, n_rope_cols):
    t, d = x2.shape
    assert n % PROJ_TN == 0 and n <= w_main_t.shape[0]
    n_att = n_att_cols // PROJ_TN
    n_rope = n_rope_cols // PROJ_TN
    n_tiles = n // PROJ_TN
    tiles_per_seq = seq // PROJ_TM
    cos, sa, sb = _rope_tables(seq)
    tab_spec = pl.BlockSpec((PROJ_TM, LANES), lambda i: (i % tiles_per_seq, 0))
    once = pl.Buffered(1)
    return pl.pallas_call(
        functools.partial(_proj_kernel, n_rope, n_att, n_tiles),
        grid=(t // PROJ_TM,),
        in_specs=[pl.BlockSpec((PROJ_TM, d), lambda i: (i, 0)),
                  pl.BlockSpec((1, 6, d), lambda i: (i // tiles_per_seq, 0, 0)),
                  pl.BlockSpec((1, d), lambda i: (0, 0)),
                  pl.BlockSpec(w_main_t.shape, lambda i: (0, 0), pipeline_mode=once),
                  pl.BlockSpec((LANES, d), lambda i: (0, 0), pipeline_mode=once),
                  tab_spec, tab_spec, tab_spec],
        out_specs=[pl.BlockSpec((PROJ_TM, n_att_cols), lambda i: (i, 0)),
                   pl.BlockSpec((PROJ_TM, n - n_att_cols), lambda i: (i, 0)),
                   pl.BlockSpec((PROJ_TM, LANES), lambda i: (i, 0))],
        out_shape=[jax.ShapeDtypeStruct((t, n_att_cols), BF16),
                   jax.ShapeDtypeStruct((t, n - n_att_cols), F32),
                   jax.ShapeDtypeStruct((t, LANES), F32)],
        compiler_params=_cparams(("arbitrary",)),
        name="proj",
    )(x2, mod3, g.reshape(1, d), w_main_t, w_gate_t, cos, sa, sb)


def _attn_kernel(lam_init, lam_ref, q_ref, k_ref, v_ref, nrm_ref, o_ref, vt_scr):
    i = pl.program_id(2)
    tq = ATT_TQ
    nhs = ATT_HEADS_PER_STEP
    seq = k_ref.shape[0]
    nq = seq // tq

    @pl.when(i == 0)
    def _():
        for hh in range(nhs):
            vt_scr[hh, :DA_V, :] = v_ref[:, hh * DA_V:(hh + 1) * DA_V].astype(F32).T.astype(BF16)
            vt_scr[hh, DA_V:, :] = jnp.ones((ATT_ONES_ROWS, seq), BF16)

    lv = lam_ref[...]
    lam = (jnp.exp(jnp.sum(lv[0:1] * lv[1:2], axis=-1, keepdims=True))
           - jnp.exp(jnp.sum(lv[2:3] * lv[3:4], axis=-1, keepdims=True)) + lam_init)
    lane = lax.broadcasted_iota(jnp.int32, (tq, LANES), 1)
    nt = (((1,), (1,)), ((), ()))

    def block(qi):
        below = qi * tq
        kpos = lax.broadcasted_iota(jnp.int32, (tq, 2 * tq), 0)
        col = lax.broadcasted_iota(jnp.int32, (tq, 2 * tq), 1)
        causal = kpos <= jnp.where(col >= tq, col - tq, col)
        for hh in range(nhs):
            hl = slice(hh * LANES, (hh + 1) * LANES)
            q = q_ref[:, hl] * (DA_QK ** -0.5)
            zero = jnp.zeros_like(q)
            q2 = jnp.concatenate([jnp.where(lane < DA_QK, q, zero), jnp.where(lane >= DA_QK, q, zero)], axis=0)
            s_diag = lax.dot_general(k_ref[below:below + tq, hl], q2, nt, preferred_element_type=F32)
            s_diag = jnp.where(causal, s_diag, NEG_INF)
            m = jnp.max(s_diag, axis=0, keepdims=True)
            if qi > 0:
                s_low = lax.dot_general(k_ref[0:below, hl], q2, nt, preferred_element_type=F32)
                m = jnp.maximum(m, jnp.max(s_low, axis=0, keepdims=True))
            acc = jnp.dot(vt_scr[hh, :, below:below + tq], jnp.exp(s_diag - m).astype(BF16),
                          preferred_element_type=F32)
            if qi > 0:
                acc = acc + jnp.dot(vt_scr[hh, :, 0:below], jnp.exp(s_low - m).astype(BF16),
                                    preferred_element_type=F32)
            o2 = acc[:DA_V, :] / acc[DA_V:DA_V + 1, :]
            o = (o2[:, :tq] - lam * o2[:, tq:]).T
            o_ref[:, hh * DA_V:(hh + 1) * DA_V] = (_rms(o, nrm_ref[...]) * (1.0 - lam_init)).astype(o_ref.dtype)

    for qi in range(nq):
        pl.when(i == qi)(functools.partial(block, qi))


def _attn(pa, lam4, da_norm, batch, seq, heads, lam_init):
    t = pa.shape[0]
    nq = seq // ATT_TQ
    nhs = ATT_HEADS_PER_STEP
    groups = heads // nhs
    wide = nhs * LANES
    return pl.pallas_call(
        functools.partial(_attn_kernel, lam_init),
        grid=(batch, groups, nq),
        in_specs=[pl.BlockSpec((4, DA_QK), lambda b, g, i: (0, 0)),
                  pl.BlockSpec((ATT_TQ, wide), lambda b, g, i: (b * nq + i, g)),
                  pl.BlockSpec((seq, wide), lambda b, g, i: (b, groups + g)),
                  pl.BlockSpec((seq, wide), lambda b, g, i: (b, 2 * groups + g)),
                  pl.BlockSpec((1, DA_V), lambda b, g, i: (0, 0))],
        out_specs=pl.BlockSpec((ATT_TQ, wide), lambda b, g, i: (b * nq + i, g)),
        out_shape=jax.ShapeDtypeStruct((t, heads * DA_V), BF16),
        scratch_shapes=[pltpu.VMEM((nhs, DA_V + ATT_ONES_ROWS, seq), BF16)],
        compiler_params=_cparams(("arbitrary", "arbitrary", "arbitrary")),
        name="attn",
    )(lam4, pa, pa, pa, da_norm.reshape(1, DA_V))


def _softcap(x):
    return GATE_CAP * jnp.tanh(x / GATE_CAP)


def _dwconv_silu(x, w, b):
    seq = x.shape[0]
    row = lax.broadcasted_iota(jnp.int32, x.shape, 0)
    y = x * w[CONV_W - 1:CONV_W]
    for s in range(1, CONV_W):
        xs = jnp.where(row >= s, pltpu.roll(x, s, 0), 0.0)
        y = y + xs * w[CONV_W - 1 - s:CONV_W - s]
    y = y + b
    return y * jax.nn.sigmoid(y)


def _mlstm_kernel(bi_ref, bf_ref, q_ref, k_ref, v_ref, og_ref, cwq_ref, cwk_ref, cbq_ref, cbk_ref,
                  gc_ref, nrm_ref, o_ref,
                  q_scr, k_scr, lic_scr, lfc_scr, lir_scr, lfr_scr, gt_scr, bc_scr, br_scr, c_scr, m_scr):
    hd = pl.program_id(1)
    L = ML_CHUNK
    seq = q_ref.shape[0]
    nc = seq // L
    bi = bi_ref[hd]
    bf = bf_ref[hd]

    q_scr[...] = _dwconv_silu(q_ref[...], cwq_ref[...], cbq_ref[...]).astype(BF16)
    k_scr[...] = _dwconv_silu(k_ref[...], cwk_ref[...], cbk_ref[...]) * (ML_QK ** -0.5)

    gc = gc_ref[...]
    glane = lax.broadcasted_iota(jnp.int32, gc.shape, 1)
    gi = jnp.sum(jnp.where(glane == hd, gc, 0.0), axis=1, keepdims=True)
    gf = jnp.sum(jnp.where(glane == pl.num_programs(1) + hd, gc, 0.0), axis=1, keepdims=True)
    lic_scr[...] = _softcap(gi + bi)
    lfc_scr[...] = jax.nn.log_sigmoid(_softcap(gf + bf))
    gt_scr[...] = gc.T
    li_row = _softcap(gt_scr[pl.ds(hd, 1), :] + bi)
    lf_row = jax.nn.log_sigmoid(_softcap(gt_scr[pl.ds(pl.num_programs(1) + hd, 1), :] + bf))
    for c in range(nc):
        lir_scr[c:c + 1, :] = li_row[:, c * L:(c + 1) * L]
        lfr_scr[c:c + 1, :] = lf_row[:, c * L:(c + 1) * L]

    c_scr[...] = jnp.zeros(c_scr.shape, F32)
    m_scr[...] = jnp.zeros(m_scr.shape, F32)

    r_i = lax.broadcasted_iota(jnp.int32, (L, L), 0)
    c_i = lax.broadcasted_iota(jnp.int32, (L, L), 1)
    tril = (c_i <= r_i)
    tril_b = tril.astype(BF16)
    triu_b = (r_i <= c_i).astype(BF16)
    ones_col = (lax.broadcasted_iota(jnp.int32, (L, LANES), 1) == 0).astype(BF16)
    nrm = nrm_ref[...]

    def split3(x):
        hi = x.astype(BF16)
        r = x - hi.astype(F32)
        mid = r.astype(BF16)
        return hi, mid, (r - mid.astype(F32)).astype(BF16)

    for c in range(nc):
        b_c = None
        for piece in split3(lfc_scr[c * L:(c + 1) * L, :]):
            term = jnp.dot(tril_b, jnp.broadcast_to(piece, (L, L)), preferred_element_type=F32)
            b_c = term if b_c is None else b_c + term
        b_r = None
        for piece in split3(lfr_scr[c:c + 1, :]):
            term = jnp.dot(jnp.broadcast_to(piece, (L, L)), triu_b, preferred_element_type=F32)
            b_r = term if b_r is None else b_r + term
        bc_scr[c] = b_c
        br_scr[c] = b_r

    def body(c):
        sl = slice(c * L, (c + 1) * L)
        qc = q_scr[sl, :]
        kc = k_scr[sl, :]
        vc = v_ref[sl, :].astype(BF16)
        lic = lic_scr[sl, :]
        lir = lir_scr[c:c + 1, :]
        m = m_scr[...]

        b_c = bc_scr[c]
        b_r = br_scr[c]
        dmat = jnp.where(tril, b_c - b_r + lir, NEG_INF)
        bcol = b_c[:, 0:1]
        inter = bcol + m
        m_j = jnp.maximum(inter, jnp.max(dmat, axis=-1, keepdims=True))
        w_intra = jnp.exp(dmat - m_j)
        w_inter = jnp.exp(inter - m_j)

        sqk = lax.dot_general(qc, kc.astype(BF16), (((1,), (1,)), ((), ())), preferred_element_type=F32)
        qk = sqk * w_intra
        q_c = jnp.dot(qc, c_scr[...].astype(BF16), preferred_element_type=F32)
        num = w_inter * q_c[:, :ML_V] + jnp.dot(qk.astype(BF16), vc, preferred_element_type=F32)
        den = w_inter * q_c[:, ML_V:ML_V + 1] + jnp.sum(qk, axis=-1, keepdims=True)
        hh = num / jnp.maximum(jnp.abs(den), jnp.exp(-m_j))
        og = og_ref[sl, :]
        o_ref[sl, :] = (_rms(hh, nrm) * jax.nn.sigmoid(og)).astype(o_ref.dtype)

        b_l = b_c[L - 1:L, 0:1]
        logw = b_l - bcol + lic
        m_new = jnp.maximum(b_l + m, jnp.max(logw, axis=0, keepdims=True))
        decay = jnp.exp(b_l + m - m_new)
        ws = jnp.exp(logw - m_new)
        kw = (kc * ws).astype(BF16)
        vext = jnp.concatenate([vc, ones_col], axis=1)
        upd = lax.dot_general(kw, vext, (((0,), (0,)), ((), ())), preferred_element_type=F32)
        c_scr[...] = decay * c_scr[...] + upd
        m_scr[...] = m_new

    for c in range(nc):
        body(c)


def _mlstm(pm, gates, conv_w, conv_b, b_ig, b_fg, ml_norm, batch, seq, heads):
    t = pm.shape[0]
    nc = seq // ML_CHUNK
    qk_w = heads * ML_QK
    vblk = 2 * qk_w // ML_V
    smem = pl.BlockSpec(memory_space=pltpu.SMEM)
    return pl.pallas_call(
        _mlstm_kernel,
        grid=(batch, heads),
        in_specs=[smem, smem,
                  pl.BlockSpec((seq, ML_QK), lambda b, h: (b, h)),
                  pl.BlockSpec((seq, ML_QK), lambda b, h: (b, heads + h)),
                  pl.BlockSpec((seq, ML_V), lambda b, h: (b, vblk + h)),
                  pl.BlockSpec((seq, ML_V), lambda b, h: (b, vblk + heads + h)),
                  pl.BlockSpec((CONV_W, ML_QK), lambda b, h: (0, h)),
                  pl.BlockSpec((CONV_W, ML_QK), lambda b, h: (0, heads + h)),
                  pl.BlockSpec((1, ML_QK), lambda b, h: (0, h)),
                  pl.BlockSpec((1, ML_QK), lambda b, h: (0, heads + h)),
                  pl.BlockSpec((seq, LANES), lambda b, h: (b, 0)),
                  pl.BlockSpec((1, ML_V), lambda b, h: (0, 0))],
        out_specs=pl.BlockSpec((seq, ML_V), lambda b, h: (b, h)),
        out_shape=jax.ShapeDtypeStruct((t, heads * ML_V), BF16),
        scratch_shapes=[pltpu.VMEM((seq, ML_QK), BF16), pltpu.VMEM((seq, ML_QK), F32),
                        pltpu.VMEM((seq, 1), F32), pltpu.VMEM((seq, 1), F32),
                        pltpu.VMEM((nc, ML_CHUNK), F32), pltpu.VMEM((nc, ML_CHUNK), F32),
                        pltpu.VMEM((LANES, seq), F32),
                        pltpu.VMEM((nc, ML_CHUNK, ML_CHUNK), F32), pltpu.VMEM((nc, ML_CHUNK, ML_CHUNK), F32),
                        pltpu.VMEM((ML_QK, ML_V + LANES), F32), pltpu.VMEM((1, 1), F32)],
        compiler_params=_cparams(("arbitrary", "arbitrary")),
        name="mlstm",
    )(b_ig, b_fg, pm, pm, pm, pm, conv_w, conv_w, conv_b.reshape(1, -1), conv_b.reshape(1, -1),
      gates, ml_norm.reshape(1, ML_V))


def _outproj_kernel(oa_ref, om_ref, w_ref, x_ref, mod_ref, g_ref, x1_ref, h2_ref):
    ka = oa_ref.shape[1]
    for r in range(0, OUT_TM, OUT_SUB):
        rows = slice(r, r + OUT_SUB)
        mixed = (jnp.dot(oa_ref[rows, :], w_ref[:ka, :], preferred_element_type=F32)
                 + jnp.dot(om_ref[rows, :], w_ref[ka:, :], preferred_element_type=F32))
        x1 = x_ref[rows, :] + mod_ref[0, 2:3, :] * mixed
        x1_ref[rows, :] = x1
        h2 = _rms(x1, g_ref[...]) * (1.0 + mod_ref[0, 4:5, :]) + mod_ref[0, 3:4, :]
        h2_ref[rows, :] = h2.astype(BF16)


def _outproj(oa, om, w_out, x2, mod3, g_ffn, seq):
    t, d = x2.shape
    tiles_per_seq = seq // OUT_TM
    return pl.pallas_call(
        _outproj_kernel,
        grid=(t // OUT_TM,),
        in_specs=[pl.BlockSpec((OUT_TM, oa.shape[1]), lambda i: (i, 0)),
                  pl.BlockSpec((OUT_TM, om.shape[1]), lambda i: (i, 0)),
                  pl.BlockSpec(w_out.shape, lambda i: (0, 0)),
                  pl.BlockSpec((OUT_TM, d), lambda i: (i, 0)),
                  pl.BlockSpec((1, 6, d), lambda i: (i // tiles_per_seq, 0, 0)),
                  pl.BlockSpec((1, d), lambda i: (0, 0))],
        out_specs=[pl.BlockSpec((OUT_TM, d), lambda i: (i, 0)),
                   pl.BlockSpec((OUT_TM, d), lambda i: (i, 0))],
        out_shape=[jax.ShapeDtypeStruct((t, d), F32), jax.ShapeDtypeStruct((t, d), BF16)],
        compiler_params=_cparams(("arbitrary",)),
        name="outproj",
    )(oa, om, w_out, x2, mod3, g_ffn.reshape(1, d))


_CAND_ROWS = 16 + 8 + 6 * 8 + 8


def _pick_max(work, iota, exact):
    mx = jnp.max(work, axis=0, keepdims=True)
    if not exact:
        return mx, work == mx
    first = jnp.min(jnp.where(work == mx, iota, float(work.shape[0])), axis=0, keepdims=True)
    return mx, iota == first


def _rank16(s, top_scr, exact, want_rank=True):
    iota = lax.broadcasted_iota(jnp.int32, s.shape, 0).astype(F32) if exact else None
    rank = jnp.full(s.shape, float(PEER_TOPK), F32) if want_rank else None
    for r in range(PEER_TOPK):
        mx, sel = _pick_max(s, iota, exact)
        if want_rank:
            rank = jnp.where(sel, float(r), rank)
        s = jnp.where(sel, NEG_INF, s)
        top_scr[r:r + 1, :] = mx
    return rank, s


def _selection_tables(s1, s2, t1_scr, t2_scr, exact):
    k = PEER_TOPK
    r1, left1 = _rank16(s1, t1_scr, exact, want_rank=exact)
    r2, _ = _rank16(s2, t2_scr, exact)
    t1 = t1_scr[...]
    t2 = t2_scr[...]
    tm = s1.shape[1]

    brow = lax.broadcasted_iota(jnp.int32, (SUBLANES, tm), 0)
    pieces = [t1[0:1] + t2, t1[1:2] + t2[0:SUBLANES]]
    for a in range(2, SUBLANES):
        pieces.append(jnp.where(brow < k // (a + 1), t1[a:a + 1] + t2[0:SUBLANES], NEG_INF))
    pieces.append(t1[SUBLANES:] + t2[0:1])
    cand = jnp.concatenate(pieces, axis=0)
    cmax = cand[0:1]

    work = cand
    iota = lax.broadcasted_iota(jnp.int32, cand.shape, 0).astype(F32) if exact else None
    for _ in range(k):
        _, sel = _pick_max(work, iota, exact)
        work = jnp.where(sel, NEG_INF, work)
    taken = jnp.where(work != cand, 1.0, 0.0)
    z = jnp.sum(taken * jnp.exp(cand - cmax), axis=0, keepdims=True)

    n_rows = [jnp.sum(taken[0:16], axis=0, keepdims=True)]
    for a in range(1, SUBLANES):
        lo = 16 + (a - 1) * SUBLANES
        n_rows.append(jnp.sum(taken[lo:lo + SUBLANES], axis=0, keepdims=True))
    base = 16 + 7 * SUBLANES
    for a in range(SUBLANES, k):
        n_rows.append(taken[base + a - SUBLANES:base + a - SUBLANES + 1])
    n1 = jnp.zeros(s1.shape, F32)
    for a in range(k):
        hit = (r1 == float(a)) if exact else (s1 == t1[a:a + 1])
        n1 = jnp.where(hit, n_rows[a], n1)

    extra = None
    if not exact:
        def count(x):
            return jnp.sum(x, axis=0, keepdims=True) - float(k)
        extra = (count(jnp.where(left1 != s1, 1.0, 0.0)) + count(jnp.where(r2 < float(k), 1.0, 0.0))
                 + count(taken))
    return (r2, jnp.exp(s2 - t2[0:1]), n1, jnp.exp(s1 - t1[0:1]) / z), extra


def _peer_sel_kernel(h2_ref, w_ref, keys_ref, u_ref, v_ref, r2_ref, e2_ref, n1_ref, e1_ref, ub_ref, vtb_ref,
                     t1_scr, t2_scr):
    ub_ref[...] = u_ref[...].astype(BF16)
    vtb_ref[...] = v_ref[...].T.astype(BF16)

    half = PEER_NKEYS
    q = jnp.dot(h2_ref[...], w_ref[...], preferred_element_type=F32).astype(BF16)
    nt = (((1,), (1,)), ((), ()))
    s1 = lax.dot_general(keys_ref[0, 0], q[:, :half], nt, preferred_element_type=F32)
    s2 = lax.dot_general(keys_ref[0, 1], q[:, half:], nt, preferred_element_type=F32)

    def store(tables):
        for ref, val in zip((r2_ref, e2_ref, n1_ref, e1_ref), tables):
            ref[0] = val.astype(ref.dtype)

    tables, extra = _selection_tables(s1, s2, t1_scr, t2_scr, exact=False)
    store(tables)

    @pl.when(jnp.max(extra) > 0.0)
    def _():
        store(_selection_tables(s1, s2, t1_scr, t2_scr, exact=True)[0])


def _peer_sel(h2, w_pq, keys, peer_u, peer_v):
    t, d = h2.shape
    ne = peer_u.shape[0]
    nh = PEER_HEADS
    qd = 2 * PEER_NKEYS
    steps = (t // SEL_TM) * nh
    rows = ne // steps
    assert rows * steps == ne and rows % LANES == 0
    tab = jax.ShapeDtypeStruct((nh, PEER_NKEYS, t), F32)
    tab16 = jax.ShapeDtypeStruct((nh, PEER_NKEYS, t), BF16)
    tab_spec = pl.BlockSpec((1, PEER_NKEYS, SEL_TM), lambda i, h: (h, 0, i))
    return pl.pallas_call(
        _peer_sel_kernel,
        grid=(t // SEL_TM, nh),
        in_specs=[pl.BlockSpec((SEL_TM, d), lambda i, h: (i, 0)),
                  pl.BlockSpec((d, qd), lambda i, h: (0, h)),
                  pl.BlockSpec((1, 2, PEER_NKEYS, PEER_NKEYS), lambda i, h: (h, 0, 0, 0)),
                  pl.BlockSpec((rows, d), lambda i, h: (i * nh + h, 0)),
                  pl.BlockSpec((rows, d), lambda i, h: (i * nh + h, 0))],
        out_specs=[tab_spec, tab_spec, tab_spec, tab_spec,
                   pl.BlockSpec((rows, d), lambda i, h: (i * nh + h, 0)),
                   pl.BlockSpec((d, rows), lambda i, h: (0, i * nh + h))],
        out_shape=[tab16, tab16, tab, tab,
                   jax.ShapeDtypeStruct((ne, d), BF16), jax.ShapeDtypeStruct((d, ne), BF16)],
        scratch_shapes=[pltpu.VMEM((PEER_TOPK, SEL_TM), F32), pltpu.VMEM((PEER_TOPK, SEL_TM), F32)],
        compiler_params=_cparams(("arbitrary", "arbitrary")),
        name="peer_sel",
    )(h2, w_pq, keys, peer_u, peer_v)


def _peer_ffn_kernel(h2_ref, u0_ref, uodd_ref, unext_ref, vt_ref, r2_ref, e2_ref, n1_ref, e1_ref, x1_ref,
                     mod_ref, modo_ref, g_ref, o_ref, acc_scr, a0_scr, a1_scr):
    c = pl.program_id(1)
    nk = PEER_NKEYS
    per = FFN_EC // nk
    nt = (((1,), (1,)), ((), ()))
    zero = jnp.zeros((), BF16)

    def preact(u_ref, dst_scr):
        dst_scr[...] = lax.dot_general(u_ref[...], h2_ref[...], nt, preferred_element_type=F32)

    def weights(a_scr, chunk):
        parts = []
        for ii in range(per):
            i = chunk * per + ii
            a = a_scr[ii * nk:(ii + 1) * nk, :]
            gsum = None
            for hd in range(PEER_HEADS):
                n1 = n1_ref[hd, pl.ds(i, 1), :].astype(BF16)
                e1 = e1_ref[hd, pl.ds(i, 1), :].astype(BF16)
                g = jnp.where(r2_ref[hd] < n1, e2_ref[hd], zero) * e1
                gsum = g if gsum is None else gsum + g
            act = 0.5 * a * (1.0 + lax.erf(a * (2.0 ** -0.5)))
            parts.append(gsum * act.astype(BF16))
        return jnp.concatenate(parts, axis=0)

    def accumulate(w, col0):
        acc_scr[...] += jnp.dot(vt_ref[:, col0:col0 + FFN_EC], w, preferred_element_type=F32)

    @pl.when(c == 0)
    def _():
        acc_scr[...] = jnp.zeros(acc_scr.shape, F32)
        preact(u0_ref, a0_scr)

    preact(uodd_ref, a1_scr)
    accumulate(weights(a0_scr, 2 * c), 0)
    preact(unext_ref, a0_scr)
    accumulate(weights(a1_scr, 2 * c + 1), FFN_EC)

    @pl.when(c == pl.num_programs(1) - 1)
    def _():
        y = acc_scr[...].T
        x2 = x1_ref[...] + mod_ref[0, 5:6, :] * y
        o_ref[...] = _rms(x2, g_ref[...]) * (1.0 + modo_ref[0, 1:2, :]) + modo_ref[0, 0:1, :]


def _peer_ffn(h2, u, vt, tabs, x1, mod3, modo3, g_final, seq):
    t, d = h2.shape
    ne = u.shape[0]
    tiles_per_seq = seq // FFN_TB
    tab_spec = pl.BlockSpec((PEER_HEADS, PEER_NKEYS, FFN_TB), lambda i, c: (0, 0, i))
    nchunk = ne // FFN_EC
    once = pl.Buffered(1)
    return pl.pallas_call(
        _peer_ffn_kernel,
        grid=(t // FFN_TB, nchunk // 2),
        in_specs=[pl.BlockSpec((FFN_TB, d), lambda i, c: (i, 0)),
                  pl.BlockSpec((FFN_EC, d), lambda i, c: (0, 0), pipeline_mode=once),
                  pl.BlockSpec((FFN_EC, d), lambda i, c: (2 * c + 1, 0)),
                  pl.BlockSpec((FFN_EC, d), lambda i, c: (jnp.minimum(2 * c + 2, nchunk - 1), 0)),
                  pl.BlockSpec((d, 2 * FFN_EC), lambda i, c: (0, c)),
                  tab_spec, tab_spec, tab_spec, tab_spec,
                  pl.BlockSpec((FFN_TB, d), lambda i, c: (i, 0), pipeline_mode=once),
                  pl.BlockSpec((1, 6, d), lambda i, c: (i // tiles_per_seq, 0, 0)),
                  pl.BlockSpec((1, 2, d), lambda i, c: (i // tiles_per_seq, 0, 0)),
                  pl.BlockSpec((1, d), lambda i, c: (0, 0))],
        out_specs=pl.BlockSpec((FFN_TB, d), lambda i, c: (i, 0)),
        out_shape=jax.ShapeDtypeStruct((t, d), F32),
        scratch_shapes=[pltpu.VMEM((d, FFN_TB), F32), pltpu.VMEM((FFN_EC, FFN_TB), F32),
                        pltpu.VMEM((FFN_EC, FFN_TB), F32)],
        compiler_params=pltpu.CompilerParams(dimension_semantics=("arbitrary", "arbitrary"),
                                             vmem_limit_bytes=FFN_VMEM_LIMIT),
        name="peer_ffn",
    )(h2, u, u, u, vt, *tabs, x1, mod3, modo3, g_final.reshape(1, d))


def kernel(x, c, w_ada, b_ada, g_mix, w_in, conv_w, conv_b, b_igate, b_fgate, lambda_q1, lambda_k1,
           lambda_q2, lambda_k2, da_norm, ml_norm, w_out, g_ffn, w_pq, sub_keys, peer_u, peer_v,
           w_ada_final, b_ada_final, g_final):
    batch, seq, d = x.shape
    depth = w_ada.shape[0]
    t = batch * seq
    da_heads = d // 256
    ml_heads = d // 512
    att_cols = 3 * da_heads * DA_V
    rope_cols = 2 * da_heads * DA_V
    main_cols = att_cols + 2 * ml_heads * ML_QK + 2 * ml_heads * ML_V
    assert batch <= SUBLANES and seq % PROJ_TM == 0 and seq % ATT_TQ == 0 and seq % ML_CHUNK == 0
    assert w_in.shape[2] == main_cols + 2 * ml_heads

    c8 = jnp.zeros((SUBLANES, d), F32).at[:batch].set(c.astype(F32))
    modo3 = _ada(c8, w_ada_final, b_ada_final)[:batch].reshape(batch, 2, d)
    xt = x.reshape(t, d)

    for l in range(depth):
        mod3 = _ada(c8, w_ada[l], b_ada[l])[:batch].reshape(batch, 6, d)
        lam_init = 0.8 - 0.6 * math.exp(-0.3 * l)

        w_in_t = jnp.swapaxes(w_in[l], 0, 1).astype(BF16)
        w_gate_t = jnp.zeros((LANES, d), BF16).at[:2 * ml_heads].set(w_in_t[main_cols:])
        pa, pm, gates = _proj(xt, mod3, g_mix[l], w_in_t, w_gate_t, seq, main_cols, att_cols, rope_cols)

        lam4 = jnp.stack([lambda_q1[l], lambda_k1[l], lambda_q2[l], lambda_k2[l]]).astype(F32)
        oa = _attn(pa, lam4, da_norm[l], batch, seq, da_heads, lam_init)

        om = _mlstm(pm, gates, conv_w[l], conv_b[l], b_igate[l], b_fgate[l], ml_norm[l], batch, seq, ml_heads)

        x1, h2 = _outproj(oa, om, w_out[l].astype(BF16), xt, mod3, g_ffn[l], seq)

        *tabs, u_b, vt_b = _peer_sel(h2, w_pq[l].astype(BF16), sub_keys[l].astype(BF16), peer_u[l], peer_v[l])
        assert depth == 1
        xt = _peer_ffn(h2, u_b, vt_b, tabs, x1, mod3, modo3, g_final, seq)

    return xt.reshape(batch, seq, d)
```
